```python
import jax, jax.numpy as jnp
from jax import lax
import numpy as np

D_MODEL = 2048
BATCH = 2
SEQ = 16384
DEPTH = 1

D_MIX = D_MODEL
D_CONV_A = D_MIX // 2
D_CONV_B = D_MIX - D_CONV_A
HEAD_DIM = 128
N_HEADS_A = D_CONV_A // HEAD_DIM
N_GROUPS_B = D_CONV_B // HEAD_DIM
KERNEL_A = 3
KERNEL_B = 31
D_IN_PROJ = 3 * D_CONV_A + 2 * D_CONV_B
N_GROUPS = 8
EXPERTS_PER_GROUP = 8
N_EXPERTS = N_GROUPS * EXPERTS_PER_GROUP
TOP_K = 2
D_EXPERT = D_MODEL // 2
BLOCK = 128
N_MOD = 6
EPS = 1e-6

kernel_name = "hybrid_shortconv_conformer_hmoe_block"


def rms_norm(x, g):
    xf = x.astype(jnp.float32)
    y = xf * lax.rsqrt(jnp.mean(xf * xf, axis=-1, keepdims=True) + EPS)
    return (y * g.astype(jnp.float32)).astype(x.dtype)


def layer_norm(x, g, b):
    xf = x.astype(jnp.float32)
    mu = jnp.mean(xf, axis=-1, keepdims=True)
    var = jnp.mean(jnp.square(xf - mu), axis=-1, keepdims=True)
    y = (xf - mu) * lax.rsqrt(var + EPS)
    return (y * g.astype(jnp.float32) + b.astype(jnp.float32)).astype(x.dtype)


def causal_dwconv(x, w):
    k, ch = w.shape
    return lax.conv_general_dilated(
        x, w[:, None, :].astype(x.dtype), window_strides=(1,), padding=[(k - 1, 0)],
        dimension_numbers=("NWC", "WIO", "NWC"), feature_group_count=ch)


def modulate(h, shift, scale):
    return h * (1.0 + scale[:, None, :]) + shift[:, None, :]


def token_mixer(h, w_in, conv_a, norm_a_out, conv_b, conv_b_bias, ln_b_gain, ln_b_bias, w_out):
    bsz, t, _ = h.shape
    proj = jnp.einsum("btd,de->bte", h, w_in)
    b_a, c_a, v_a, val_b, gate_b = jnp.split(
        proj, [D_CONV_A, 2 * D_CONV_A, 3 * D_CONV_A, 3 * D_CONV_A + D_CONV_B], axis=-1)
    y_a = b_a * causal_dwconv(c_a * v_a, conv_a)
    y_a = rms_norm(y_a.reshape(bsz, t, N_HEADS_A, HEAD_DIM),
                   norm_a_out.reshape(N_HEADS_A, HEAD_DIM)).reshape(bsz, t, D_CONV_A)
    u = val_b * jax.nn.sigmoid(gate_b)
    u = causal_dwconv(u, conv_b) + conv_b_bias
    y_b = jax.nn.silu(layer_norm(u, ln_b_gain, ln_b_bias))
    y = jnp.concatenate([y_a, y_b], axis=-1)
    return jnp.einsum("bte,ed->btd", y, w_out)


def hierarchical_moe(h, w_rg, b_rg, w_re, b_re, w_eg, w_eu, w_ed):
    n_tok, d = h.shape
    coarse = (h @ w_rg + b_rg).astype(jnp.float32)
    p_coarse = jax.nn.softmax(coarse, axis=-1)
    g_sel = jnp.argmax(coarse, axis=-1)
    p_g = jnp.take_along_axis(p_coarse, g_sel[:, None], axis=-1)
    fine_all = jnp.einsum("nd,gde->nge", h, w_re) + b_re
    fine = jnp.take_along_axis(fine_all, g_sel[:, None, None], axis=1)[:, 0].astype(jnp.float32)
    top_v, top_i = lax.top_k(fine, TOP_K)
    gate_w = p_g * jax.nn.softmax(top_v, axis=-1)
    expert = g_sel[:, None] * EXPERTS_PER_GROUP + top_i

    n_assign = n_tok * TOP_K
    flat_e = expert.reshape(-1).astype(jnp.int32)
    flat_tok = jnp.repeat(jnp.arange(n_tok, dtype=jnp.int32), TOP_K)
    flat_w = gate_w.reshape(-1).astype(h.dtype)
    order = jnp.argsort(flat_e)
    se = flat_e[order]
    counts = jnp.bincount(flat_e, length=N_EXPERTS).astype(jnp.int32)
    padded = (counts + BLOCK - 1) // BLOCK * BLOCK
    start = jnp.cumsum(counts) - counts
    pend = jnp.cumsum(padded)
    pstart = pend - padded
    dest = pstart[se] + (jnp.arange(n_assign, dtype=jnp.int32) - start[se])
    n_blocks = (n_assign + N_EXPERTS * (BLOCK - 1) + BLOCK - 1) // BLOCK
    cap = n_blocks * BLOCK
    buf_tok = jnp.full((cap,), n_tok, jnp.int32).at[dest].set(flat_tok[order])
    buf_w = jnp.zeros((cap,), h.dtype).at[dest].set(flat_w[order])
    block_start = jnp.arange(n_blocks, dtype=jnp.int32) * BLOCK
    block_e = jnp.minimum(jnp.searchsorted(pend, block_start, side="right"), N_EXPERTS - 1)
    h_pad = jnp.concatenate([h, jnp.zeros((1, d), h.dtype)], axis=0)

    def expert_block(args):
        tok, w, e = args
        xb = h_pad[tok]
        hid = jax.nn.silu(xb @ w_eg[e]) * (xb @ w_eu[e])
        return (hid @ w_ed[e]) * w[:, None]

    ys = lax.map(expert_block, (buf_tok.reshape(n_blocks, BLOCK),
                                buf_w.reshape(n_blocks, BLOCK), block_e))
    out = jax.ops.segment_sum(ys.reshape(cap, d), buf_tok, num_segments=n_tok + 1)
    return out[:n_tok]


def setup_inputs(seed: int = 0) -> dict:
    key = jax.random.key(seed)
    ks = jax.random.split(key, 24)
    nrm = jax.random.normal
    L, D = DEPTH, D_MODEL
    return {
        "x": nrm(ks[0], (BATCH, SEQ, D), jnp.float32),
        "c": nrm(ks[1], (BATCH, D), jnp.float32),
        "w_ada": nrm(ks[2], (L, D, N_MOD * D), jnp.float32) * (0.5 * D ** -0.5),
        "b_ada": nrm(ks[3], (L, N_MOD * D), jnp.float32) * 0.01,
        "norm_mix_pre": 1.0 + 0.02 * nrm(ks[4], (L, D), jnp.float32),
        "norm_mix_post": 1.0 + 0.02 * nrm(ks[5], (L, D), jnp.float32),
        "w_in": nrm(ks[6], (L, D, D_IN_PROJ), jnp.float32) * D ** -0.5,
        "conv_a": nrm(ks[7], (L, KERNEL_A, D_CONV_A), jnp.float32) * KERNEL_A ** -0.5,
        "norm_a_out": 1.0 + 0.02 * nrm(ks[8], (L, D_CONV_A), jnp.float32),
        "conv_b": nrm(ks[9], (L, KERNEL_B, D_CONV_B), jnp.float32) * KERNEL_B ** -0.5,
        "conv_b_bias": nrm(ks[10], (L, D_CONV_B), jnp.float32) * 0.01,
        "ln_b_gain": 1.0 + 0.02 * nrm(ks[11], (L, D_CONV_B), jnp.float32),
        "ln_b_bias": nrm(ks[12], (L, D_CONV_B), jnp.float32) * 0.01,
        "w_out": nrm(ks[13], (L, D_MIX, D), jnp.float32) * D_MIX ** -0.5,
        "norm_moe_pre": 1.0 + 0.02 * nrm(ks[14], (L, D), jnp.float32),
        "norm_moe_post": 1.0 + 0.02 * nrm(ks[15], (L, D), jnp.float32),
        "w_router_group": nrm(ks[16], (L, D, N_GROUPS), jnp.float32) * D ** -0.5,
        "b_router_group": nrm(ks[17], (L, N_GROUPS), jnp.float32) * 0.01,
        "w_router_expert": nrm(ks[18], (L, N_GROUPS, D, EXPERTS_PER_GROUP), jnp.float32) * D ** -0.5,
        "b_router_expert": nrm(ks[19], (L, N_GROUPS, EXPERTS_PER_GROUP), jnp.float32) * 0.01,
        "w_expert_gate": nrm(ks[20], (L, N_EXPERTS, D, D_EXPERT), jnp.float32) * D ** -0.5,
        "w_expert_up": nrm(ks[21], (L, N_EXPERTS, D, D_EXPERT), jnp.float32) * D ** -0.5,
        "w_expert_down": nrm(ks[22], (L, N_EXPERTS, D_EXPERT, D), jnp.float32) * D_EXPERT ** -0.5,
    }


def reference(x, c, w_ada, b_ada, norm_mix_pre, norm_mix_post, w_in, conv_a, norm_a_out,
              conv_b, conv_b_bias, ln_b_gain, ln_b_bias, w_out, norm_moe_pre, norm_moe_post,
              w_router_group, b_router_group, w_router_expert, b_router_expert,
              w_expert_gate, w_expert_up, w_expert_down):
    bsz, t, d = x.shape
    c_act = jax.nn.silu(c)
    for l in range(DEPTH):
        mod = c_act @ w_ada[l] + b_ada[l]
        shift1, scale1, gate1, shift2, scale2, gate2 = jnp.split(mod, N_MOD, axis=-1)
        h = modulate(rms_norm(x, norm_mix_pre[l]), shift1, scale1)
        y = token_mixer(h, w_in[l], conv_a[l], norm_a_out[l], conv_b[l], conv_b_bias[l],
                        ln_b_gain[l], ln_b_bias[l], w_out[l])
        x = x + gate1[:, None, :] * rms_norm(y, norm_mix_post[l])
        h = modulate(rms_norm(x, norm_moe_pre[l]), shift2, scale2)
        y = hierarchical_moe(h.reshape(bsz * t, d), w_router_group[l], b_router_group[l],
                             w_router_expert[l], b_router_expert[l], w_expert_gate[l],
                             w_expert_up[l], w_expert_down[l]).reshape(bsz, t, d)
        x = x + gate2[:, None, :] * rms_norm(y, norm_moe_post[l])
    return x
```

```python
import functools

import jax
import jax.numpy as jnp
from jax import lax
from jax.experimental import pallas as pl
from jax.experimental.pallas import tpu as pltpu

EPS = 1e-6
HEAD_DIM = 128
KERNEL_A = 3
KERNEL_B = 31
N_MOD = 6
TOP_K = 2
ROUTE_LANES = 128
HIST_A = 8
HIST_B = 32

TM_MIX = 256
TM_EXP = 256
TM_CMB = 256
ADA_COLS = 1024
VMEM_LIMIT = 56 * 1024 * 1024

_BF16 = jnp.bfloat16
_F32 = jnp.float32


def _dot(a, b):
    return jnp.dot(a, b, preferred_element_type=_F32)


def _rms(v):
    return lax.rsqrt(jnp.mean(v * v, axis=-1, keepdims=True) + EPS)


def _ada_kernel(c_ref, w_ref, b_ref, o_ref):
    c = c_ref[...]
    act = (c * jax.nn.sigmoid(c)).astype(_BF16)
    o_ref[...] = _dot(act, w_ref[...].astype(_BF16)) + b_ref[...]


def _ada(c, w_ada, b_ada):
    bsz, d = c.shape
    n_out = w_ada.shape[1]
    c_pad = jnp.zeros((8, d), _F32).at[:bsz].set(c)
    out = pl.pallas_call(
        _ada_kernel,
        grid=(n_out // ADA_COLS,),
        in_specs=[
            pl.BlockSpec((8, d), lambda j: (0, 0)),
            pl.BlockSpec((d, ADA_COLS), lambda j: (0, j)),
            pl.BlockSpec((1, ADA_COLS), lambda j: (0, j)),
        ],
        out_specs=pl.BlockSpec((8, ADA_COLS), lambda j: (0, j)),
        out_shape=jax.ShapeDtypeStruct((8, n_out), _F32),
        compiler_params=pltpu.CompilerParams(
            dimension_semantics=("arbitrary",), vmem_limit_bytes=VMEM_LIMIT),
        name="ada",
    )(c_pad, w_ada, b_ada.reshape(1, n_out))
    return out[:bsz]


def _mix_in_kernel(x_ref, sh_ref, sc_ref, g_ref, w_ref, ca_ref, na_ref, cb_ref, cbb_ref,
                   lg_ref, lb_ref, y_ref, cv_ext, u_ext, *, tm, da, db):
    @pl.when(pl.program_id(1) == 0)
    def _():
        cv_ext[0:HIST_A, :] = jnp.zeros((HIST_A, da), _F32)
        u_ext[0:HIST_B, :] = jnp.zeros((HIST_B, db), _F32)

    x = x_ref[0]
    h = x * _rms(x) * g_ref[...]
    h = h * (1.0 + sc_ref[0]) + sh_ref[0]
    hb = h.astype(_BF16)

    b_a = _dot(hb, w_ref[:, 0:da])
    c_a = _dot(hb, w_ref[:, da:2 * da])
    v_a = _dot(hb, w_ref[:, 2 * da:3 * da])
    cv_ext[HIST_A:HIST_A + tm, :] = c_a * v_a
    conv = jnp.zeros((tm, da), _F32)
    for k in range(KERNEL_A):
        off = HIST_A - (KERNEL_A - 1) + k
        conv = conv + ca_ref[k:k + 1, :] * cv_ext[off:off + tm, :]
    cv_ext[0:HIST_A, :] = cv_ext[tm:tm + HIST_A, :]
    y_a = b_a * conv
    for hd in range(da // HEAD_DIM):
        lo, hi = hd * HEAD_DIM, (hd + 1) * HEAD_DIM
        seg = y_a[:, lo:hi]
        y_ref[0, :, lo:hi] = (seg * _rms(seg) * na_ref[:, lo:hi]).astype(_BF16)

    val = _dot(hb, w_ref[:, 3 * da:3 * da + db])
    gate = _dot(hb, w_ref[:, 3 * da + db:3 * da + 2 * db])
    u_ext[HIST_B:HIST_B + tm, :] = val * jax.nn.sigmoid(gate)
    acc = jnp.zeros((tm, db), _F32) + cbb_ref[...]
    for k in range(KERNEL_B):
        off = HIST_B - (KERNEL_B - 1) + k
        acc = acc + cb_ref[k:k + 1, :] * u_ext[off:off + tm, :]
    u_ext[0:HIST_B, :] = u_ext[tm:tm + HIST_B, :]
    mu = jnp.mean(acc, axis=-1, keepdims=True)
    cen = acc - mu
    var = jnp.mean(cen * cen, axis=-1, keepdims=True)
    yn = cen * lax.rsqrt(var + EPS) * lg_ref[...] + lb_ref[...]
    y_ref[0, :, da:da + db] = (yn * jax.nn.sigmoid(yn)).astype(_BF16)


def _mix_in(x, shift, scale, g_pre, w_in_bf, conv_a, norm_a, conv_b, conv_b_bias, ln_g, ln_b):
    bsz, t, d = x.shape
    da = conv_a.shape[1]
    db = conv_b.shape[1]
    tm = TM_MIX
    full = lambda shape: pl.BlockSpec(shape, lambda b, s: (0,) * len(shape))
    per_b = pl.BlockSpec((1, 1, d), lambda b, s: (b, 0, 0))
    return pl.pallas_call(
        functools.partial(_mix_in_kernel, tm=tm, da=da, db=db),
        grid=(bsz, t // tm),
        in_specs=[
            pl.BlockSpec((1, tm, d), lambda b, s: (b, s, 0)),
            per_b, per_b,
            full((1, d)),
            pl.BlockSpec(w_in_bf.shape, lambda b, s: (0, 0), pipeline_mode=pl.Buffered(1)),
            full((KERNEL_A, da)), full((1, da)),
            full((KERNEL_B, db)), full((1, db)), full((1, db)), full((1, db)),
        ],
        out_specs=pl.BlockSpec((1, tm, da + db), lambda b, s: (b, s, 0)),
        out_shape=jax.ShapeDtypeStruct((bsz, t, da + db), _BF16),
        scratch_shapes=[pltpu.VMEM((HIST_A + tm, da), _F32), pltpu.VMEM((HIST_B + tm, db), _F32)],
        compiler_params=pltpu.CompilerParams(
            dimension_semantics=("arbitrary", "arbitrary"), vmem_limit_bytes=VMEM_LIMIT),
        name="mix_in",
    )(x, shift, scale, g_pre, w_in_bf, conv_a, norm_a, conv_b, conv_b_bias, ln_g, ln_b)


def _mix_out_kernel(y_ref, x_ref, g1_ref, sh2_ref, sc2_ref, npost_ref, npre_ref, wout_ref, wr_ref,
                    br_ref, x1_ref, h2_ref, route_ref, *, n_groups, per_group):
    o = _dot(y_ref[0], wout_ref[...])
    x1 = x_ref[0] + g1_ref[0] * (o * _rms(o) * npost_ref[...])
    x1_ref[0] = x1
    h2 = x1 * _rms(x1) * npre_ref[...]
    h2 = h2 * (1.0 + sc2_ref[0]) + sh2_ref[0]
    h2_ref[0] = h2

    logits = _dot(h2.astype(_BF16), wr_ref[...]) + br_ref[...]
    lane = lax.broadcasted_iota(jnp.int32, logits.shape, 1).astype(_F32)
    neg = jnp.float32(-jnp.inf)
    big = jnp.float32(ROUTE_LANES)
    is_coarse = lane < n_groups
    coarse = jnp.where(is_coarse, logits, neg)
    m = jnp.max(coarse, axis=-1, keepdims=True)
    g_sel = jnp.min(jnp.where(coarse == m, lane, big), axis=-1, keepdims=True)
    p_g = 1.0 / jnp.sum(jnp.where(is_coarse, jnp.exp(logits - m), 0.0), axis=-1, keepdims=True)
    lo = n_groups + g_sel * per_group
    fine = jnp.where((lane >= lo) & (lane < lo + per_group), logits, neg)
    v1 = jnp.max(fine, axis=-1, keepdims=True)
    l1 = jnp.min(jnp.where(fine == v1, lane, big), axis=-1, keepdims=True)
    fine2 = jnp.where(lane == l1, neg, fine)
    v2 = jnp.max(fine2, axis=-1, keepdims=True)
    l2 = jnp.min(jnp.where(fine2 == v2, lane, big), axis=-1, keepdims=True)
    e2 = jnp.exp(v2 - v1)
    w1 = p_g / (1.0 + e2)
    w2 = p_g * e2 / (1.0 + e2)
    route = jnp.where(lane == 0, l1 - n_groups,
                      jnp.where(lane == 1, l2 - n_groups,
                                jnp.where(lane == 2, w1, jnp.where(lane == 3, w2, 0.0))))
    route_ref[0] = route


def _mix_out(y, x, gate1, shift2, scale2, n_post, n_pre, w_out_bf, w_route, b_route, n_groups,
             per_group):
    bsz, t, d = x.shape
    tm = TM_MIX
    full = lambda shape: pl.BlockSpec(shape, lambda b, s: (0,) * len(shape))
    per_b = pl.BlockSpec((1, 1, d), lambda b, s: (b, 0, 0))
    tile = pl.BlockSpec((1, tm, d), lambda b, s: (b, s, 0))
    return pl.pallas_call(
        functools.partial(_mix_out_kernel, n_groups=n_groups, per_group=per_group),
        grid=(bsz, t // tm),
        in_specs=[
            tile, tile, per_b, per_b, per_b, full((1, d)), full((1, d)),
            pl.BlockSpec(w_out_bf.shape, lambda b, s: (0, 0), pipeline_mode=pl.Buffered(1)),
            full((d, ROUTE_LANES)), full((1, ROUTE_LANES)),
        ],
        out_specs=[tile, tile, pl.BlockSpec((1, tm, ROUTE_LANES), lambda b, s: (b, s, 0))],
        out_shape=[
            jax.ShapeDtypeStruct((bsz, t, d), _F32),
            jax.ShapeDtypeStruct((bsz, t, d), _F32),
            jax.ShapeDtypeStruct((bsz, t, ROUTE_LANES), _F32),
        ],
        compiler_params=pltpu.CompilerParams(
            dimension_semantics=("arbitrary", "arbitrary"), vmem_limit_bytes=VMEM_LIMIT),
        name="mix_out",
    )(y, x, gate1, shift2, scale2, n_post, n_pre, w_out_bf, w_route, b_route)


def _start_row_gather(src_hbm, idx_ref, dst, sem, n_rows):
    for r in range(n_rows):
        pltpu.make_async_copy(src_hbm.at[pl.ds(idx_ref[0, 0, r], 1)], dst.at[pl.ds(r, 1)], sem).start()


def _wait_row_gather(src_hbm, dst, sem, n_rows):
    pltpu.make_async_copy(src_hbm.at[pl.ds(0, n_rows)], dst, sem).wait()


def _expert_kernel(te_ref, tv_ref, tok_cur, tok_nxt, h_hbm, wg_ref, wu_ref, wd_ref, ys_ref, xbuf,
                   sem, *, tm):
    del te_ref
    i = pl.program_id(0)
    n = pl.num_programs(0)
    slot = i % 2
    valid = tv_ref[i] == 1

    @pl.when(jnp.logical_and(i == 0, valid))
    def _():
        _start_row_gather(h_hbm, tok_cur, xbuf.at[0], sem.at[0], tm)

    @pl.when(valid)
    def _():
        _wait_row_gather(h_hbm, xbuf.at[slot], sem.at[slot], tm)
        nxt = jnp.minimum(i + 1, n - 1)

        @pl.when(jnp.logical_and(i + 1 < n, tv_ref[nxt] == 1))
        def _():
            _start_row_gather(h_hbm, tok_nxt, xbuf.at[1 - slot], sem.at[1 - slot], tm)

        xb = xbuf[slot].astype(_BF16)
        g = _dot(xb, wg_ref[0])
        u = _dot(xb, wu_ref[0])
        hid = (g * jax.nn.sigmoid(g) * u).astype(_BF16)
        ys_ref[...] = _dot(hid, wd_ref[0])

    @pl.when(jnp.logical_not(valid))
    def _():
        ys_ref[...] = jnp.zeros(ys_ref.shape, _F32)


def _experts(h2_flat, tok_tiles, tile_e, tile_valid, wg_bf, wu_bf, wd_bf):
    n_tok, d = h2_flat.shape
    n_tiles = tok_tiles.shape[0]
    tm = TM_EXP
    f = wg_bf.shape[2]
    smem_tile = lambda imap: pl.BlockSpec((1, 1, tm), imap, memory_space=pltpu.SMEM)
    return pl.pallas_call(
        functools.partial(_expert_kernel, tm=tm),
        grid_spec=pltpu.PrefetchScalarGridSpec(
            num_scalar_prefetch=2,
            grid=(n_tiles,),
            in_specs=[
                smem_tile(lambda i, te, tv: (i, 0, 0)),
                smem_tile(lambda i, te, tv: (jnp.minimum(i + 1, n_tiles - 1), 0, 0)),
                pl.BlockSpec(memory_space=pl.ANY),
                pl.BlockSpec((1, d, f), lambda i, te, tv: (te[i], 0, 0)),
                pl.BlockSpec((1, d, f), lambda i, te, tv: (te[i], 0, 0)),
                pl.BlockSpec((1, f, d), lambda i, te, tv: (te[i], 0, 0)),
            ],
            out_specs=pl.BlockSpec((tm, d), lambda i, te, tv: (i, 0)),
            scratch_shapes=[pltpu.VMEM((2, tm, d), _F32), pltpu.SemaphoreType.DMA((2,))],
        ),
        out_shape=jax.ShapeDtypeStruct((n_tiles * tm, d), _F32),
        compiler_params=pltpu.CompilerParams(
            dimension_semantics=("arbitrary",), vmem_limit_bytes=VMEM_LIMIT),
        name="experts",
    )(tile_e, tile_valid, tok_tiles, tok_tiles, h2_flat, wg_bf, wu_bf, wd_bf)


def _combine_kernel(d_cur, d_nxt, ys_hbm, x1_ref, route_ref, g2_ref, npost_ref, o_ref, rbuf, sem, *,
                    tk):
    i = pl.program_id(0)
    n = pl.num_programs(0)
    slot = i % 2

    @pl.when(i == 0)
    def _():
        _start_row_gather(ys_hbm, d_cur, rbuf.at[0], sem.at[0], TOP_K * tk)

    _wait_row_gather(ys_hbm, rbuf.at[slot], sem.at[slot], TOP_K * tk)

    @pl.when(i + 1 < n)
    def _():
        _start_row_gather(ys_hbm, d_nxt, rbuf.at[1 - slot], sem.at[1 - slot], TOP_K * tk)

    route = route_ref[0]
    y = route[:, 2:3] * rbuf[slot, 0:tk, :] + route[:, 3:4] * rbuf[slot, tk:2 * tk, :]
    o_ref[0] = x1_ref[0] + g2_ref[0] * (y * _rms(y) * npost_ref[...])


def _combine(ys, dest_tiles, x1, route, gate2, n_post):
    bsz, t, d = x1.shape
    tk = TM_CMB
    per_seq = t // tk
    n_tiles = bsz * per_seq
    smem_tile = lambda imap: pl.BlockSpec((1, 1, TOP_K * tk), imap, memory_space=pltpu.SMEM)
    return pl.pallas_call(
        functools.partial(_combine_kernel, tk=tk),
        grid=(n_tiles,),
        in_specs=[
            smem_tile(lambda i: (i, 0, 0)),
            smem_tile(lambda i: (jnp.minimum(i + 1, n_tiles - 1), 0, 0)),
            pl.BlockSpec(memory_space=pl.ANY),
            pl.BlockSpec((1, tk, d), lambda i: (i // per_seq, i % per_seq, 0)),
            pl.BlockSpec((1, tk, ROUTE_LANES), lambda i: (i // per_seq, i % per_seq, 0)),
            pl.BlockSpec((1, 1, d), lambda i: (i // per_seq, 0, 0)),
            pl.BlockSpec((1, d), lambda i: (0, 0)),
        ],
        out_specs=pl.BlockSpec((1, tk, d), lambda i: (i // per_seq, i % per_seq, 0)),
        out_shape=jax.ShapeDtypeStruct((bsz, t, d), _F32),
        scratch_shapes=[pltpu.VMEM((2, TOP_K * tk, d), _F32), pltpu.SemaphoreType.DMA((2,))],
        compiler_params=pltpu.CompilerParams(
            dimension_semantics=("arbitrary",), vmem_limit_bytes=VMEM_LIMIT),
        name="combine",
    )(dest_tiles, dest_tiles, ys, x1, route, gate2, n_post)


def _dispatch_tables(expert_ids, n_experts, tm, n_tiles):
    n_assign = expert_ids.size
    flat_e = expert_ids.reshape(-1)
    onehot = (flat_e[:, None] == jnp.arange(n_experts, dtype=jnp.int32)[None, :]).astype(jnp.int32)
    csum = jnp.cumsum(onehot, axis=0)
    rank = jnp.take_along_axis(csum, flat_e[:, None], axis=1)[:, 0] - 1
    counts = csum[-1]
    padded = (counts + tm - 1) // tm * tm
    pend = jnp.cumsum(padded)
    pstart = pend - padded
    dest = pstart[flat_e] + rank
    flat_tok = jnp.arange(n_assign, dtype=jnp.int32) // TOP_K
    tok = jnp.zeros((n_tiles * tm,), jnp.int32).at[dest].set(flat_tok)
    tile_start = jnp.arange(n_tiles, dtype=jnp.int32) * tm
    tile_e = jnp.minimum(jnp.searchsorted(pend, tile_start, side="right"), n_experts - 1)
    tile_valid = (tile_start < pend[-1]).astype(jnp.int32)
    return dest.astype(jnp.int32), tok, tile_e.astype(jnp.int32), tile_valid


def kernel(x, c, w_ada, b_ada, norm_mix_pre, norm_mix_post, w_in, conv_a, norm_a_out, conv_b, conv_b_bias, ln_b_gain, ln_b_bias, w_out, norm_moe_pre, norm_moe_post, w_router_group, b_router_group, w_router_expert, b_router_expert, w_expert_gate, w_expert_up, w_expert_down):
    bsz, t, d = x.shape
    depth = w_ada.shape[0]
    n_groups = w_router_group.shape[2]
    per_group = w_router_expert.shape[3]
    n_experts = n_groups * per_group
    n_tok = bsz * t
    n_assign = n_tok * TOP_K
    n_tiles = (n_assign + n_experts * (TM_EXP - 1) + TM_EXP - 1) // TM_EXP
    row = lambda v: v.reshape(1, -1)

    for l in range(depth):
        mod = _ada(c, w_ada[l], b_ada[l])
        shift1, scale1, gate1, shift2, scale2, gate2 = [
            m.reshape(bsz, 1, d) for m in jnp.split(mod, N_MOD, axis=-1)]

        y = _mix_in(x, shift1, scale1, row(norm_mix_pre[l]), w_in[l].astype(_BF16), conv_a[l],
                    row(norm_a_out[l]), conv_b[l], row(conv_b_bias[l]), row(ln_b_gain[l]),
                    row(ln_b_bias[l]))

        n_route = n_groups + n_experts
        w_fine = jnp.transpose(w_router_expert[l], (1, 0, 2)).reshape(d, n_experts)
        w_route = jnp.zeros((d, ROUTE_LANES), _F32).at[:, :n_route].set(
            jnp.concatenate([w_router_group[l], w_fine], axis=1)).astype(_BF16)
        b_route = jnp.zeros((1, ROUTE_LANES), _F32).at[0, :n_route].set(
            jnp.concatenate([b_router_group[l], b_router_expert[l].reshape(-1)]))
        x1, h2, route = _mix_out(y, x, gate1, shift2, scale2, row(norm_mix_post[l]),
                                 row(norm_moe_pre[l]), w_out[l].astype(_BF16), w_route, b_route,
                                 n_groups, per_group)

        expert_ids = route.reshape(n_tok, ROUTE_LANES)[:, :TOP_K].astype(jnp.int32)
        dest, tok, tile_e, tile_valid = _dispatch_tables(expert_ids, n_experts, TM_EXP, n_tiles)
        ys = _experts(h2.reshape(n_tok, d), tok.reshape(n_tiles, 1, TM_EXP), tile_e, tile_valid,
                      w_expert_gate[l].astype(_BF16), w_expert_up[l].astype(_BF16),
                      w_expert_down[l].astype(_BF16))

        n_ctiles = n_tok // TM_CMB
        dest_tiles = jnp.transpose(dest.reshape(n_ctiles, TM_CMB, TOP_K), (0, 2, 1)).reshape(
            n_ctiles, 1, TOP_K * TM_CMB)
        x = _combine(ys, dest_tiles, x1, route, gate2, row(norm_moe_post[l]))
    return x
```

```python
import functools

import jax
import jax.numpy as jnp
from jax import lax
from jax.experimental import pallas as pl
from jax.experimental.pallas import tpu as pltpu

EPS = 1e-6
HEAD_DIM = 128
KERNEL_A = 3
KERNEL_B = 31
N_MOD = 6
TOP_K = 2
ROUTE_LANES = 128
ROUTE_ROWS = 8
EXPERT_ROWS = 128
TQ_RANK = 1024
HIST_A = 8
HIST_B = 32

TM_MIX = 256
TM_EXP = 256
TM_CMB = 256
ADA_COLS = 1024
VMEM_LIMIT = 56 * 1024 * 1024

_BF16 = jnp.bfloat16
_F32 = jnp.float32


def _dot(a, b):
    return jnp.dot(a, b, preferred_element_type=_F32)


def _rms(v):
    return lax.rsqrt(jnp.mean(v * v, axis=-1, keepdims=True) + EPS)


def _ada_kernel(c_ref, w_ref, b_ref, o_ref):
    c = c_ref[...]
    act = (c * jax.nn.sigmoid(c)).astype(_BF16)
    o_ref[...] = _dot(act, w_ref[...].astype(_BF16)) + b_ref[...]


def _ada(c, w_ada, b_ada):
    bsz, d = c.shape
    n_out = w_ada.shape[1]
    c_pad = jnp.zeros((8, d), _F32).at[:bsz].set(c)
    out = pl.pallas_call(
        _ada_kernel,
        grid=(n_out // ADA_COLS,),
        in_specs=[
            pl.BlockSpec((8, d), lambda j: (0, 0)),
            pl.BlockSpec((d, ADA_COLS), lambda j: (0, j)),
            pl.BlockSpec((1, ADA_COLS), lambda j: (0, j)),
        ],
        out_specs=pl.BlockSpec((8, ADA_COLS), lambda j: (0, j)),
        out_shape=jax.ShapeDtypeStruct((8, n_out), _F32),
        compiler_params=pltpu.CompilerParams(
            dimension_semantics=("arbitrary",), vmem_limit_bytes=VMEM_LIMIT),
        name="ada",
    )(c_pad, w_ada, b_ada.reshape(1, n_out))
    return out[:bsz]


def _mix_in_kernel(x_ref, sh_ref, sc_ref, g_ref, w_ref, ca_ref, na_ref, cb_ref, cbb_ref,
                   lg_ref, lb_ref, y_ref, cv_ext, u_ext, *, tm, da, db):
    @pl.when(pl.program_id(1) == 0)
    def _():
        cv_ext[0:HIST_A, :] = jnp.zeros((HIST_A, da), _F32)
        u_ext[0:HIST_B, :] = jnp.zeros((HIST_B, db), _F32)

    x = x_ref[0]
    h = x * _rms(x) * g_ref[...]
    h = h * (1.0 + sc_ref[0]) + sh_ref[0]
    hb = h.astype(_BF16)

    b_a = _dot(hb, w_ref[:, 0:da])
    c_a = _dot(hb, w_ref[:, da:2 * da])
    v_a = _dot(hb, w_ref[:, 2 * da:3 * da])
    cv_ext[HIST_A:HIST_A + tm, :] = c_a * v_a
    conv = jnp.zeros((tm, da), _F32)
    for k in range(KERNEL_A):
        off = HIST_A - (KERNEL_A - 1) + k
        conv = conv + ca_ref[k:k + 1, :] * cv_ext[off:off + tm, :]
    cv_ext[0:HIST_A, :] = cv_ext[tm:tm + HIST_A, :]
    y_a = b_a * conv
    for hd in range(da // HEAD_DIM):
        lo, hi = hd * HEAD_DIM, (hd + 1) * HEAD_DIM
        seg = y_a[:, lo:hi]
        y_ref[0, :, lo:hi] = (seg * _rms(seg) * na_ref[:, lo:hi]).astype(_BF16)

    val = _dot(hb, w_ref[:, 3 * da:3 * da + db])
    gate = _dot(hb, w_ref[:, 3 * da + db:3 * da + 2 * db])
    u_ext[HIST_B:HIST_B + tm, :] = val * jax.nn.sigmoid(gate)
    acc = jnp.zeros((tm, db), _F32) + cbb_ref[...]
    for k in range(KERNEL_B):
        off = HIST_B - (KERNEL_B - 1) + k
        acc = acc + cb_ref[k:k + 1, :] * u_ext[off:off + tm, :]
    u_ext[0:HIST_B, :] = u_ext[tm:tm + HIST_B, :]
    mu = jnp.mean(acc, axis=-1, keepdims=True)
    cen = acc - mu
    var = jnp.mean(cen * cen, axis=-1, keepdims=True)
    yn = cen * lax.rsqrt(var + EPS) * lg_ref[...] + lb_ref[...]
    y_ref[0, :, da:da + db] = (yn * jax.nn.sigmoid(yn)).astype(_BF16)


def _mix_in(x, shift, scale, g_pre, w_in_bf, conv_a, norm_a, conv_b, conv_b_bias, ln_g, ln_b):
    bsz, t, d = x.shape
    da = conv_a.shape[1]
    db = conv_b.shape[1]
    tm = TM_MIX
    full = lambda shape: pl.BlockSpec(shape, lambda b, s: (0,) * len(shape))
    per_b = pl.BlockSpec((1, 1, d), lambda b, s: (b, 0, 0))
    return pl.pallas_call(
        functools.partial(_mix_in_kernel, tm=tm, da=da, db=db),
        grid=(bsz, t // tm),
        in_specs=[
            pl.BlockSpec((1, tm, d), lambda b, s: (b, s, 0)),
            per_b, per_b,
            full((1, d)),
            pl.BlockSpec(w_in_bf.shape, lambda b, s: (0, 0), pipeline_mode=pl.Buffered(1)),
            full((KERNEL_A, da)), full((1, da)),
            full((KERNEL_B, db)), full((1, db)), full((1, db)), full((1, db)),
        ],
        out_specs=pl.BlockSpec((1, tm, da + db), lambda b, s: (b, s, 0)),
        out_shape=jax.ShapeDtypeStruct((bsz, t, da + db), _BF16),
        scratch_shapes=[pltpu.VMEM((HIST_A + tm, da), _F32), pltpu.VMEM((HIST_B + tm, db), _F32)],
        compiler_params=pltpu.CompilerParams(
            dimension_semantics=("arbitrary", "arbitrary"), vmem_limit_bytes=VMEM_LIMIT),
        name="mix_in",
    )(x, shift, scale, g_pre, w_in_bf, conv_a, norm_a, conv_b, conv_b_bias, ln_g, ln_b)


def _mix_out_kernel(y_ref, x_ref, g1_ref, sh2_ref, sc2_ref, npost_ref, npre_ref, wout_ref, wr_ref,
                    br_ref, x1_ref, h2_ref, route_ref, route_t_ref, *, n_groups, per_group):
    o = _dot(y_ref[0], wout_ref[...])
    x1 = x_ref[0] + g1_ref[0] * (o * _rms(o) * npost_ref[...])
    x1_ref[0] = x1
    h2 = x1 * _rms(x1) * npre_ref[...]
    h2 = h2 * (1.0 + sc2_ref[0]) + sh2_ref[0]
    h2_ref[0] = h2

    logits = _dot(h2.astype(_BF16), wr_ref[...]) + br_ref[...]
    lane = lax.broadcasted_iota(jnp.int32, logits.shape, 1).astype(_F32)
    neg = jnp.float32(-jnp.inf)
    big = jnp.float32(ROUTE_LANES)
    is_coarse = lane < n_groups
    coarse = jnp.where(is_coarse, logits, neg)
    m = jnp.max(coarse, axis=-1, keepdims=True)
    g_sel = jnp.min(jnp.where(coarse == m, lane, big), axis=-1, keepdims=True)
    p_g = 1.0 / jnp.sum(jnp.where(is_coarse, jnp.exp(logits - m), 0.0), axis=-1, keepdims=True)
    lo = n_groups + g_sel * per_group
    fine = jnp.where((lane >= lo) & (lane < lo + per_group), logits, neg)
    v1 = jnp.max(fine, axis=-1, keepdims=True)
    l1 = jnp.min(jnp.where(fine == v1, lane, big), axis=-1, keepdims=True)
    fine2 = jnp.where(lane == l1, neg, fine)
    v2 = jnp.max(fine2, axis=-1, keepdims=True)
    l2 = jnp.min(jnp.where(fine2 == v2, lane, big), axis=-1, keepdims=True)
    e2 = jnp.exp(v2 - v1)
    w1 = p_g / (1.0 + e2)
    w2 = p_g * e2 / (1.0 + e2)
    route = jnp.where(lane == 0, l1 - n_groups,
                      jnp.where(lane == 1, l2 - n_groups,
                                jnp.where(lane == 2, w1, jnp.where(lane == 3, w2, 0.0))))
    route_ref[0] = route
    route_t_ref[...] = jnp.transpose(route)[0:ROUTE_ROWS, :]


def _mix_out(y, x, gate1, shift2, scale2, n_post, n_pre, w_out_bf, w_route, b_route, n_groups,
             per_group):
    bsz, t, d = x.shape
    tm = TM_MIX
    full = lambda shape: pl.BlockSpec(shape, lambda b, s: (0,) * len(shape))
    per_b = pl.BlockSpec((1, 1, d), lambda b, s: (b, 0, 0))
    tile = pl.BlockSpec((1, tm, d), lambda b, s: (b, s, 0))
    return pl.pallas_call(
        functools.partial(_mix_out_kernel, n_groups=n_groups, per_group=per_group),
        grid=(bsz, t // tm),
        in_specs=[
            tile, tile, per_b, per_b, per_b, full((1, d)), full((1, d)),
            pl.BlockSpec(w_out_bf.shape, lambda b, s: (0, 0), pipeline_mode=pl.Buffered(1)),
            full((d, ROUTE_LANES)), full((1, ROUTE_LANES)),
        ],
        out_specs=[tile, tile, pl.BlockSpec((1, tm, ROUTE_LANES), lambda b, s: (b, s, 0)),
                   pl.BlockSpec((ROUTE_ROWS, tm), lambda b, s: (0, b * (t // tm) + s))],
        out_shape=[
            jax.ShapeDtypeStruct((bsz, t, d), _F32),
            jax.ShapeDtypeStruct((bsz, t, d), _F32),
            jax.ShapeDtypeStruct((bsz, t, ROUTE_LANES), _F32),
            jax.ShapeDtypeStruct((ROUTE_ROWS, bsz * t), _F32),
        ],
        compiler_params=pltpu.CompilerParams(
            dimension_semantics=("arbitrary", "arbitrary"), vmem_limit_bytes=VMEM_LIMIT),
        name="mix_out",
    )(y, x, gate1, shift2, scale2, n_post, n_pre, w_out_bf, w_route, b_route)


def _rank_kernel(rt_ref, u_ref, dest_ref, cnt_ref, carry, pbase, *, tq, tm):
    phase = pl.program_id(0)
    j = pl.program_id(1)
    sub = lax.broadcasted_iota(jnp.int32, (EXPERT_ROWS, tq), 0).astype(_F32)
    oh0 = sub == rt_ref[0:1, :]
    oh1 = sub == rt_ref[1:2, :]
    both = jnp.where(oh0, 1.0, 0.0) + jnp.where(oh1, 1.0, 0.0)
    tile_counts = jnp.sum(both, axis=1, keepdims=True)

    @pl.when(jnp.logical_and(phase == 0, j == 0))
    def _():
        carry[...] = jnp.zeros(carry.shape, _F32)

    @pl.when(jnp.logical_and(phase == 1, j == 0))
    def _():
        cnt = carry[...]
        cnt_ref[...] = cnt
        tiles = jnp.floor((cnt + (tm - 1)) * (1.0 / tm))
        hi = jnp.floor(tiles * (1.0 / 16.0))
        lo = tiles - 16.0 * hi
        r = lax.broadcasted_iota(jnp.int32, (EXPERT_ROWS, EXPERT_ROWS), 0)
        c = lax.broadcasted_iota(jnp.int32, (EXPERT_ROWS, EXPERT_ROWS), 1)
        before = jnp.where(c < r, 1.0, 0.0).astype(_BF16)
        start_tiles = 16.0 * _dot(before, hi.astype(_BF16)) + _dot(before, lo.astype(_BF16))
        pbase[...] = start_tiles * tm
        carry[...] = jnp.zeros(carry.shape, _F32)

    @pl.when(phase == 1)
    def _():
        earlier = _dot(both.astype(_BF16), u_ref[...])
        slot = earlier + carry[:, 0:1] + pbase[:, 0:1]
        d0 = jnp.sum(jnp.where(oh0, slot, 0.0), axis=0, keepdims=True)
        d1 = jnp.sum(jnp.where(oh1, slot, 0.0), axis=0, keepdims=True)
        row = lax.broadcasted_iota(jnp.int32, (ROUTE_ROWS, tq), 0)
        dest_ref[...] = jnp.where(row == 0, d0, jnp.where(row == 1, d1, 0.0)).astype(jnp.int32)

    carry[...] = carry[...] + tile_counts


def _rank(route_t, tm):
    n_tok = route_t.shape[1]
    tq = TQ_RANK
    assert tm & (tm - 1) == 0, "segment padding must be a power of two for exact f32 arithmetic"
    idx = jnp.arange(tq, dtype=jnp.int32)
    earlier_mask = (idx[:, None] < idx[None, :]).astype(_BF16)
    return pl.pallas_call(
        functools.partial(_rank_kernel, tq=tq, tm=tm),
        grid=(2, n_tok // tq),
        in_specs=[
            pl.BlockSpec((ROUTE_ROWS, tq), lambda p, j: (0, j)),
            pl.BlockSpec((tq, tq), lambda p, j: (0, 0)),
        ],
        out_specs=[pl.BlockSpec((ROUTE_ROWS, tq), lambda p, j: (0, j * p)),
                   pl.BlockSpec((EXPERT_ROWS, ROUTE_LANES), lambda p, j: (0, 0))],
        out_shape=[jax.ShapeDtypeStruct((ROUTE_ROWS, n_tok), jnp.int32),
                   jax.ShapeDtypeStruct((EXPERT_ROWS, ROUTE_LANES), _F32)],
        scratch_shapes=[pltpu.VMEM((EXPERT_ROWS, ROUTE_LANES), _F32),
                        pltpu.VMEM((EXPERT_ROWS, ROUTE_LANES), _F32)],
        compiler_params=pltpu.CompilerParams(
            dimension_semantics=("arbitrary", "arbitrary"), vmem_limit_bytes=VMEM_LIMIT),
        name="rank",
    )(route_t, earlier_mask)


def _row_copy(src, dst, sem):
    return pltpu.make_async_copy(src, dst, sem)


def _scatter_kernel(ps_ref, pn_ref, used_ref, d0_ref, d1_ref, h_ref, xs_hbm, zrow, sem, psem, *, tk,
                    tm):
    i = pl.program_id(0)

    @pl.when(i == 0)
    def _():
        zrow[...] = jnp.zeros(zrow.shape, _F32)

    @pl.when(i == pl.num_programs(0) - 1)
    def _():
        n_rows = xs_hbm.shape[0]

        def fill(t, carry):
            dst = xs_hbm.at[pl.ds(pl.multiple_of(t * tm, tm), tm)]
            _row_copy(zrow, dst, psem).start()
            _row_copy(zrow, dst, psem).wait()
            return carry

        lax.fori_loop(used_ref[0], n_rows // tm, fill, 0)

    for r in range(tk):
        src = h_ref.at[pl.ds(r, 1)]
        _row_copy(src, xs_hbm.at[pl.ds(d0_ref[0, 0, r], 1)], sem).start()
        _row_copy(src, xs_hbm.at[pl.ds(d1_ref[0, 0, r], 1)], sem).start()

    pad_start = ps_ref[i]
    n_pad = pn_ref[i]
    zsrc = zrow.at[pl.ds(0, 1)]

    def start_pad(r, carry):
        _row_copy(zsrc, xs_hbm.at[pl.ds(pad_start + r, 1)], psem).start()
        return carry

    def wait_pad(r, carry):
        _row_copy(zsrc, xs_hbm.at[pl.ds(0, 1)], psem).wait()
        return carry

    lax.fori_loop(0, n_pad, start_pad, 0)
    lax.fori_loop(0, n_pad, wait_pad, 0)
    for _ in range(TOP_K):
        _row_copy(h_ref, xs_hbm.at[pl.ds(0, tk)], sem).wait()


def _scatter(h2_flat, d0_tiles, d1_tiles, pad_start, pad_len, used_tiles, n_rows):
    n_tok, d = h2_flat.shape
    tk = TM_CMB
    tm = TM_EXP
    smem_tile = pl.BlockSpec((1, 1, tk), lambda i, ps, pn, used: (i, 0, 0), memory_space=pltpu.SMEM)
    return pl.pallas_call(
        functools.partial(_scatter_kernel, tk=tk, tm=tm),
        grid_spec=pltpu.PrefetchScalarGridSpec(
            num_scalar_prefetch=3,
            grid=(n_tok // tk,),
            in_specs=[smem_tile, smem_tile, pl.BlockSpec((tk, d), lambda i, ps, pn, used: (i, 0))],
            out_specs=pl.BlockSpec(memory_space=pl.ANY),
            scratch_shapes=[pltpu.VMEM((tm, d), _F32), pltpu.SemaphoreType.DMA(()),
                            pltpu.SemaphoreType.DMA(())],
        ),
        out_shape=jax.ShapeDtypeStruct((n_rows, d), _F32),
        compiler_params=pltpu.CompilerParams(
            dimension_semantics=("arbitrary",), vmem_limit_bytes=VMEM_LIMIT),
        name="scatter",
    )(pad_start, pad_len, used_tiles, d0_tiles, d1_tiles, h2_flat)


def _expert_kernel(te_ref, tv_ref, last_ref, xs_ref, wg_ref, wu_ref, wd_ref, ys_ref):
    del te_ref, last_ref
    valid = tv_ref[pl.program_id(0)] == 1

    @pl.when(valid)
    def _():
        xb = xs_ref[...].astype(_BF16)
        g = _dot(xb, wg_ref[0])
        u = _dot(xb, wu_ref[0])
        hid = (g * jax.nn.sigmoid(g) * u).astype(_BF16)
        ys_ref[...] = _dot(hid, wd_ref[0])

    @pl.when(jnp.logical_not(valid))
    def _():
        ys_ref[...] = jnp.zeros(ys_ref.shape, _F32)


def _experts(xs, tile_e, tile_valid, last_valid, wg_bf, wu_bf, wd_bf):
    n_rows, d = xs.shape
    tm = TM_EXP
    f = wg_bf.shape[2]
    w_in_spec = pl.BlockSpec((1, d, f), lambda i, te, tv, last: (te[i], 0, 0))
    return pl.pallas_call(
        _expert_kernel,
        grid_spec=pltpu.PrefetchScalarGridSpec(
            num_scalar_prefetch=3,
            grid=(n_rows // tm,),
            in_specs=[
                pl.BlockSpec((tm, d), lambda i, te, tv, last: (jnp.minimum(i, last[0]), 0)),
                w_in_spec, w_in_spec,
                pl.BlockSpec((1, f, d), lambda i, te, tv, last: (te[i], 0, 0)),
            ],
            out_specs=pl.BlockSpec((tm, d), lambda i, te, tv, last: (i, 0)),
        ),
        out_shape=jax.ShapeDtypeStruct((n_rows, d), _F32),
        compiler_params=pltpu.CompilerParams(
            dimension_semantics=("arbitrary",), vmem_limit_bytes=VMEM_LIMIT),
        name="experts",
    )(tile_e, tile_valid, last_valid, xs, wg_bf, wu_bf, wd_bf)


def _start_row_gather(src_hbm, idx_ref, dst, sem, n_rows):
    for r in range(n_rows):
        _row_copy(src_hbm.at[pl.ds(idx_ref[0, 0, r], 1)], dst.at[pl.ds(r, 1)], sem).start()


def _wait_row_gather(src_hbm, dst, sem, n_rows):
    _row_copy(src_hbm.at[pl.ds(0, n_rows)], dst, sem).wait()


def _combine_kernel(d_cur, d_nxt, ys_hbm, x1_ref, route_ref, g2_ref, npost_ref, o_ref, rbuf, sem, *,
                    tk):
    i = pl.program_id(0)
    n = pl.num_programs(0)
    slot = i % 2

    @pl.when(i == 0)
    def _():
        _start_row_gather(ys_hbm, d_cur, rbuf.at[0], sem.at[0], TOP_K * tk)

    _wait_row_gather(ys_hbm, rbuf.at[slot], sem.at[slot], TOP_K * tk)

    @pl.when(i + 1 < n)
    def _():
        _start_row_gather(ys_hbm, d_nxt, rbuf.at[1 - slot], sem.at[1 - slot], TOP_K * tk)

    route = route_ref[0]
    y = route[:, 2:3] * rbuf[slot, 0:tk, :] + route[:, 3:4] * rbuf[slot, tk:2 * tk, :]
    o_ref[0] = x1_ref[0] + g2_ref[0] * (y * _rms(y) * npost_ref[...])


def _combine(ys, dest_tiles, x1, route, gate2, n_post):
    bsz, t, d = x1.shape
    tk = TM_CMB
    per_seq = t // tk
    n_tiles = bsz * per_seq
    smem_tile = lambda imap: pl.BlockSpec((1, 1, TOP_K * tk), imap, memory_space=pltpu.SMEM)
    return pl.pallas_call(
        functools.partial(_combine_kernel, tk=tk),
        grid=(n_tiles,),
        in_specs=[
            smem_tile(lambda i: (i, 0, 0)),
            smem_tile(lambda i: (jnp.minimum(i + 1, n_tiles - 1), 0, 0)),
            pl.BlockSpec(memory_space=pl.ANY),
            pl.BlockSpec((1, tk, d), lambda i: (i // per_seq, i % per_seq, 0)),
            pl.BlockSpec((1, tk, ROUTE_LANES), lambda i: (i // per_seq, i % per_seq, 0)),
            pl.BlockSpec((1, 1, d), lambda i: (i // per_seq, 0, 0)),
            pl.BlockSpec((1, d), lambda i: (0, 0)),
        ],
        out_specs=pl.BlockSpec((1, tk, d), lambda i: (i // per_seq, i % per_seq, 0)),
        out_shape=jax.ShapeDtypeStruct((bsz, t, d), _F32),
        scratch_shapes=[pltpu.VMEM((2, TOP_K * tk, d), _F32), pltpu.SemaphoreType.DMA((2,))],
        compiler_params=pltpu.CompilerParams(
            dimension_semantics=("arbitrary",), vmem_limit_bytes=VMEM_LIMIT),
        name="combine",
    )(dest_tiles, dest_tiles, ys, x1, route, gate2, n_post)


def _segment_tables(counts, tm, n_tiles, n_steps):
    n_experts = counts.shape[0]
    padded = (counts + tm - 1) // tm * tm
    pend = jnp.cumsum(padded)
    pstart = pend - padded
    tile_start = jnp.arange(n_tiles, dtype=jnp.int32) * tm
    tile_e = jnp.minimum(jnp.searchsorted(pend, tile_start, side="right"), n_experts - 1)
    tile_valid = (tile_start < pend[-1]).astype(jnp.int32)
    last_valid = (pend[-1:] // tm - 1).astype(jnp.int32)
    pad_start = jnp.zeros((n_steps,), jnp.int32).at[:n_experts].set(pstart + counts)
    pad_len = jnp.zeros((n_steps,), jnp.int32).at[:n_experts].set(padded - counts)
    return tile_e.astype(jnp.int32), tile_valid, last_valid, pad_start, pad_len


def kernel(x, c, w_ada, b_ada, norm_mix_pre, norm_mix_post, w_in, conv_a, norm_a_out, conv_b, conv_b_bias, ln_b_gain, ln_b_bias, w_out, norm_moe_pre, norm_moe_post, w_router_group, b_router_group, w_router_expert, b_router_expert, w_expert_gate, w_expert_up, w_expert_down):
    bsz, t, d = x.shape
    depth = w_ada.shape[0]
    n_groups = w_router_group.shape[2]
    per_group = w_router_expert.shape[3]
    n_experts = n_groups * per_group
    n_tok = bsz * t
    n_assign = n_tok * TOP_K
    n_tiles = (n_assign + n_experts * (TM_EXP - 1) + TM_EXP - 1) // TM_EXP
    row = lambda v: v.reshape(1, -1)

    for l in range(depth):
        mod = _ada(c, w_ada[l], b_ada[l])
        shift1, scale1, gate1, shift2, scale2, gate2 = [
            m.reshape(bsz, 1, d) for m in jnp.split(mod, N_MOD, axis=-1)]

        y = _mix_in(x, shift1, scale1, row(norm_mix_pre[l]), w_in[l].astype(_BF16), conv_a[l],
                    row(norm_a_out[l]), conv_b[l], row(conv_b_bias[l]), row(ln_b_gain[l]),
                    row(ln_b_bias[l]))

        n_route = n_groups + n_experts
        w_fine = jnp.transpose(w_router_expert[l], (1, 0, 2)).reshape(d, n_experts)
        w_route = jnp.zeros((d, ROUTE_LANES), _F32).at[:, :n_route].set(
            jnp.concatenate([w_router_group[l], w_fine], axis=1)).astype(_BF16)
        b_route = jnp.zeros((1, ROUTE_LANES), _F32).at[0, :n_route].set(
            jnp.concatenate([b_router_group[l], b_router_expert[l].reshape(-1)]))
        x1, h2, route, route_t = _mix_out(y, x, gate1, shift2, scale2, row(norm_mix_post[l]),
                                          row(norm_moe_pre[l]), w_out[l].astype(_BF16), w_route,
                                          b_route, n_groups, per_group)

        dest, counts_f = _rank(route_t, TM_EXP)
        counts = counts_f[:n_experts, 0].astype(jnp.int32)
        n_ctiles = n_tok // TM_CMB
        assert n_ctiles >= n_experts
        tile_e, tile_valid, last_valid, pad_start, pad_len = _segment_tables(
            counts, TM_EXP, n_tiles, n_ctiles)
        d0_tiles = dest[0].reshape(n_ctiles, 1, TM_CMB)
        d1_tiles = dest[1].reshape(n_ctiles, 1, TM_CMB)
        xs = _scatter(h2.reshape(n_tok, d), d0_tiles, d1_tiles, pad_start, pad_len, last_valid + 1,
                      n_tiles * TM_EXP)
        ys = _experts(xs, tile_e, tile_valid, last_valid, w_expert_gate[l].astype(_BF16),
                      w_expert_up[l].astype(_BF16), w_expert_down[l].astype(_BF16))

        dest_tiles = jnp.concatenate([d0_tiles, d1_tiles], axis=2)
        x = _combine(ys, dest_tiles, x1, route, gate2, row(norm_moe_post[l]))
    return x
```

```python
import functools

import jax
import jax.numpy as jnp
from jax import lax
from jax.experimental import pallas as pl
from jax.experimental.pallas import tpu as pltpu

EPS = 1e-6
HEAD_DIM = 128
KERNEL_A = 3
KERNEL_B = 31
N_MOD = 6
TOP_K = 2
ROUTE_LANES = 128
ROUTE_ROWS = 8
EXPERT_ROWS = 128
TQ_RANK = 1024
HIST_A = 8
HIST_B = 32
CONV_ROWS = 128
MIX_COLS = 256

TM_MIX = 256
TM_EXP = 256
TM_CMB = 256
ADA_COLS = 1024
VMEM_LIMIT = 56 * 1024 * 1024

_BF16 = jnp.bfloat16
_F32 = jnp.float32


def _dot(a, b):
    return jnp.dot(a, b, preferred_element_type=_F32)


def _rms(v):
    return lax.rsqrt(jnp.mean(v * v, axis=-1, keepdims=True) + EPS)


def _ada_kernel(c_ref, w_ref, b_ref, o_ref):
    c = c_ref[...]
    act = (c * jax.nn.sigmoid(c)).astype(_BF16)
    o_ref[...] = _dot(act, w_ref[...].astype(_BF16)) + b_ref[...]


def _ada(c, w_ada, b_ada):
    bsz, d = c.shape
    n_out = w_ada.shape[1]
    c_pad = jnp.zeros((8, d), _F32).at[:bsz].set(c)
    out = pl.pallas_call(
        _ada_kernel,
        grid=(n_out // ADA_COLS,),
        in_specs=[
            pl.BlockSpec((8, d), lambda j: (0, 0)),
            pl.BlockSpec((d, ADA_COLS), lambda j: (0, j)),
            pl.BlockSpec((1, ADA_COLS), lambda j: (0, j)),
        ],
        out_specs=pl.BlockSpec((8, ADA_COLS), lambda j: (0, j)),
        out_shape=jax.ShapeDtypeStruct((8, n_out), _F32),
        compiler_params=pltpu.CompilerParams(
            dimension_semantics=("arbitrary",), vmem_limit_bytes=VMEM_LIMIT),
        name="ada",
    )(c_pad, w_ada, b_ada.reshape(1, n_out))
    return out[:bsz]


def _dwconv_chunk(ext_ref, w_ref, r0, rows, lo, hi, hist, ksize):
    span = rows + hist
    first = hist - (ksize - 1)
    xe = ext_ref[r0:r0 + span, lo:hi]
    acc = None
    for b in range(8):
        offsets = [o for o in range(first, hist + 1) if o % 8 == b]
        if not offsets:
            continue
        shifted = xe if b == 0 else pltpu.roll(xe, span - b, 0)
        for o in offsets:
            k = o - first
            term = w_ref[k:k + 1, lo:hi] * shifted[o - b:o - b + rows, :]
            acc = term if acc is None else acc + term
    return acc


def _mix_in_kernel(x_ref, sh_ref, sc_ref, g_ref, w_ref, ca_ref, na_ref, cb_ref, cbb_ref,
                   lg_ref, lb_ref, y_ref, cv_ext, u_ext, ba_buf, conv_b_out, *, tm, da, db):
    nc = MIX_COLS
    @pl.when(pl.program_id(1) == 0)
    def _():
        cv_ext[...] = jnp.zeros(cv_ext.shape, _F32)
        u_ext[0:HIST_B, :] = jnp.zeros((HIST_B, db), _F32)
        ba_buf[...] = jnp.zeros(ba_buf.shape, _F32)
        conv_b_out[...] = jnp.zeros(conv_b_out.shape, _F32)

    x = x_ref[0]
    h = x * _rms(x) * g_ref[...]
    h = h * (1.0 + sc_ref[0]) + sh_ref[0]
    hb = h.astype(_BF16)

    def finish_head(hd):
        lo, hi = hd * HEAD_DIM, (hd + 1) * HEAD_DIM
        for r0 in range(0, tm, CONV_ROWS):
            conv = _dwconv_chunk(cv_ext, ca_ref, r0, CONV_ROWS, lo, hi, HIST_A, KERNEL_A)
            seg = ba_buf[r0:r0 + CONV_ROWS, lo:hi] * conv
            y_ref[0, r0:r0 + CONV_ROWS, lo:hi] = (seg * _rms(seg) * na_ref[:, lo:hi]).astype(_BF16)

    def finish_rows(r0, rows):
        acc = conv_b_out[r0:r0 + rows, :]
        mu = jnp.mean(acc, axis=-1, keepdims=True)
        cen = acc - mu
        var = jnp.mean(cen * cen, axis=-1, keepdims=True)
        yn = cen * lax.rsqrt(var + EPS) * lg_ref[...] + lb_ref[...]
        y_ref[0, r0:r0 + rows, da:da + db] = (yn * jax.nn.sigmoid(yn)).astype(_BF16)

    n_b = db // nc
    heads_per = (da // HEAD_DIM) // n_b
    ln_rows = tm // n_b
    for j in range(n_b):
        lo = j * nc
        val = _dot(hb, w_ref[:, 3 * da + lo:3 * da + lo + nc])
        gate = _dot(hb, w_ref[:, 3 * da + db + lo:3 * da + db + lo + nc])
        for hd in range(j * heads_per, (j + 1) * heads_per):
            finish_head(hd)
        finish_rows(j * ln_rows, ln_rows)
        u_ext[HIST_B:HIST_B + tm, lo:lo + nc] = val * jax.nn.sigmoid(gate)
    cv_ext[0:HIST_A, :] = cv_ext[tm:tm + HIST_A, :]

    conv_chunks = [(lo, r0) for lo in range(0, db, HEAD_DIM) for r0 in range(0, tm, CONV_ROWS)]
    n_a = da // nc
    n_mm = 3 * n_a
    done = 0
    for q in range(n_mm):
        j = q // 2 if q < 2 * n_a else q - 2 * n_a
        lo = j * nc
        if q < 2 * n_a and q % 2 == 0:
            c_part = _dot(hb, w_ref[:, da + lo:da + lo + nc])
        elif q < 2 * n_a:
            v_part = _dot(hb, w_ref[:, 2 * da + lo:2 * da + lo + nc])
            cv_ext[HIST_A:HIST_A + tm, lo:lo + nc] = c_part * v_part
        else:
            ba_buf[:, lo:lo + nc] = _dot(hb, w_ref[:, lo:lo + nc])
        upto = (q + 1) * len(conv_chunks) // n_mm
        for lo_c, r0 in conv_chunks[done:upto]:
            conv = _dwconv_chunk(u_ext, cb_ref, r0, CONV_ROWS, lo_c, lo_c + HEAD_DIM, HIST_B, KERNEL_B)
            conv_b_out[r0:r0 + CONV_ROWS, lo_c:lo_c + HEAD_DIM] = conv + cbb_ref[:, lo_c:lo_c + HEAD_DIM]
        done = upto
    u_ext[0:HIST_B, :] = u_ext[tm:tm + HIST_B, :]


def _mix_in(x, shift, scale, g_pre, w_in_bf, conv_a, norm_a, conv_b, conv_b_bias, ln_g, ln_b):
    bsz, t, d = x.shape
    da = conv_a.shape[1]
    db = conv_b.shape[1]
    tm = TM_MIX
    full = lambda shape: pl.BlockSpec(shape, lambda b, s: (0,) * len(shape))
    per_b = pl.BlockSpec((1, 1, d), lambda b, s: (b, 0, 0))
    n_t = t // tm
    return pl.pallas_call(
        functools.partial(_mix_in_kernel, tm=tm, da=da, db=db),
        grid=(bsz, n_t + 1),
        in_specs=[
            pl.BlockSpec((1, tm, d), lambda b, s: (b, jnp.minimum(s, n_t - 1), 0)),
            per_b, per_b,
            full((1, d)),
            pl.BlockSpec(w_in_bf.shape, lambda b, s: (0, 0), pipeline_mode=pl.Buffered(1)),
            full((KERNEL_A, da)), full((1, da)),
            full((KERNEL_B, db)), full((1, db)), full((1, db)), full((1, db)),
        ],
        out_specs=pl.BlockSpec((1, tm, da + db), lambda b, s: (b, jnp.maximum(s - 1, 0), 0)),
        out_shape=jax.ShapeDtypeStruct((bsz, t, da + db), _BF16),
        scratch_shapes=[pltpu.VMEM((HIST_A + tm, da), _F32), pltpu.VMEM((HIST_B + tm, db), _F32),
                        pltpu.VMEM((tm, da), _F32), pltpu.VMEM((tm, db), _F32)],
        compiler_params=pltpu.CompilerParams(
            dimension_semantics=("arbitrary", "arbitrary"), vmem_limit_bytes=VMEM_LIMIT),
        name="mix_in",
    )(x, shift, scale, g_pre, w_in_bf, conv_a, norm_a, conv_b, conv_b_bias, ln_g, ln_b)


def _mix_out_kernel(y_ref, x_ref, g1_ref, sh2_ref, sc2_ref, npost_ref, npre_ref, wout_ref, wr_ref,
                    br_ref, x1_ref, h2_ref, route_ref, route_t_ref, *, n_groups, per_group):
    o = _dot(y_ref[0], wout_ref[...])
    x1 = x_ref[0] + g1_ref[0] * (o * _rms(o) * npost_ref[...])
    x1_ref[0] = x1
    h2 = x1 * _rms(x1) * npre_ref[...]
    h2 = h2 * (1.0 + sc2_ref[0]) + sh2_ref[0]
    h2_ref[0] = h2

    logits = _dot(h2.astype(_BF16), wr_ref[...]) + br_ref[...]
    lane = lax.broadcasted_iota(jnp.int32, logits.shape, 1).astype(_F32)
    neg = jnp.float32(-jnp.inf)
    big = jnp.float32(ROUTE_LANES)
    is_coarse = lane < n_groups
    coarse = jnp.where(is_coarse, logits, neg)
    m = jnp.max(coarse, axis=-1, keepdims=True)
    g_sel = jnp.min(jnp.where(coarse == m, lane, big), axis=-1, keepdims=True)
    p_g = 1.0 / jnp.sum(jnp.where(is_coarse, jnp.exp(logits - m), 0.0), axis=-1, keepdims=True)
    lo = n_groups + g_sel * per_group
    fine = jnp.where((lane >= lo) & (lane < lo + per_group), logits, neg)
    v1 = jnp.max(fine, axis=-1, keepdims=True)
    l1 = jnp.min(jnp.where(fine == v1, lane, big), axis=-1, keepdims=True)
    fine2 = jnp.where(lane == l1, neg, fine)
    v2 = jnp.max(fine2, axis=-1, keepdims=True)
    l2 = jnp.min(jnp.where(fine2 == v2, lane, big), axis=-1, keepdims=True)
    e2 = jnp.exp(v2 - v1)
    w1 = p_g / (1.0 + e2)
    w2 = p_g * e2 / (1.0 + e2)
    route = jnp.where(lane == 0, l1 - n_groups,
                      jnp.where(lane == 1, l2 - n_groups,
                                jnp.where(lane == 2, w1, jnp.where(lane == 3, w2, 0.0))))
    route_ref[0] = route
    route_t_ref[...] = jnp.transpose(route)[0:ROUTE_ROWS, :]


def _mix_out(y, x, gate1, shift2, scale2, n_post, n_pre, w_out_bf, w_route, b_route, n_groups,
             per_group):
    bsz, t, d = x.shape
    tm = TM_MIX
    full = lambda shape: pl.BlockSpec(shape, lambda b, s: (0,) * len(shape))
    per_b = pl.BlockSpec((1, 1, d), lambda b, s: (b, 0, 0))
    tile = pl.BlockSpec((1, tm, d), lambda b, s: (b, s, 0))
    return pl.pallas_call(
        functools.partial(_mix_out_kernel, n_groups=n_groups, per_group=per_group),
        grid=(bsz, t // tm),
        in_specs=[
            tile, tile, per_b, per_b, per_b, full((1, d)), full((1, d)),
            pl.BlockSpec(w_out_bf.shape, lambda b, s: (0, 0), pipeline_mode=pl.Buffered(1)),
            full((d, ROUTE_LANES)), full((1, ROUTE_LANES)),
        ],
        out_specs=[tile, tile, pl.BlockSpec((1, tm, ROUTE_LANES), lambda b, s: (b, s, 0)),
                   pl.BlockSpec((ROUTE_ROWS, tm), lambda b, s: (0, b * (t // tm) + s))],
        out_shape=[
            jax.ShapeDtypeStruct((bsz, t, d), _F32),
            jax.ShapeDtypeStruct((bsz, t, d), _F32),
            jax.ShapeDtypeStruct((bsz, t, ROUTE_LANES), _F32),
            jax.ShapeDtypeStruct((ROUTE_ROWS, bsz * t), _F32),
        ],
        compiler_params=pltpu.CompilerParams(
            dimension_semantics=("arbitrary", "arbitrary"), vmem_limit_bytes=VMEM_LIMIT),
        name="mix_out",
    )(y, x, gate1, shift2, scale2, n_post, n_pre, w_out_bf, w_route, b_route)


def _rank_kernel(rt_ref, u_ref, dest_ref, cnt_ref, carry, pbase, *, tq, tm):
    phase = pl.program_id(0)
    j = pl.program_id(1)
    sub = lax.broadcasted_iota(jnp.int32, (EXPERT_ROWS, tq), 0).astype(_F32)
    oh0 = sub == rt_ref[0:1, :]
    oh1 = sub == rt_ref[1:2, :]
    both = jnp.where(oh0, 1.0, 0.0) + jnp.where(oh1, 1.0, 0.0)
    tile_counts = jnp.sum(both, axis=1, keepdims=True)

    @pl.when(jnp.logical_and(phase == 0, j == 0))
    def _():
        carry[...] = jnp.zeros(carry.shape, _F32)

    @pl.when(jnp.logical_and(phase == 1, j == 0))
    def _():
        cnt = carry[...]
        cnt_ref[...] = cnt
        tiles = jnp.floor((cnt + (tm - 1)) * (1.0 / tm))
        hi = jnp.floor(tiles * (1.0 / 16.0))
        lo = tiles - 16.0 * hi
        r = lax.broadcasted_iota(jnp.int32, (EXPERT_ROWS, EXPERT_ROWS), 0)
        c = lax.broadcasted_iota(jnp.int32, (EXPERT_ROWS, EXPERT_ROWS), 1)
        before = jnp.where(c < r, 1.0, 0.0).astype(_BF16)
        start_tiles = 16.0 * _dot(before, hi.astype(_BF16)) + _dot(before, lo.astype(_BF16))
        pbase[...] = start_tiles * tm
        carry[...] = jnp.zeros(carry.shape, _F32)

    @pl.when(phase == 1)
    def _():
        earlier = _dot(both.astype(_BF16), u_ref[...])
        slot = earlier + carry[:, 0:1] + pbase[:, 0:1]
        d0 = jnp.sum(jnp.where(oh0, slot, 0.0), axis=0, keepdims=True)
        d1 = jnp.sum(jnp.where(oh1, slot, 0.0), axis=0, keepdims=True)
        row = lax.broadcasted_iota(jnp.int32, (ROUTE_ROWS, tq), 0)
        dest_ref[...] = jnp.where(row == 0, d0, jnp.where(row == 1, d1, 0.0)).astype(jnp.int32)

    carry[...] = carry[...] + tile_counts


def _rank(route_t, tm):
    n_tok = route_t.shape[1]
    tq = TQ_RANK
    assert tm & (tm - 1) == 0, "segment padding must be a power of two for exact f32 arithmetic"
    idx = jnp.arange(tq, dtype=jnp.int32)
    earlier_mask = (idx[:, None] < idx[None, :]).astype(_BF16)
    return pl.pallas_call(
        functools.partial(_rank_kernel, tq=tq, tm=tm),
        grid=(2, n_tok // tq),
        in_specs=[
            pl.BlockSpec((ROUTE_ROWS, tq), lambda p, j: (0, j)),
            pl.BlockSpec((tq, tq), lambda p, j: (0, 0)),
        ],
        out_specs=[pl.BlockSpec((ROUTE_ROWS, tq), lambda p, j: (0, j * p)),
                   pl.BlockSpec((EXPERT_ROWS, ROUTE_LANES), lambda p, j: (0, 0))],
        out_shape=[jax.ShapeDtypeStruct((ROUTE_ROWS, n_tok), jnp.int32),
                   jax.ShapeDtypeStruct((EXPERT_ROWS, ROUTE_LANES), _F32)],
        scratch_shapes=[pltpu.VMEM((EXPERT_ROWS, ROUTE_LANES), _F32),
                        pltpu.VMEM((EXPERT_ROWS, ROUTE_LANES), _F32)],
        compiler_params=pltpu.CompilerParams(
            dimension_semantics=("arbitrary", "arbitrary"), vmem_limit_bytes=VMEM_LIMIT),
        name="rank",
    )(route_t, earlier_mask)


def _row_copy(src, dst, sem):
    return pltpu.make_async_copy(src, dst, sem)


def _scatter_kernel(ps_ref, pn_ref, used_ref, d0_ref, d1_ref, h_ref, xs_hbm, zrow, sem, psem, *, tk,
                    tm):
    i = pl.program_id(0)

    @pl.when(i == 0)
    def _():
        zrow[...] = jnp.zeros(zrow.shape, _F32)

    @pl.when(i == pl.num_programs(0) - 1)
    def _():
        n_rows = xs_hbm.shape[0]

        def fill(t, carry):
            dst = xs_hbm.at[pl.ds(pl.multiple_of(t * tm, tm), tm)]
            _row_copy(zrow, dst, psem).start()
            _row_copy(zrow, dst, psem).wait()
            return carry

        lax.fori_loop(used_ref[0], n_rows // tm, fill, 0)

    for r in range(tk):
        src = h_ref.at[pl.ds(r, 1)]
        _row_copy(src, xs_hbm.at[pl.ds(d0_ref[0, 0, r], 1)], sem).start()
        _row_copy(src, xs_hbm.at[pl.ds(d1_ref[0, 0, r], 1)], sem).start()

    pad_start = ps_ref[i]
    n_pad = pn_ref[i]
    zsrc = zrow.at[pl.ds(0, 1)]

    def start_pad(r, carry):
        _row_copy(zsrc, xs_hbm.at[pl.ds(pad_start + r, 1)], psem).start()
        return carry

    def wait_pad(r, carry):
        _row_copy(zsrc, xs_hbm.at[pl.ds(0, 1)], psem).wait()
        return carry

    lax.fori_loop(0, n_pad, start_pad, 0)
    lax.fori_loop(0, n_pad, wait_pad, 0)
    for _ in range(TOP_K):
        _row_copy(h_ref, xs_hbm.at[pl.ds(0, tk)], sem).wait()


def _scatter(h2_flat, d0_tiles, d1_tiles, pad_start, pad_len, used_tiles, n_rows):
    n_tok, d = h2_flat.shape
    tk = TM_CMB
    tm = TM_EXP
    smem_tile = pl.BlockSpec((1, 1, tk), lambda i, ps, pn, used: (i, 0, 0), memory_space=pltpu.SMEM)
    return pl.pallas_call(
        functools.partial(_scatter_kernel, tk=tk, tm=tm),
        grid_spec=pltpu.PrefetchScalarGridSpec(
            num_scalar_prefetch=3,
            grid=(n_tok // tk,),
            in_specs=[smem_tile, smem_tile, pl.BlockSpec((tk, d), lambda i, ps, pn, used: (i, 0))],
            out_specs=pl.BlockSpec(memory_space=pl.ANY),
            scratch_shapes=[pltpu.VMEM((tm, d), _F32), pltpu.SemaphoreType.DMA(()),
                            pltpu.SemaphoreType.DMA(())],
        ),
        out_shape=jax.ShapeDtypeStruct((n_rows, d), _F32),
        compiler_params=pltpu.CompilerParams(
            dimension_semantics=("arbitrary",), vmem_limit_bytes=VMEM_LIMIT),
        name="scatter",
    )(pad_start, pad_len, used_tiles, d0_tiles, d1_tiles, h2_flat)


def _expert_kernel(te_ref, tv_ref, last_ref, xs_ref, wg_ref, wu_ref, wd_ref, ys_ref):
    del te_ref, last_ref
    valid = tv_ref[pl.program_id(0)] == 1

    @pl.when(valid)
    def _():
        xb = xs_ref[...].astype(_BF16)
        g = _dot(xb, wg_ref[0])
        u = _dot(xb, wu_ref[0])
        hid = (g * jax.nn.sigmoid(g) * u).astype(_BF16)
        ys_ref[...] = _dot(hid, wd_ref[0])

    @pl.when(jnp.logical_not(valid))
    def _():
        ys_ref[...] = jnp.zeros(ys_ref.shape, _F32)


def _experts(xs, tile_e, tile_valid, last_valid, wg_bf, wu_bf, wd_bf):
    n_rows, d = xs.shape
    tm = TM_EXP
    f = wg_bf.shape[2]
    w_in_spec = pl.BlockSpec((1, d, f), lambda i, te, tv, last: (te[i], 0, 0))
    return pl.pallas_call(
        _expert_kernel,
        grid_spec=pltpu.PrefetchScalarGridSpec(
            num_scalar_prefetch=3,
            grid=(n_rows // tm,),
            in_specs=[
                pl.BlockSpec((tm, d), lambda i, te, tv, last: (jnp.minimum(i, last[0]), 0)),
                w_in_spec, w_in_spec,
                pl.BlockSpec((1, f, d), lambda i, te, tv, last: (te[i], 0, 0)),
            ],
            out_specs=pl.BlockSpec((tm, d), lambda i, te, tv, last: (i, 0)),
        ),
        out_shape=jax.ShapeDtypeStruct((n_rows, d), _F32),
        compiler_params=pltpu.CompilerParams(
            dimension_semantics=("arbitrary",), vmem_limit_bytes=VMEM_LIMIT),
        name="experts",
    )(tile_e, tile_valid, last_valid, xs, wg_bf, wu_bf, wd_bf)


def _start_row_gather(src_hbm, idx_ref, dst, sem, n_rows):
    for r in range(n_rows):
        _row_copy(src_hbm.at[pl.ds(idx_ref[0, 0, r], 1)], dst.at[pl.ds(r, 1)], sem).start()


def _wait_row_gather(src_hbm, dst, sem, n_rows):
    _row_copy(src_hbm.at[pl.ds(0, n_rows)], dst, sem).wait()


def _combine_kernel(d_cur, d_nxt, ys_hbm, x1_ref, route_ref, g2_ref, npost_ref, o_ref, rbuf, sem, *,
                    tk):
    i = pl.program_id(0)
    n = pl.num_programs(0)
    slot = i % 2

    @pl.when(i == 0)
    def _():
        _start_row_gather(ys_hbm, d_cur, rbuf.at[0], sem.at[0], TOP_K * tk)

    _wait_row_gather(ys_hbm, rbuf.at[slot], sem.at[slot], TOP_K * tk)

    @pl.when(i + 1 < n)
    def _():
        _start_row_gather(ys_hbm, d_nxt, rbuf.at[1 - slot], sem.at[1 - slot], TOP_K * tk)

    route = route_ref[0]
    y = route[:, 2:3] * rbuf[slot, 0:tk, :] + route[:, 3:4] * rbuf[slot, tk:2 * tk, :]
    o_ref[0] = x1_ref[0] + g2_ref[0] * (y * _rms(y) * npost_ref[...])


def _combine(ys, dest_tiles, x1, route, gate2, n_post):
    bsz, t, d = x1.shape
    tk = TM_CMB
    per_seq = t // tk
    n_tiles = bsz * per_seq
    smem_tile = lambda imap: pl.BlockSpec((1, 1, TOP_K * tk), imap, memory_space=pltpu.SMEM)
    return pl.pallas_call(
        functools.partial(_combine_kernel, tk=tk),
        grid=(n_tiles,),
        in_specs=[
            smem_tile(lambda i: (i, 0, 0)),
            smem_tile(lambda i: (jnp.minimum(i + 1, n_tiles - 1), 0, 0)),
            pl.BlockSpec(memory_space=pl.ANY),
            pl.BlockSpec((1, tk, d), lambda i: (i // per_seq, i % per_seq, 0)),
            pl.BlockSpec((1, tk, ROUTE_LANES), lambda i: (i // per_seq, i % per_seq, 0)),
            pl.BlockSpec((1, 1, d), lambda i: (i // per_seq, 0, 0)),
            pl.BlockSpec((1, d), lambda i: (0, 0)),
        ],
        out_specs=pl.BlockSpec((1, tk, d), lambda i: (i // per_seq, i % per_seq, 0)),
        out_shape=jax.ShapeDtypeStruct((bsz, t, d), _F32),
        scratch_shapes=[pltpu.VMEM((2, TOP_K * tk, d), _F32), pltpu.SemaphoreType.DMA((2,))],
        compiler_params=pltpu.CompilerParams(
            dimension_semantics=("arbitrary",), vmem_limit_bytes=VMEM_LIMIT),
        name="combine",
    )(dest_tiles, dest_tiles, ys, x1, route, gate2, n_post)


def _segment_tables(counts, tm, n_tiles, n_steps):
    n_experts = counts.shape[0]
    padded = (counts + tm - 1) // tm * tm
    pend = jnp.cumsum(padded)
    pstart = pend - padded
    tile_start = jnp.arange(n_tiles, dtype=jnp.int32) * tm
    tile_e = jnp.minimum(jnp.sum(pend[None, :] <= tile_start[:, None], axis=1), n_experts - 1)
    tile_valid = (tile_start < pend[-1]).astype(jnp.int32)
    last_valid = (pend[-1:] // tm - 1).astype(jnp.int32)
    pad_start = jnp.zeros((n_steps,), jnp.int32).at[:n_experts].set(pstart + counts)
    pad_len = jnp.zeros((n_steps,), jnp.int32).at[:n_experts].set(padded - counts)
    return tile_e.astype(jnp.int32), tile_valid, last_valid, pad_start, pad_len


def kernel(x, c, w_ada, b_ada, norm_mix_pre, norm_mix_post, w_in, conv_a, norm_a_out, conv_b, conv_b_bias, ln_b_gain, ln_b_bias, w_out, norm_moe_pre, norm_moe_post, w_router_group, b_router_group, w_router_expert, b_router_expert, w_expert_gate, w_expert_up, w_expert_down):
    bsz, t, d = x.shape
    depth = w_ada.shape[0]
    n_groups = w_router_group.shape[2]
    per_group = w_router_expert.shape[3]
    n_experts = n_groups * per_group
    n_tok = bsz * t
    n_assign = n_tok * TOP_K
    n_tiles = (n_assign + n_experts * (TM_EXP - 1) + TM_EXP - 1) // TM_EXP
    row = lambda v: v.reshape(1, -1)

    for l in range(depth):
        mod = _ada(c, w_ada[l], b_ada[l])
        shift1, scale1, gate1, shift2, scale2, gate2 = [
            m.reshape(bsz, 1, d) for m in jnp.split(mod, N_MOD, axis=-1)]

        y = _mix_in(x, shift1, scale1, row(norm_mix_pre[l]), w_in[l].astype(_BF16), conv_a[l],
                    row(norm_a_out[l]), conv_b[l], row(conv_b_bias[l]), row(ln_b_gain[l]),
                    row(ln_b_bias[l]))

        n_route = n_groups + n_experts
        w_fine = jnp.transpose(w_router_expert[l], (1, 0, 2)).reshape(d, n_experts)
        w_route = jnp.zeros((d, ROUTE_LANES), _F32).at[:, :n_route].set(
            jnp.concatenate([w_router_group[l], w_fine], axis=1)).astype(_BF16)
        b_route = jnp.zeros((1, ROUTE_LANES), _F32).at[0, :n_route].set(
            jnp.concatenate([b_router_group[l], b_router_expert[l].reshape(-1)]))
        x1, h2, route, route_t = _mix_out(y, x, gate1, shift2, scale2, row(norm_mix_post[l]),
                                          row(norm_moe_pre[l]), w_out[l].astype(_BF16), w_route,
                                          b_route, n_groups, per_group)

        dest, counts_f = _rank(route_t, TM_EXP)
        counts = counts_f[:n_experts, 0].astype(jnp.int32)
        n_ctiles = n_tok // TM_CMB
        assert n_ctiles >= n_experts
        tile_e, tile_valid, last_valid, pad_start, pad_len = _segment_tables(
            counts, TM_EXP, n_tiles, n_ctiles)
        d0_tiles = dest[0].reshape(n_ctiles, 1, TM_CMB)
        d1_tiles = dest[1].reshape(n_ctiles, 1, TM_CMB)
        xs = _scatter(h2.reshape(n_tok, d), d0_tiles, d1_tiles, pad_start, pad_len, last_valid + 1,
                      n_tiles * TM_EXP)
        ys = _experts(xs, tile_e, tile_valid, last_valid, w_expert_gate[l].astype(_BF16),
                      w_expert_up[l].astype(_BF16), w_expert_down[l].astype(_BF16))

        dest_tiles = jnp.concatenate([d0_tiles, d1_tiles], axis=2)
        x = _combine(ys, dest_tiles, x1, route, gate2, row(norm_moe_post[l]))
    return x
```

```python
import functools

import jax
import jax.numpy as jnp
from jax import lax
from jax.experimental import pallas as pl
from jax.experimental.pallas import tpu as pltpu

EPS = 1e-6
HEAD_DIM = 128
KERNEL_A = 3
KERNEL_B = 31
N_MOD = 6
TOP_K = 2
ROUTE_LANES = 128
ROUTE_ROWS = 8
EXPERT_ROWS = 128
TQ_RANK = 1024
HIST_A = 8
HIST_B = 32
CONV_ROWS = 128
MIX_COLS = 256
W_CHUNKS = 8

TM_MIX = 256
TM_EXP = 256
TM_CMB = 256
ADA_COLS = 1024
VMEM_LIMIT = 56 * 1024 * 1024

_BF16 = jnp.bfloat16
_F32 = jnp.float32


def _dot(a, b):
    return jnp.dot(a, b, preferred_element_type=_F32)


def _rms(v):
    return lax.rsqrt(jnp.mean(v * v, axis=-1, keepdims=True) + EPS)


def _ada_kernel(c_ref, w_ref, b_ref, o_ref):
    c = c_ref[...]
    act = (c * jax.nn.sigmoid(c)).astype(_BF16)
    o_ref[...] = _dot(act, w_ref[...].astype(_BF16)) + b_ref[...]


def _ada(c, w_ada, b_ada):
    bsz, d = c.shape
    n_out = w_ada.shape[1]
    c_pad = jnp.zeros((8, d), _F32).at[:bsz].set(c)
    out = pl.pallas_call(
        _ada_kernel,
        grid=(n_out // ADA_COLS,),
        in_specs=[
            pl.BlockSpec((8, d), lambda j: (0, 0)),
            pl.BlockSpec((d, ADA_COLS), lambda j: (0, j)),
            pl.BlockSpec((1, ADA_COLS), lambda j: (0, j)),
        ],
        out_specs=pl.BlockSpec((8, ADA_COLS), lambda j: (0, j)),
        out_shape=jax.ShapeDtypeStruct((8, n_out), _F32),
        compiler_params=pltpu.CompilerParams(
            dimension_semantics=("arbitrary",), vmem_limit_bytes=VMEM_LIMIT),
        name="ada",
    )(c_pad, w_ada, b_ada.reshape(1, n_out))
    return out[:bsz]


def _dwconv_chunk(ext_ref, w_ref, r0, rows, lo, hi, hist, ksize):
    span = rows + hist
    first = hist - (ksize - 1)
    xe = ext_ref[r0:r0 + span, lo:hi]
    acc = None
    for b in range(8):
        offsets = [o for o in range(first, hist + 1) if o % 8 == b]
        if not offsets:
            continue
        shifted = xe if b == 0 else pltpu.roll(xe, span - b, 0)
        for o in offsets:
            k = o - first
            term = w_ref[k:k + 1, lo:hi] * shifted[o - b:o - b + rows, :]
            acc = term if acc is None else acc + term
    return acc


def _mix_in_kernel(x_ref, sh_ref, sc_ref, g_ref, w_ref, ca_ref, na_ref, cb_ref, cbb_ref,
                   lg_ref, lb_ref, y_ref, cv_ext, u_ext, ba_buf, conv_b_out, *, tm, da, db):
    nc = MIX_COLS
    @pl.when(pl.program_id(1) == 0)
    def _():
        cv_ext[...] = jnp.zeros(cv_ext.shape, _F32)
        u_ext[0:HIST_B, :] = jnp.zeros((HIST_B, db), _F32)
        ba_buf[...] = jnp.zeros(ba_buf.shape, _F32)
        conv_b_out[...] = jnp.zeros(conv_b_out.shape, _F32)

    x = x_ref[0]
    h = x * _rms(x) * g_ref[...]
    h = h * (1.0 + sc_ref[0]) + sh_ref[0]
    hb = h.astype(_BF16)

    def finish_head(hd):
        lo, hi = hd * HEAD_DIM, (hd + 1) * HEAD_DIM
        for r0 in range(0, tm, CONV_ROWS):
            conv = _dwconv_chunk(cv_ext, ca_ref, r0, CONV_ROWS, lo, hi, HIST_A, KERNEL_A)
            seg = ba_buf[r0:r0 + CONV_ROWS, lo:hi] * conv
            y_ref[0, r0:r0 + CONV_ROWS, lo:hi] = (seg * _rms(seg) * na_ref[:, lo:hi]).astype(_BF16)

    def finish_rows(r0, rows):
        acc = conv_b_out[r0:r0 + rows, :]
        mu = jnp.mean(acc, axis=-1, keepdims=True)
        cen = acc - mu
        var = jnp.mean(cen * cen, axis=-1, keepdims=True)
        yn = cen * lax.rsqrt(var + EPS) * lg_ref[...] + lb_ref[...]
        y_ref[0, r0:r0 + rows, da:da + db] = (yn * jax.nn.sigmoid(yn)).astype(_BF16)

    n_b = db // nc
    heads_per = (da // HEAD_DIM) // n_b
    ln_rows = tm // n_b
    for j in range(n_b):
        lo = j * nc
        val = _dot(hb, w_ref[:, 3 * da + lo:3 * da + lo + nc])
        gate = _dot(hb, w_ref[:, 3 * da + db + lo:3 * da + db + lo + nc])
        for hd in range(j * heads_per, (j + 1) * heads_per):
            finish_head(hd)
        finish_rows(j * ln_rows, ln_rows)
        u_ext[HIST_B:HIST_B + tm, lo:lo + nc] = val * jax.nn.sigmoid(gate)
    cv_ext[0:HIST_A, :] = cv_ext[tm:tm + HIST_A, :]

    conv_chunks = [(lo, r0) for lo in range(0, db, HEAD_DIM) for r0 in range(0, tm, CONV_ROWS)]
    n_a = da // nc
    n_mm = 3 * n_a
    done = 0
    for q in range(n_mm):
        j = q // 2 if q < 2 * n_a else q - 2 * n_a
        lo = j * nc
        if q < 2 * n_a and q % 2 == 0:
            c_part = _dot(hb, w_ref[:, da + lo:da + lo + nc])
        elif q < 2 * n_a:
            v_part = _dot(hb, w_ref[:, 2 * da + lo:2 * da + lo + nc])
            cv_ext[HIST_A:HIST_A + tm, lo:lo + nc] = c_part * v_part
        else:
            ba_buf[:, lo:lo + nc] = _dot(hb, w_ref[:, lo:lo + nc])
        upto = (q + 1) * len(conv_chunks) // n_mm
        for lo_c, r0 in conv_chunks[done:upto]:
            conv = _dwconv_chunk(u_ext, cb_ref, r0, CONV_ROWS, lo_c, lo_c + HEAD_DIM, HIST_B, KERNEL_B)
            conv_b_out[r0:r0 + CONV_ROWS, lo_c:lo_c + HEAD_DIM] = conv + cbb_ref[:, lo_c:lo_c + HEAD_DIM]
        done = upto
    u_ext[0:HIST_B, :] = u_ext[tm:tm + HIST_B, :]


def _mix_in(x, shift, scale, g_pre, w_in_bf, conv_a, norm_a, conv_b, conv_b_bias, ln_g, ln_b):
    bsz, t, d = x.shape
    da = conv_a.shape[1]
    db = conv_b.shape[1]
    tm = TM_MIX
    full = lambda shape: pl.BlockSpec(shape, lambda b, s: (0,) * len(shape))
    per_b = pl.BlockSpec((1, 1, d), lambda b, s: (b, 0, 0))
    n_t = t // tm
    return pl.pallas_call(
        functools.partial(_mix_in_kernel, tm=tm, da=da, db=db),
        grid=(bsz, n_t + 1),
        in_specs=[
            pl.BlockSpec((1, tm, d), lambda b, s: (b, jnp.minimum(s, n_t - 1), 0)),
            per_b, per_b,
            full((1, d)),
            pl.BlockSpec(w_in_bf.shape, lambda b, s: (0, 0), pipeline_mode=pl.Buffered(1)),
            full((KERNEL_A, da)), full((1, da)),
            full((KERNEL_B, db)), full((1, db)), full((1, db)), full((1, db)),
        ],
        out_specs=pl.BlockSpec((1, tm, da + db), lambda b, s: (b, jnp.maximum(s - 1, 0), 0)),
        out_shape=jax.ShapeDtypeStruct((bsz, t, da + db), _BF16),
        scratch_shapes=[pltpu.VMEM((HIST_A + tm, da), _F32), pltpu.VMEM((HIST_B + tm, db), _F32),
                        pltpu.VMEM((tm, da), _F32), pltpu.VMEM((tm, db), _F32)],
        compiler_params=pltpu.CompilerParams(
            dimension_semantics=("arbitrary", "arbitrary"), vmem_limit_bytes=VMEM_LIMIT),
        name="mix_in",
    )(x, shift, scale, g_pre, w_in_bf, conv_a, norm_a, conv_b, conv_b_bias, ln_g, ln_b)


def _mix_out_kernel(y_ref, x_ref, g1_ref, sh2_ref, sc2_ref, npost_ref, npre_ref, wout_ref, wr_ref,
                    br_ref, x1_ref, h2_ref, route_ref, route_t_ref, *, n_groups, per_group):
    o = _dot(y_ref[0], wout_ref[...])
    x1 = x_ref[0] + g1_ref[0] * (o * _rms(o) * npost_ref[...])
    x1_ref[0] = x1
    h2 = x1 * _rms(x1) * npre_ref[...]
    h2 = h2 * (1.0 + sc2_ref[0]) + sh2_ref[0]
    h2_ref[0] = h2

    logits = _dot(h2.astype(_BF16), wr_ref[...]) + br_ref[...]
    lane = lax.broadcasted_iota(jnp.int32, logits.shape, 1).astype(_F32)
    neg = jnp.float32(-jnp.inf)
    big = jnp.float32(ROUTE_LANES)
    is_coarse = lane < n_groups
    coarse = jnp.where(is_coarse, logits, neg)
    m = jnp.max(coarse, axis=-1, keepdims=True)
    g_sel = jnp.min(jnp.where(coarse == m, lane, big), axis=-1, keepdims=True)
    p_g = 1.0 / jnp.sum(jnp.where(is_coarse, jnp.exp(logits - m), 0.0), axis=-1, keepdims=True)
    lo = n_groups + g_sel * per_group
    fine = jnp.where((lane >= lo) & (lane < lo + per_group), logits, neg)
    v1 = jnp.max(fine, axis=-1, keepdims=True)
    l1 = jnp.min(jnp.where(fine == v1, lane, big), axis=-1, keepdims=True)
    fine2 = jnp.where(lane == l1, neg, fine)
    v2 = jnp.max(fine2, axis=-1, keepdims=True)
    l2 = jnp.min(jnp.where(fine2 == v2, lane, big), axis=-1, keepdims=True)
    e2 = jnp.exp(v2 - v1)
    w1 = p_g / (1.0 + e2)
    w2 = p_g * e2 / (1.0 + e2)
    route = jnp.where(lane == 0, l1 - n_groups,
                      jnp.where(lane == 1, l2 - n_groups,
                                jnp.where(lane == 2, w1, jnp.where(lane == 3, w2, 0.0))))
    route_ref[0] = route
    route_t_ref[...] = jnp.transpose(route)[0:ROUTE_ROWS, :]


def _mix_out(y, x, gate1, shift2, scale2, n_post, n_pre, w_out_bf, w_route, b_route, n_groups,
             per_group):
    bsz, t, d = x.shape
    tm = TM_MIX
    full = lambda shape: pl.BlockSpec(shape, lambda b, s: (0,) * len(shape))
    per_b = pl.BlockSpec((1, 1, d), lambda b, s: (b, 0, 0))
    tile = pl.BlockSpec((1, tm, d), lambda b, s: (b, s, 0))
    return pl.pallas_call(
        functools.partial(_mix_out_kernel, n_groups=n_groups, per_group=per_group),
        grid=(bsz, t // tm),
        in_specs=[
            tile, tile, per_b, per_b, per_b, full((1, d)), full((1, d)),
            pl.BlockSpec(w_out_bf.shape, lambda b, s: (0, 0), pipeline_mode=pl.Buffered(1)),
            full((d, ROUTE_LANES)), full((1, ROUTE_LANES)),
        ],
        out_specs=[tile, tile, pl.BlockSpec((1, tm, ROUTE_LANES), lambda b, s: (b, s, 0)),
                   pl.BlockSpec((ROUTE_ROWS, tm), lambda b, s: (0, b * (t // tm) + s))],
        out_shape=[
            jax.ShapeDtypeStruct((bsz, t, d), _F32),
            jax.ShapeDtypeStruct((bsz, t, d), _F32),
            jax.ShapeDtypeStruct((bsz, t, ROUTE_LANES), _F32),
            jax.ShapeDtypeStruct((ROUTE_ROWS, bsz * t), _F32),
        ],
        compiler_params=pltpu.CompilerParams(
            dimension_semantics=("arbitrary", "arbitrary"), vmem_limit_bytes=VMEM_LIMIT),
        name="mix_out",
    )(y, x, gate1, shift2, scale2, n_post, n_pre, w_out_bf, w_route, b_route)


def _rank_kernel(rt_ref, u_ref, dest_ref, cnt_ref, carry, pbase, *, tq, tm):
    phase = pl.program_id(0)
    j = pl.program_id(1)
    sub = lax.broadcasted_iota(jnp.int32, (EXPERT_ROWS, tq), 0).astype(_F32)
    oh0 = sub == rt_ref[0:1, :]
    oh1 = sub == rt_ref[1:2, :]
    both = jnp.where(oh0, 1.0, 0.0) + jnp.where(oh1, 1.0, 0.0)
    tile_counts = jnp.sum(both, axis=1, keepdims=True)

    @pl.when(jnp.logical_and(phase == 0, j == 0))
    def _():
        carry[...] = jnp.zeros(carry.shape, _F32)

    @pl.when(jnp.logical_and(phase == 1, j == 0))
    def _():
        cnt = carry[...]
        cnt_ref[...] = cnt
        tiles = jnp.floor((cnt + (tm - 1)) * (1.0 / tm))
        hi = jnp.floor(tiles * (1.0 / 16.0))
        lo = tiles - 16.0 * hi
        r = lax.broadcasted_iota(jnp.int32, (EXPERT_ROWS, EXPERT_ROWS), 0)
        c = lax.broadcasted_iota(jnp.int32, (EXPERT_ROWS, EXPERT_ROWS), 1)
        before = jnp.where(c < r, 1.0, 0.0).astype(_BF16)
        start_tiles = 16.0 * _dot(before, hi.astype(_BF16)) + _dot(before, lo.astype(_BF16))
        pbase[...] = start_tiles * tm
        carry[...] = jnp.zeros(carry.shape, _F32)

    @pl.when(phase == 1)
    def _():
        earlier = _dot(both.astype(_BF16), u_ref[...])
        slot = earlier + carry[:, 0:1] + pbase[:, 0:1]
        d0 = jnp.sum(jnp.where(oh0, slot, 0.0), axis=0, keepdims=True)
        d1 = jnp.sum(jnp.where(oh1, slot, 0.0), axis=0, keepdims=True)
        row = lax.broadcasted_iota(jnp.int32, (ROUTE_ROWS, tq), 0)
        dest_ref[...] = jnp.where(row == 0, d0, jnp.where(row == 1, d1, 0.0)).astype(jnp.int32)

    carry[...] = carry[...] + tile_counts


def _rank(route_t, tm):
    n_tok = route_t.shape[1]
    tq = TQ_RANK
    assert tm & (tm - 1) == 0, "segment padding must be a power of two for exact f32 arithmetic"
    idx = jnp.arange(tq, dtype=jnp.int32)
    earlier_mask = (idx[:, None] < idx[None, :]).astype(_BF16)
    return pl.pallas_call(
        functools.partial(_rank_kernel, tq=tq, tm=tm),
        grid=(2, n_tok // tq),
        in_specs=[
            pl.BlockSpec((ROUTE_ROWS, tq), lambda p, j: (0, j)),
            pl.BlockSpec((tq, tq), lambda p, j: (0, 0)),
        ],
        out_specs=[pl.BlockSpec((ROUTE_ROWS, tq), lambda p, j: (0, j * p)),
                   pl.BlockSpec((EXPERT_ROWS, ROUTE_LANES), lambda p, j: (0, 0))],
        out_shape=[jax.ShapeDtypeStruct((ROUTE_ROWS, n_tok), jnp.int32),
                   jax.ShapeDtypeStruct((EXPERT_ROWS, ROUTE_LANES), _F32)],
        scratch_shapes=[pltpu.VMEM((EXPERT_ROWS, ROUTE_LANES), _F32),
                        pltpu.VMEM((EXPERT_ROWS, ROUTE_LANES), _F32)],
        compiler_params=pltpu.CompilerParams(
            dimension_semantics=("arbitrary", "arbitrary"), vmem_limit_bytes=VMEM_LIMIT),
        name="rank",
    )(route_t, earlier_mask)


def _row_copy(src, dst, sem):
    return pltpu.make_async_copy(src, dst, sem)


def _scatter_kernel(ps_ref, pn_ref, used_ref, d0_ref, d1_ref, h_ref, xs_hbm, zrow, sem, psem, *, tk,
                    tm):
    i = pl.program_id(0)

    @pl.when(i == 0)
    def _():
        zrow[...] = jnp.zeros(zrow.shape, _F32)

    @pl.when(i == pl.num_programs(0) - 1)
    def _():
        n_rows = xs_hbm.shape[0]

        def fill(t, carry):
            dst = xs_hbm.at[pl.ds(pl.multiple_of(t * tm, tm), tm)]
            _row_copy(zrow, dst, psem).start()
            _row_copy(zrow, dst, psem).wait()
            return carry

        lax.fori_loop(used_ref[0], n_rows // tm, fill, 0)

    for r in range(tk):
        src = h_ref.at[pl.ds(r, 1)]
        _row_copy(src, xs_hbm.at[pl.ds(d0_ref[0, 0, r], 1)], sem).start()
        _row_copy(src, xs_hbm.at[pl.ds(d1_ref[0, 0, r], 1)], sem).start()

    pad_start = ps_ref[i]
    n_pad = pn_ref[i]
    zsrc = zrow.at[pl.ds(0, 1)]

    def start_pad(r, carry):
        _row_copy(zsrc, xs_hbm.at[pl.ds(pad_start + r, 1)], psem).start()
        return carry

    def wait_pad(r, carry):
        _row_copy(zsrc, xs_hbm.at[pl.ds(0, 1)], psem).wait()
        return carry

    lax.fori_loop(0, n_pad, start_pad, 0)
    lax.fori_loop(0, n_pad, wait_pad, 0)
    for _ in range(TOP_K):
        _row_copy(h_ref, xs_hbm.at[pl.ds(0, tk)], sem).wait()


def _scatter(h2_flat, d0_tiles, d1_tiles, pad_start, pad_len, used_tiles, n_rows):
    n_tok, d = h2_flat.shape
    tk = TM_CMB
    tm = TM_EXP
    smem_tile = pl.BlockSpec((1, 1, tk), lambda i, ps, pn, used: (i, 0, 0), memory_space=pltpu.SMEM)
    return pl.pallas_call(
        functools.partial(_scatter_kernel, tk=tk, tm=tm),
        grid_spec=pltpu.PrefetchScalarGridSpec(
            num_scalar_prefetch=3,
            grid=(n_tok // tk,),
            in_specs=[smem_tile, smem_tile, pl.BlockSpec((tk, d), lambda i, ps, pn, used: (i, 0))],
            out_specs=pl.BlockSpec(memory_space=pl.ANY),
            scratch_shapes=[pltpu.VMEM((tm, d), _F32), pltpu.SemaphoreType.DMA(()),
                            pltpu.SemaphoreType.DMA(())],
        ),
        out_shape=jax.ShapeDtypeStruct((n_rows, d), _F32),
        compiler_params=pltpu.CompilerParams(
            dimension_semantics=("arbitrary",), vmem_limit_bytes=VMEM_LIMIT),
        name="scatter",
    )(pad_start, pad_len, used_tiles, d0_tiles, d1_tiles, h2_flat)


def _expert_kernel(tv_ref, par_ref, glo_ref, ghi_ref, sege_ref, ntot_ref, last_ref,
                   xs_ref, wg_hbm, wu_hbm, wd_hbm, ys_ref,
                   wgb, wub, wdb, sg, su, sd, sem, *, layer, nch):
    del last_ref
    i = pl.program_id(0)
    rg = wgb.shape[1] // nch
    rd = wdb.shape[1] // nch

    def chunk_copies(g):
        e = sege_ref[g // nch]
        c = g % nch
        s = g % 2
        rows_g = pl.ds(pl.multiple_of(c * rg, rg), rg)
        rows_d = pl.ds(pl.multiple_of(c * rd, rd), rd)
        return (_row_copy(wg_hbm.at[layer, e, rows_g], sg.at[s], sem.at[s]),
                _row_copy(wu_hbm.at[layer, e, rows_g], su.at[s], sem.at[s]),
                _row_copy(wd_hbm.at[layer, e, rows_d], sd.at[s], sem.at[s]))

    def start_chunk(g):
        for cp in chunk_copies(g):
            cp.start()

    def process_chunk(g, carry):
        for cp in chunk_copies(g):
            cp.wait()
        c = g % nch
        s = g % 2
        p = (g // nch) % 2
        wgb[p, pl.ds(pl.multiple_of(c * rg, rg), rg), :] = sg[s].astype(_BF16)
        wub[p, pl.ds(pl.multiple_of(c * rg, rg), rg), :] = su[s].astype(_BF16)
        wdb[p, pl.ds(pl.multiple_of(c * rd, rd), rd), :] = sd[s].astype(_BF16)

        @pl.when(g + 2 < ntot_ref[0])
        def _():
            start_chunk(g + 2)
        return carry

    @pl.when(i == 0)
    def _():
        start_chunk(0)
        start_chunk(1)
        lax.fori_loop(0, nch, process_chunk, 0)

    valid = tv_ref[i] == 1

    @pl.when(valid)
    def _():
        lax.fori_loop(glo_ref[i], ghi_ref[i], process_chunk, 0)
        p = par_ref[i]
        xb = xs_ref[...].astype(_BF16)
        g = _dot(xb, wgb[p])
        u = _dot(xb, wub[p])
        hid = (g * jax.nn.sigmoid(g) * u).astype(_BF16)
        ys_ref[...] = _dot(hid, wdb[p])

    @pl.when(jnp.logical_not(valid))
    def _():
        ys_ref[...] = jnp.zeros(ys_ref.shape, _F32)


def _experts(xs, tables, w_gate, w_up, w_down, layer):
    n_rows, d = xs.shape
    tm = TM_EXP
    f = w_gate.shape[3]
    nch = W_CHUNKS
    any_spec = pl.BlockSpec(memory_space=pl.ANY)
    return pl.pallas_call(
        functools.partial(_expert_kernel, layer=layer, nch=nch),
        grid_spec=pltpu.PrefetchScalarGridSpec(
            num_scalar_prefetch=7,
            grid=(n_rows // tm,),
            in_specs=[
                pl.BlockSpec((tm, d), lambda i, *s: (jnp.minimum(i, s[6][0]), 0)),
                any_spec, any_spec, any_spec,
            ],
            out_specs=pl.BlockSpec((tm, d), lambda i, *s: (i, 0)),
            scratch_shapes=[
                pltpu.VMEM((2, d, f), _BF16), pltpu.VMEM((2, d, f), _BF16), pltpu.VMEM((2, f, d), _BF16),
                pltpu.VMEM((2, d // nch, f), _F32), pltpu.VMEM((2, d // nch, f), _F32),
                pltpu.VMEM((2, f // nch, d), _F32), pltpu.SemaphoreType.DMA((2,)),
            ],
        ),
        out_shape=jax.ShapeDtypeStruct((n_rows, d), _F32),
        compiler_params=pltpu.CompilerParams(
            dimension_semantics=("arbitrary",), vmem_limit_bytes=VMEM_LIMIT),
        name="experts",
    )(*tables, xs, w_gate, w_up, w_down)


def _start_row_gather(src_hbm, idx_ref, dst, sem, n_rows):
    for r in range(n_rows):
        _row_copy(src_hbm.at[pl.ds(idx_ref[0, 0, r], 1)], dst.at[pl.ds(r, 1)], sem).start()


def _wait_row_gather(src_hbm, dst, sem, n_rows):
    _row_copy(src_hbm.at[pl.ds(0, n_rows)], dst, sem).wait()


def _combine_kernel(d_cur, d_nxt, ys_hbm, x1_ref, route_ref, g2_ref, npost_ref, o_ref, rbuf, sem, *,
                    tk):
    i = pl.program_id(0)
    n = pl.num_programs(0)
    slot = i % 2

    @pl.when(i == 0)
    def _():
        _start_row_gather(ys_hbm, d_cur, rbuf.at[0], sem.at[0], TOP_K * tk)

    _wait_row_gather(ys_hbm, rbuf.at[slot], sem.at[slot], TOP_K * tk)

    @pl.when(i + 1 < n)
    def _():
        _start_row_gather(ys_hbm, d_nxt, rbuf.at[1 - slot], sem.at[1 - slot], TOP_K * tk)

    route = route_ref[0]
    y = route[:, 2:3] * rbuf[slot, 0:tk, :] + route[:, 3:4] * rbuf[slot, tk:2 * tk, :]
    o_ref[0] = x1_ref[0] + g2_ref[0] * (y * _rms(y) * npost_ref[...])


def _combine(ys, dest_tiles, x1, route, gate2, n_post):
    bsz, t, d = x1.shape
    tk = TM_CMB
    per_seq = t // tk
    n_tiles = bsz * per_seq
    smem_tile = lambda imap: pl.BlockSpec((1, 1, TOP_K * tk), imap, memory_space=pltpu.SMEM)
    return pl.pallas_call(
        functools.partial(_combine_kernel, tk=tk),
        grid=(n_tiles,),
        in_specs=[
            smem_tile(lambda i: (i, 0, 0)),
            smem_tile(lambda i: (jnp.minimum(i + 1, n_tiles - 1), 0, 0)),
            pl.BlockSpec(memory_space=pl.ANY),
            pl.BlockSpec((1, tk, d), lambda i: (i // per_seq, i % per_seq, 0)),
            pl.BlockSpec((1, tk, ROUTE_LANES), lambda i: (i // per_seq, i % per_seq, 0)),
            pl.BlockSpec((1, 1, d), lambda i: (i // per_seq, 0, 0)),
            pl.BlockSpec((1, d), lambda i: (0, 0)),
        ],
        out_specs=pl.BlockSpec((1, tk, d), lambda i: (i // per_seq, i % per_seq, 0)),
        out_shape=jax.ShapeDtypeStruct((bsz, t, d), _F32),
        scratch_shapes=[pltpu.VMEM((2, TOP_K * tk, d), _F32), pltpu.SemaphoreType.DMA((2,))],
        compiler_params=pltpu.CompilerParams(
            dimension_semantics=("arbitrary",), vmem_limit_bytes=VMEM_LIMIT),
        name="combine",
    )(dest_tiles, dest_tiles, ys, x1, route, gate2, n_post)


def _segment_tables(counts, tm, n_tiles, n_steps, nch):
    n_experts = counts.shape[0]
    i32 = jnp.int32
    nt = (counts + tm - 1) // tm
    tend = jnp.cumsum(nt)
    tstart = tend - nt
    tile_idx = jnp.arange(n_tiles, dtype=i32)
    tile_e = jnp.minimum(jnp.sum(tend[None, :] <= tile_idx[:, None], axis=1), n_experts - 1)
    valid = tile_idx < tend[-1]
    nonempty = nt > 0
    seg_of_e = jnp.cumsum(nonempty.astype(i32)) - 1
    n_seg = seg_of_e[-1] + 1
    ids = jnp.arange(n_experts, dtype=i32)
    seg_expert = jnp.sum(
        jnp.where(nonempty[None, :] & (seg_of_e[None, :] == ids[:, None]), ids[None, :], 0), axis=1)
    k = seg_of_e[tile_e]
    j = tile_idx - tstart[tile_e]
    ntt = jnp.maximum(nt[tile_e], 1)
    has_next = valid & (k + 1 < n_seg)
    glo = jnp.where(has_next, (k + 1) * nch + j * nch // ntt, 0)
    ghi = jnp.where(has_next, (k + 1) * nch + (j + 1) * nch // ntt, 0)
    expert_tables = (valid.astype(i32), (k % 2).astype(i32), glo.astype(i32), ghi.astype(i32),
                     seg_expert.astype(i32), (n_seg * nch).reshape(1).astype(i32),
                     (tend[-1:] - 1).astype(i32))
    pad_start = jnp.zeros((n_steps,), i32).at[:n_experts].set(tstart * tm + counts)
    pad_len = jnp.zeros((n_steps,), i32).at[:n_experts].set(nt * tm - counts)
    return expert_tables, pad_start, pad_len, tend[-1:].astype(i32)


def kernel(x, c, w_ada, b_ada, norm_mix_pre, norm_mix_post, w_in, conv_a, norm_a_out, conv_b, conv_b_bias, ln_b_gain, ln_b_bias, w_out, norm_moe_pre, norm_moe_post, w_router_group, b_router_group, w_router_expert, b_router_expert, w_expert_gate, w_expert_up, w_expert_down):
    bsz, t, d = x.shape
    depth = w_ada.shape[0]
    n_groups = w_router_group.shape[2]
    per_group = w_router_expert.shape[3]
    n_experts = n_groups * per_group
    n_tok = bsz * t
    n_assign = n_tok * TOP_K
    n_tiles = (n_assign + n_experts * (TM_EXP - 1) + TM_EXP - 1) // TM_EXP
    row = lambda v: v.reshape(1, -1)

    for l in range(depth):
        mod = _ada(c, w_ada[l], b_ada[l])
        shift1, scale1, gate1, shift2, scale2, gate2 = [
            m.reshape(bsz, 1, d) for m in jnp.split(mod, N_MOD, axis=-1)]

        y = _mix_in(x, shift1, scale1, row(norm_mix_pre[l]), w_in[l].astype(_BF16), conv_a[l],
                    row(norm_a_out[l]), conv_b[l], row(conv_b_bias[l]), row(ln_b_gain[l]),
                    row(ln_b_bias[l]))

        n_route = n_groups + n_experts
        w_fine = jnp.transpose(w_router_expert[l], (1, 0, 2)).reshape(d, n_experts)
        w_route = jnp.zeros((d, ROUTE_LANES), _F32).at[:, :n_route].set(
            jnp.concatenate([w_router_group[l], w_fine], axis=1)).astype(_BF16)
        b_route = jnp.zeros((1, ROUTE_LANES), _F32).at[0, :n_route].set(
            jnp.concatenate([b_router_group[l], b_router_expert[l].reshape(-1)]))
        x1, h2, route, route_t = _mix_out(y, x, gate1, shift2, scale2, row(norm_mix_post[l]),
                                          row(norm_moe_pre[l]), w_out[l].astype(_BF16), w_route,
                                          b_route, n_groups, per_group)

        dest, counts_f = _rank(route_t, TM_EXP)
        counts = counts_f[:n_experts, 0].astype(jnp.int32)
        n_ctiles = n_tok // TM_CMB
        assert n_ctiles >= n_experts
        expert_tables, pad_start, pad_len, used_tiles = _segment_tables(
            counts, TM_EXP, n_tiles, n_ctiles, W_CHUNKS)
        d0_tiles = dest[0].reshape(n_ctiles, 1, TM_CMB)
        d1_tiles = dest[1].reshape(n_ctiles, 1, TM_CMB)
        xs = _scatter(h2.reshape(n_tok, d), d0_tiles, d1_tiles, pad_start, pad_len, used_tiles,
                      n_tiles * TM_EXP)
        ys = _experts(xs, expert_tables, w_expert_gate, w_expert_up, w_expert_down, l)

        dest_tiles = jnp.concatenate([d0_tiles, d1_tiles], axis=2)
        x = _combine(ys, dest_tiles, x1, route, gate2, row(norm_moe_post[l]))
    return x
```

```python
import functools

import jax
import jax.numpy as jnp
from jax import lax
from jax.experimental import pallas as pl
from jax.experimental.pallas import tpu as pltpu

EPS = 1e-6
HEAD_DIM = 128
KERNEL_A = 3
KERNEL_B = 31
N_MOD = 6
TOP_K = 2
ROUTE_LANES = 128
ROUTE_ROWS = 8
EXPERT_ROWS = 128
TQ_RANK = 1024
HIST_A = 8
HIST_B = 32
CONV_ROWS = 128
MIX_COLS = 256
W_CHUNKS = 8

TM_MIX = 256
TM_EXP = 256
TM_CMB = 256
ADA_COLS = 1024
VMEM_LIMIT = 56 * 1024 * 1024

_BF16 = jnp.bfloat16
_F32 = jnp.float32


def _dot(a, b):
    return jnp.dot(a, b, preferred_element_type=_F32)


def _rms(v):
    return lax.rsqrt(jnp.mean(v * v, axis=-1, keepdims=True) + EPS)


def _ada_kernel(c_ref, w_ref, b_ref, o_ref):
    c = c_ref[...]
    act = (c * jax.nn.sigmoid(c)).astype(_BF16)
    o_ref[...] = _dot(act, w_ref[...].astype(_BF16)) + b_ref[...]


def _ada(c, w_ada, b_ada):
    bsz, d = c.shape
    n_out = w_ada.shape[1]
    c_pad = jnp.zeros((8, d), _F32).at[:bsz].set(c)
    out = pl.pallas_call(
        _ada_kernel,
        grid=(n_out // ADA_COLS,),
        in_specs=[
            pl.BlockSpec((8, d), lambda j: (0, 0)),
            pl.BlockSpec((d, ADA_COLS), lambda j: (0, j)),
            pl.BlockSpec((1, ADA_COLS), lambda j: (0, j)),
        ],
        out_specs=pl.BlockSpec((8, ADA_COLS), lambda j: (0, j)),
        out_shape=jax.ShapeDtypeStruct((8, n_out), _F32),
        compiler_params=pltpu.CompilerParams(
            dimension_semantics=("arbitrary",), vmem_limit_bytes=VMEM_LIMIT),
        name="ada",
    )(c_pad, w_ada, b_ada.reshape(1, n_out))
    return out[:bsz]


def _dwconv_chunk(ext_ref, w_ref, r0, rows, lo, hi, hist, ksize):
    span = rows + hist
    first = hist - (ksize - 1)
    xe = ext_ref[r0:r0 + span, lo:hi]
    acc = None
    for b in range(8):
        offsets = [o for o in range(first, hist + 1) if o % 8 == b]
        if not offsets:
            continue
        shifted = xe if b == 0 else pltpu.roll(xe, span - b, 0)
        for o in offsets:
            k = o - first
            term = w_ref[k:k + 1, lo:hi] * shifted[o - b:o - b + rows, :]
            acc = term if acc is None else acc + term
    return acc


def _mix_in_kernel(x_ref, sh_ref, sc_ref, g_ref, w_ref, ca_ref, na_ref, cb_ref, cbb_ref,
                   lg_ref, lb_ref, y_ref, cv_ext, u_ext, ba_buf, conv_b_out, *, tm, da, db):
    nc = MIX_COLS
    @pl.when(pl.program_id(1) == 0)
    def _():
        cv_ext[...] = jnp.zeros(cv_ext.shape, _F32)
        u_ext[0:HIST_B, :] = jnp.zeros((HIST_B, db), _F32)
        ba_buf[...] = jnp.zeros(ba_buf.shape, _F32)
        conv_b_out[...] = jnp.zeros(conv_b_out.shape, _F32)

    x = x_ref[0]
    h = x * _rms(x) * g_ref[...]
    h = h * (1.0 + sc_ref[0]) + sh_ref[0]
    hb = h.astype(_BF16)

    def finish_head(hd):
        lo, hi = hd * HEAD_DIM, (hd + 1) * HEAD_DIM
        for r0 in range(0, tm, CONV_ROWS):
            conv = _dwconv_chunk(cv_ext, ca_ref, r0, CONV_ROWS, lo, hi, HIST_A, KERNEL_A)
            seg = ba_buf[r0:r0 + CONV_ROWS, lo:hi] * conv
            y_ref[0, r0:r0 + CONV_ROWS, lo:hi] = (seg * _rms(seg) * na_ref[:, lo:hi]).astype(_BF16)

    def finish_rows(r0, rows):
        acc = conv_b_out[r0:r0 + rows, :]
        mu = jnp.mean(acc, axis=-1, keepdims=True)
        cen = acc - mu
        var = jnp.mean(cen * cen, axis=-1, keepdims=True)
        yn = cen * lax.rsqrt(var + EPS) * lg_ref[...] + lb_ref[...]
        y_ref[0, r0:r0 + rows, da:da + db] = (yn * jax.nn.sigmoid(yn)).astype(_BF16)

    n_b = db // nc
    heads_per = (da // HEAD_DIM) // n_b
    ln_rows = tm // n_b
    for j in range(n_b):
        lo = j * nc
        val = _dot(hb, w_ref[:, 3 * da + lo:3 * da + lo + nc])
        gate = _dot(hb, w_ref[:, 3 * da + db + lo:3 * da + db + lo + nc])
        for hd in range(j * heads_per, (j + 1) * heads_per):
            finish_head(hd)
        finish_rows(j * ln_rows, ln_rows)
        u_ext[HIST_B:HIST_B + tm, lo:lo + nc] = val * jax.nn.sigmoid(gate)
    cv_ext[0:HIST_A, :] = cv_ext[tm:tm + HIST_A, :]

    conv_chunks = [(lo, r0) for lo in range(0, db, HEAD_DIM) for r0 in range(0, tm, CONV_ROWS)]
    n_a = da // nc
    n_mm = 3 * n_a
    done = 0
    for q in range(n_mm):
        j = q // 2 if q < 2 * n_a else q - 2 * n_a
        lo = j * nc
        if q < 2 * n_a and q % 2 == 0:
            c_part = _dot(hb, w_ref[:, da + lo:da + lo + nc])
        elif q < 2 * n_a:
            v_part = _dot(hb, w_ref[:, 2 * da + lo:2 * da + lo + nc])
            cv_ext[HIST_A:HIST_A + tm, lo:lo + nc] = c_part * v_part
        else:
            ba_buf[:, lo:lo + nc] = _dot(hb, w_ref[:, lo:lo + nc])
        upto = (q + 1) * len(conv_chunks) // n_mm
        for lo_c, r0 in conv_chunks[done:upto]:
            conv = _dwconv_chunk(u_ext, cb_ref, r0, CONV_ROWS, lo_c, lo_c + HEAD_DIM, HIST_B, KERNEL_B)
            conv_b_out[r0:r0 + CONV_ROWS, lo_c:lo_c + HEAD_DIM] = conv + cbb_ref[:, lo_c:lo_c + HEAD_DIM]
        done = upto
    u_ext[0:HIST_B, :] = u_ext[tm:tm + HIST_B, :]


def _mix_in(x, shift, scale, g_pre, w_in_bf, conv_a, norm_a, conv_b, conv_b_bias, ln_g, ln_b):
    bsz, t, d = x.shape
    da = conv_a.shape[1]
    db = conv_b.shape[1]
    tm = TM_MIX
    full = lambda shape: pl.BlockSpec(shape, lambda b, s: (0,) * len(shape))
    per_b = pl.BlockSpec((1, 1, d), lambda b, s: (b, 0, 0))
    n_t = t // tm
    return pl.pallas_call(
        functools.partial(_mix_in_kernel, tm=tm, da=da, db=db),
        grid=(bsz, n_t + 1),
        in_specs=[
            pl.BlockSpec((1, tm, d), lambda b, s: (b, jnp.minimum(s, n_t - 1), 0)),
            per_b, per_b,
            full((1, d)),
            pl.BlockSpec(w_in_bf.shape, lambda b, s: (0, 0), pipeline_mode=pl.Buffered(1)),
            full((KERNEL_A, da)), full((1, da)),
            full((KERNEL_B, db)), full((1, db)), full((1, db)), full((1, db)),
        ],
        out_specs=pl.BlockSpec((1, tm, da + db), lambda b, s: (b, jnp.maximum(s - 1, 0), 0)),
        out_shape=jax.ShapeDtypeStruct((bsz, t, da + db), _BF16),
        scratch_shapes=[pltpu.VMEM((HIST_A + tm, da), _F32), pltpu.VMEM((HIST_B + tm, db), _F32),
                        pltpu.VMEM((tm, da), _F32), pltpu.VMEM((tm, db), _F32)],
        compiler_params=pltpu.CompilerParams(
            dimension_semantics=("arbitrary", "arbitrary"), vmem_limit_bytes=VMEM_LIMIT),
        name="mix_in",
    )(x, shift, scale, g_pre, w_in_bf, conv_a, norm_a, conv_b, conv_b_bias, ln_g, ln_b)


def _mix_out_kernel(y_ref, x_ref, g1_ref, sh2_ref, sc2_ref, npost_ref, npre_ref, wout_ref, wr_ref,
                    br_ref, x1_ref, h2_ref, route_ref, route_t_ref, *, n_groups, per_group):
    o = _dot(y_ref[0], wout_ref[...])
    x1 = x_ref[0] + g1_ref[0] * (o * _rms(o) * npost_ref[...])
    x1_ref[0] = x1
    h2 = x1 * _rms(x1) * npre_ref[...]
    h2 = h2 * (1.0 + sc2_ref[0]) + sh2_ref[0]
    h2_ref[0] = h2

    logits = _dot(h2.astype(_BF16), wr_ref[...]) + br_ref[...]
    lane = lax.broadcasted_iota(jnp.int32, logits.shape, 1).astype(_F32)
    neg = jnp.float32(-jnp.inf)
    big = jnp.float32(ROUTE_LANES)
    is_coarse = lane < n_groups
    coarse = jnp.where(is_coarse, logits, neg)
    m = jnp.max(coarse, axis=-1, keepdims=True)
    g_sel = jnp.min(jnp.where(coarse == m, lane, big), axis=-1, keepdims=True)
    p_g = 1.0 / jnp.sum(jnp.where(is_coarse, jnp.exp(logits - m), 0.0), axis=-1, keepdims=True)
    lo = n_groups + g_sel * per_group
    fine = jnp.where((lane >= lo) & (lane < lo + per_group), logits, neg)
    v1 = jnp.max(fine, axis=-1, keepdims=True)
    l1 = jnp.min(jnp.where(fine == v1, lane, big), axis=-1, keepdims=True)
    fine2 = jnp.where(lane == l1, neg, fine)
    v2 = jnp.max(fine2, axis=-1, keepdims=True)
    l2 = jnp.min(jnp.where(fine2 == v2, lane, big), axis=-1, keepdims=True)
    e2 = jnp.exp(v2 - v1)
    w1 = p_g / (1.0 + e2)
    w2 = p_g * e2 / (1.0 + e2)
    route = jnp.where(lane == 0, l1 - n_groups,
                      jnp.where(lane == 1, l2 - n_groups,
                                jnp.where(lane == 2, w1, jnp.where(lane == 3, w2, 0.0))))
    route_ref[0] = route
    route_t_ref[...] = jnp.transpose(route)[0:ROUTE_ROWS, :]


def _mix_out(y, x, gate1, shift2, scale2, n_post, n_pre, w_out_bf, w_route, b_route, n_groups,
             per_group):
    bsz, t, d = x.shape
    tm = TM_MIX
    full = lambda shape: pl.BlockSpec(shape, lambda b, s: (0,) * len(shape))
    per_b = pl.BlockSpec((1, 1, d), lambda b, s: (b, 0, 0))
    tile = pl.BlockSpec((1, tm, d), lambda b, s: (b, s, 0))
    return pl.pallas_call(
        functools.partial(_mix_out_kernel, n_groups=n_groups, per_group=per_group),
        grid=(bsz, t // tm),
        in_specs=[
            tile, tile, per_b, per_b, per_b, full((1, d)), full((1, d)),
            pl.BlockSpec(w_out_bf.shape, lambda b, s: (0, 0), pipeline_mode=pl.Buffered(1)),
            full((d, ROUTE_LANES)), full((1, ROUTE_LANES)),
        ],
        out_specs=[tile, tile, pl.BlockSpec((1, tm, ROUTE_LANES), lambda b, s: (b, s, 0)),
                   pl.BlockSpec((ROUTE_ROWS, tm), lambda b, s: (0, b * (t // tm) + s))],
        out_shape=[
            jax.ShapeDtypeStruct((bsz, t, d), _F32),
            jax.ShapeDtypeStruct((bsz, t, d), _F32),
            jax.ShapeDtypeStruct((bsz, t, ROUTE_LANES), _F32),
            jax.ShapeDtypeStruct((ROUTE_ROWS, bsz * t), _F32),
        ],
        compiler_params=pltpu.CompilerParams(
            dimension_semantics=("arbitrary", "arbitrary"), vmem_limit_bytes=VMEM_LIMIT),
        name="mix_out",
    )(y, x, gate1, shift2, scale2, n_post, n_pre, w_out_bf, w_route, b_route)


def _rank_kernel(rt_ref, u_ref, dest_ref, cnt_ref, carry, pbase, *, tq, tm):
    phase = pl.program_id(0)
    j = pl.program_id(1)
    sub = lax.broadcasted_iota(jnp.int32, (EXPERT_ROWS, tq), 0).astype(_F32)
    oh0 = sub == rt_ref[0:1, :]
    oh1 = sub == rt_ref[1:2, :]
    both = jnp.where(oh0, 1.0, 0.0) + jnp.where(oh1, 1.0, 0.0)
    tile_counts = jnp.sum(both, axis=1, keepdims=True)

    @pl.when(jnp.logical_and(phase == 0, j == 0))
    def _():
        carry[...] = jnp.zeros(carry.shape, _F32)

    @pl.when(jnp.logical_and(phase == 1, j == 0))
    def _():
        cnt = carry[...]
        cnt_ref[...] = cnt
        tiles = jnp.floor((cnt + (tm - 1)) * (1.0 / tm))
        hi = jnp.floor(tiles * (1.0 / 16.0))
        lo = tiles - 16.0 * hi
        r = lax.broadcasted_iota(jnp.int32, (EXPERT_ROWS, EXPERT_ROWS), 0)
        c = lax.broadcasted_iota(jnp.int32, (EXPERT_ROWS, EXPERT_ROWS), 1)
        before = jnp.where(c < r, 1.0, 0.0).astype(_BF16)
        start_tiles = 16.0 * _dot(before, hi.astype(_BF16)) + _dot(before, lo.astype(_BF16))
        pbase[...] = start_tiles * tm
        carry[...] = jnp.zeros(carry.shape, _F32)

    @pl.when(phase == 1)
    def _():
        earlier = _dot(both.astype(_BF16), u_ref[...])
        slot = earlier + carry[:, 0:1] + pbase[:, 0:1]
        d0 = jnp.sum(jnp.where(oh0, slot, 0.0), axis=0, keepdims=True)
        d1 = jnp.sum(jnp.where(oh1, slot, 0.0), axis=0, keepdims=True)
        row = lax.broadcasted_iota(jnp.int32, (ROUTE_ROWS, tq), 0)
        dest_ref[...] = jnp.where(row == 0, d0, jnp.where(row == 1, d1, 0.0)).astype(jnp.int32)

    carry[...] = carry[...] + tile_counts


def _rank(route_t, tm):
    n_tok = route_t.shape[1]
    tq = TQ_RANK
    assert tm & (tm - 1) == 0, "segment padding must be a power of two for exact f32 arithmetic"
    idx = jnp.arange(tq, dtype=jnp.int32)
    earlier_mask = (idx[:, None] < idx[None, :]).astype(_BF16)
    return pl.pallas_call(
        functools.partial(_rank_kernel, tq=tq, tm=tm),
        grid=(2, n_tok // tq),
        in_specs=[
            pl.BlockSpec((ROUTE_ROWS, tq), lambda p, j: (0, j)),
            pl.BlockSpec((tq, tq), lambda p, j: (0, 0)),
        ],
        out_specs=[pl.BlockSpec((ROUTE_ROWS, tq), lambda p, j: (0, j * p)),
                   pl.BlockSpec((EXPERT_ROWS, ROUTE_LANES), lambda p, j: (0, 0))],
        out_shape=[jax.ShapeDtypeStruct((ROUTE_ROWS, n_tok), jnp.int32),
                   jax.ShapeDtypeStruct((EXPERT_ROWS, ROUTE_LANES), _F32)],
        scratch_shapes=[pltpu.VMEM((EXPERT_ROWS, ROUTE_LANES), _F32),
                        pltpu.VMEM((EXPERT_ROWS, ROUTE_LANES), _F32)],
        compiler_params=pltpu.CompilerParams(
            dimension_semantics=("arbitrary", "arbitrary"), vmem_limit_bytes=VMEM_LIMIT),
        name="rank",
    )(route_t, earlier_mask)


def _row_copy(src, dst, sem):
    return pltpu.make_async_copy(src, dst, sem)


def _scatter_kernel(ps_ref, pn_ref, used_ref, d0_ref, d1_ref, h_ref, xs_hbm, zrow, sem, psem, *, tk,
                    tm):
    i = pl.program_id(0)

    @pl.when(i == 0)
    def _():
        zrow[...] = jnp.zeros(zrow.shape, _F32)

    @pl.when(i == pl.num_programs(0) - 1)
    def _():
        n_rows = xs_hbm.shape[0]

        def fill(t, carry):
            dst = xs_hbm.at[pl.ds(pl.multiple_of(t * tm, tm), tm)]
            _row_copy(zrow, dst, psem).start()
            _row_copy(zrow, dst, psem).wait()
            return carry

        lax.fori_loop(used_ref[0], n_rows // tm, fill, 0)

    for r in range(tk):
        src = h_ref.at[pl.ds(r, 1)]
        _row_copy(src, xs_hbm.at[pl.ds(d0_ref[0, 0, r], 1)], sem).start(priority=0)
        _row_copy(src, xs_hbm.at[pl.ds(d1_ref[0, 0, r], 1)], sem).start(priority=1)

    pad_start = ps_ref[i]
    n_pad = pn_ref[i]
    zsrc = zrow.at[pl.ds(0, 1)]

    def start_pad(r, carry):
        _row_copy(zsrc, xs_hbm.at[pl.ds(pad_start + r, 1)], psem).start()
        return carry

    def wait_pad(r, carry):
        _row_copy(zsrc, xs_hbm.at[pl.ds(0, 1)], psem).wait()
        return carry

    lax.fori_loop(0, n_pad, start_pad, 0)
    lax.fori_loop(0, n_pad, wait_pad, 0)
    for _ in range(TOP_K):
        _row_copy(h_ref, xs_hbm.at[pl.ds(0, tk)], sem).wait()


def _scatter(h2_flat, d0_tiles, d1_tiles, pad_start, pad_len, used_tiles, n_rows):
    n_tok, d = h2_flat.shape
    tk = TM_CMB
    tm = TM_EXP
    smem_tile = pl.BlockSpec((1, 1, tk), lambda i, ps, pn, used: (i, 0, 0), memory_space=pltpu.SMEM)
    return pl.pallas_call(
        functools.partial(_scatter_kernel, tk=tk, tm=tm),
        grid_spec=pltpu.PrefetchScalarGridSpec(
            num_scalar_prefetch=3,
            grid=(n_tok // tk,),
            in_specs=[smem_tile, smem_tile, pl.BlockSpec((tk, d), lambda i, ps, pn, used: (i, 0))],
            out_specs=pl.BlockSpec(memory_space=pl.ANY),
            scratch_shapes=[pltpu.VMEM((tm, d), _F32), pltpu.SemaphoreType.DMA(()),
                            pltpu.SemaphoreType.DMA(())],
        ),
        out_shape=jax.ShapeDtypeStruct((n_rows, d), _F32),
        compiler_params=pltpu.CompilerParams(
            dimension_semantics=("arbitrary",), vmem_limit_bytes=VMEM_LIMIT),
        name="scatter",
    )(pad_start, pad_len, used_tiles, d0_tiles, d1_tiles, h2_flat)


def _expert_kernel(tv_ref, par_ref, glo_ref, ghi_ref, sege_ref, ntot_ref, last_ref,
                   xs_ref, wg_hbm, wu_hbm, wd_hbm, ys_ref,
                   wgb, wub, wdb, sg, su, sd, sem, *, layer, nch):
    del last_ref
    i = pl.program_id(0)
    rg = wgb.shape[1] // nch
    rd = wdb.shape[1] // nch

    def chunk_copies(g):
        e = sege_ref[g // nch]
        c = g % nch
        s = g % 2
        rows_g = pl.ds(pl.multiple_of(c * rg, rg), rg)
        rows_d = pl.ds(pl.multiple_of(c * rd, rd), rd)
        return (_row_copy(wg_hbm.at[layer, e, rows_g], sg.at[s], sem.at[s]),
                _row_copy(wu_hbm.at[layer, e, rows_g], su.at[s], sem.at[s]),
                _row_copy(wd_hbm.at[layer, e, rows_d], sd.at[s], sem.at[s]))

    def start_chunk(g):
        for cp in chunk_copies(g):
            cp.start()

    def process_chunk(g, carry):
        for cp in chunk_copies(g):
            cp.wait()
        c = g % nch
        s = g % 2
        p = (g // nch) % 2
        wgb[p, pl.ds(pl.multiple_of(c * rg, rg), rg), :] = sg[s].astype(_BF16)
        wub[p, pl.ds(pl.multiple_of(c * rg, rg), rg), :] = su[s].astype(_BF16)
        wdb[p, pl.ds(pl.multiple_of(c * rd, rd), rd), :] = sd[s].astype(_BF16)

        @pl.when(g + 2 < ntot_ref[0])
        def _():
            start_chunk(g + 2)
        return carry

    @pl.when(i == 0)
    def _():
        start_chunk(0)
        start_chunk(1)
        lax.fori_loop(0, nch, process_chunk, 0)

    valid = tv_ref[i] == 1

    @pl.when(valid)
    def _():
        lax.fori_loop(glo_ref[i], ghi_ref[i], process_chunk, 0)
        p = par_ref[i]
        xb = xs_ref[...].astype(_BF16)
        g = _dot(xb, wgb[p])
        u = _dot(xb, wub[p])
        hid = (g * jax.nn.sigmoid(g) * u).astype(_BF16)
        ys_ref[...] = _dot(hid, wdb[p])

    @pl.when(jnp.logical_not(valid))
    def _():
        ys_ref[...] = jnp.zeros(ys_ref.shape, _F32)


def _experts(xs, tables, w_gate, w_up, w_down, layer):
    n_rows, d = xs.shape
    tm = TM_EXP
    f = w_gate.shape[3]
    nch = W_CHUNKS
    any_spec = pl.BlockSpec(memory_space=pl.ANY)
    return pl.pallas_call(
        functools.partial(_expert_kernel, layer=layer, nch=nch),
        grid_spec=pltpu.PrefetchScalarGridSpec(
            num_scalar_prefetch=7,
            grid=(n_rows // tm,),
            in_specs=[
                pl.BlockSpec((tm, d), lambda i, *s: (jnp.minimum(i, s[6][0]), 0)),
                any_spec, any_spec, any_spec,
            ],
            out_specs=pl.BlockSpec((tm, d), lambda i, *s: (i, 0)),
            scratch_shapes=[
                pltpu.VMEM((2, d, f), _BF16), pltpu.VMEM((2, d, f), _BF16), pltpu.VMEM((2, f, d), _BF16),
                pltpu.VMEM((2, d // nch, f), _F32), pltpu.VMEM((2, d // nch, f), _F32),
                pltpu.VMEM((2, f // nch, d), _F32), pltpu.SemaphoreType.DMA((2,)),
            ],
        ),
        out_shape=jax.ShapeDtypeStruct((n_rows, d), _F32),
        compiler_params=pltpu.CompilerParams(
            dimension_semantics=("arbitrary",), vmem_limit_bytes=VMEM_LIMIT),
        name="experts",
    )(*tables, xs, w_gate, w_up, w_down)


def _start_row_gather(src_hbm, idx_ref, dst, sem, n_rows):
    for r in range(n_rows):
        _row_copy(src_hbm.at[pl.ds(idx_ref[0, 0, r], 1)], dst.at[pl.ds(r, 1)], sem).start(
            priority=r % 2)


def _wait_row_gather(src_hbm, dst, sem, n_rows):
    _row_copy(src_hbm.at[pl.ds(0, n_rows)], dst, sem).wait()


def _combine_kernel(d_cur, d_nxt, ys_hbm, x1_ref, route_ref, g2_ref, npost_ref, o_ref, rbuf, sem, *,
                    tk):
    i = pl.program_id(0)
    n = pl.num_programs(0)
    slot = i % 2

    @pl.when(i == 0)
    def _():
        _start_row_gather(ys_hbm, d_cur, rbuf.at[0], sem.at[0], TOP_K * tk)

    _wait_row_gather(ys_hbm, rbuf.at[slot], sem.at[slot], TOP_K * tk)

    @pl.when(i + 1 < n)
    def _():
        _start_row_gather(ys_hbm, d_nxt, rbuf.at[1 - slot], sem.at[1 - slot], TOP_K * tk)

    route = route_ref[0]
    y = route[:, 2:3] * rbuf[slot, 0:tk, :] + route[:, 3:4] * rbuf[slot, tk:2 * tk, :]
    o_ref[0] = x1_ref[0] + g2_ref[0] * (y * _rms(y) * npost_ref[...])


def _combine(ys, dest_tiles, x1, route, gate2, n_post):
    bsz, t, d = x1.shape
    tk = TM_CMB
    per_seq = t // tk
    n_tiles = bsz * per_seq
    smem_tile = lambda imap: pl.BlockSpec((1, 1, TOP_K * tk), imap, memory_space=pltpu.SMEM)
    return pl.pallas_call(
        functools.partial(_combine_kernel, tk=tk),
        grid=(n_tiles,),
        in_specs=[
            smem_tile(lambda i: (i, 0, 0)),
            smem_tile(lambda i: (jnp.minimum(i + 1, n_tiles - 1), 0, 0)),
            pl.BlockSpec(memory_space=pl.ANY),
            pl.BlockSpec((1, tk, d), lambda i: (i // per_seq, i % per_seq, 0)),
            pl.BlockSpec((1, tk, ROUTE_LANES), lambda i: (i // per_seq, i % per_seq, 0)),
            pl.BlockSpec((1, 1, d), lambda i: (i // per_seq, 0, 0)),
            pl.BlockSpec((1, d), lambda i: (0, 0)),
        ],
        out_specs=pl.BlockSpec((1, tk, d), lambda i: (i // per_seq, i % per_seq, 0)),
        out_shape=jax.ShapeDtypeStruct((bsz, t, d), _F32),
        scratch_shapes=[pltpu.VMEM((2, TOP_K * tk, d), _F32), pltpu.SemaphoreType.DMA((2,))],
        compiler_params=pltpu.CompilerParams(
            dimension_semantics=("arbitrary",), vmem_limit_bytes=VMEM_LIMIT),
        name="combine",
    )(dest_tiles, dest_tiles, ys, x1, route, gate2, n_post)


def _segment_tables(counts, tm, n_tiles, n_steps, nch):
    n_experts = counts.shape[0]
    i32 = jnp.int32
    nt = (counts + tm - 1) // tm
    tend = jnp.cumsum(nt)
    tstart = tend - nt
    tile_idx = jnp.arange(n_tiles, dtype=i32)
    valid = tile_idx < tend[-1]
    nonempty = nt > 0
    seg_of_e = jnp.cumsum(nonempty.astype(i32)) - 1
    n_seg = seg_of_e[-1] + 1
    ids = jnp.arange(n_experts, dtype=i32)
    seg_expert = jnp.sum(
        jnp.where(nonempty[None, :] & (seg_of_e[None, :] == ids[:, None]), ids[None, :], 0), axis=1)
    owns = (tstart[None, :] <= tile_idx[:, None]) & (tile_idx[:, None] < tend[None, :])
    pick = lambda v: jnp.sum(jnp.where(owns, v[None, :], 0), axis=1)
    k = pick(seg_of_e)
    j = tile_idx - pick(tstart)
    ntt = jnp.maximum(pick(nt), 1)
    has_next = valid & (k + 1 < n_seg)
    glo = jnp.where(has_next, (k + 1) * nch + j * nch // ntt, 0)
    ghi = jnp.where(has_next, (k + 1) * nch + (j + 1) * nch // ntt, 0)
    expert_tables = (valid.astype(i32), (k % 2).astype(i32), glo.astype(i32), ghi.astype(i32),
                     seg_expert.astype(i32), (n_seg * nch).reshape(1).astype(i32),
                     (tend[-1:] - 1).astype(i32))
    pad_start = jnp.zeros((n_steps,), i32).at[:n_experts].set(tstart * tm + counts)
    pad_len = jnp.zeros((n_steps,), i32).at[:n_experts].set(nt * tm - counts)
    return expert_tables, pad_start, pad_len, tend[-1:].astype(i32)


def kernel(x, c, w_ada, b_ada, norm_mix_pre, norm_mix_post, w_in, conv_a, norm_a_out, conv_b, conv_b_bias, ln_b_gain, ln_b_bias, w_out, norm_moe_pre, norm_moe_post, w_router_group, b_router_group, w_router_expert, b_router_expert, w_expert_gate, w_expert_up, w_expert_down):
    bsz, t, d = x.shape
    depth = w_ada.shape[0]
    n_groups = w_router_group.shape[2]
    per_group = w_router_expert.shape[3]
    n_experts = n_groups * per_group
    n_tok = bsz * t
    n_assign = n_tok * TOP_K
    n_tiles = (n_assign + n_experts * (TM_EXP - 1) + TM_EXP - 1) // TM_EXP
    row = lambda v: v.reshape(1, -1)

    for l in range(depth):
        mod = _ada(c, w_ada[l], b_ada[l])
        shift1, scale1, gate1, shift2, scale2, gate2 = [
            m.reshape(bsz, 1, d) for m in jnp.split(mod, N_MOD, axis=-1)]

        y = _mix_in(x, shift1, scale1, row(norm_mix_pre[l]), w_in[l].astype(_BF16), conv_a[l],
                    row(norm_a_out[l]), conv_b[l], row(conv_b_bias[l]), row(ln_b_gain[l]),
                    row(ln_b_bias[l]))

        n_route = n_groups + n_experts
        w_fine = jnp.transpose(w_router_expert[l], (1, 0, 2)).reshape(d, n_experts)
        w_route = jnp.zeros((d, ROUTE_LANES), _F32).at[:, :n_route].set(
            jnp.concatenate([w_router_group[l], w_fine], axis=1)).astype(_BF16)
        b_route = jnp.zeros((1, ROUTE_LANES), _F32).at[0, :n_route].set(
            jnp.concatenate([b_router_group[l], b_router_expert[l].reshape(-1)]))
        x1, h2, route, route_t = _mix_out(y, x, gate1, shift2, scale2, row(norm_mix_post[l]),
                                          row(norm_moe_pre[l]), w_out[l].astype(_BF16), w_route,
                                          b_route, n_groups, per_group)

        dest, counts_f = _rank(route_t, TM_EXP)
        counts = counts_f[:n_experts, 0].astype(jnp.int32)
        n_ctiles = n_tok // TM_CMB
        assert n_ctiles >= n_experts
        expert_tables, pad_start, pad_len, used_tiles = _segment_tables(
            counts, TM_EXP, n_tiles, n_ctiles, W_CHUNKS)
        d0_tiles = dest[0].reshape(n_ctiles, 1, TM_CMB)
        d1_tiles = dest[1].reshape(n_ctiles, 1, TM_CMB)
        xs = _scatter(h2.reshape(n_tok, d), d0_tiles, d1_tiles, pad_start, pad_len, used_tiles,
                      n_tiles * TM_EXP)
        ys = _experts(xs, expert_tables, w_expert_gate, w_expert_up, w_expert_down, l)

        dest_tiles = jnp.concatenate([d0_tiles, d1_tiles], axis=2)
        x = _combine(ys, dest_tiles, x1, route, gate2, row(norm_moe_post[l]))
    return x
```

```python
import functools

import jax
import jax.numpy as jnp
from jax import lax
from jax.experimental import pallas as pl
from jax.experimental.pallas import tpu as pltpu

EPS = 1e-6
HEAD_DIM = 128
KERNEL_A = 3
KERNEL_B = 31
N_MOD = 6
TOP_K = 2
ROUTE_LANES = 128
ROUTE_ROWS = 8
EXPERT_ROWS = 128
TQ_RANK = 1024
HIST_A = 8
HIST_B = 32
CONV_ROWS = 128
MIX_COLS = 256
W_CHUNKS = 8

TM_MIX = 256
TM_EXP = 256
TM_CMB = 256
ADA_COLS = 1024
VMEM_LIMIT = 56 * 1024 * 1024

_BF16 = jnp.bfloat16
_F32 = jnp.float32


def _dot(a, b):
    return jnp.dot(a, b, preferred_element_type=_F32)


def _rms(v):
    return lax.rsqrt(jnp.mean(v * v, axis=-1, keepdims=True) + EPS)


def _pack_bf16_pairs(v):
    n = v.shape[1] // 2
    bits = pltpu.bitcast(v.astype(_BF16).astype(_F32), jnp.uint32)
    return (bits[:, :n] >> 16) | bits[:, n:]


def _unpack_bf16_pairs(p):
    lo = pltpu.bitcast(p << 16, _F32)
    hi = pltpu.bitcast(p & jnp.uint32(0xFFFF0000), _F32)
    return lo, hi


def _ada_kernel(c_ref, w_ref, b_ref, o_ref):
    c = c_ref[...]
    act = (c * jax.nn.sigmoid(c)).astype(_BF16)
    o_ref[...] = _dot(act, w_ref[...].astype(_BF16)) + b_ref[...]


def _ada(c, w_ada, b_ada):
    bsz, d = c.shape
    n_out = w_ada.shape[1]
    c_pad = jnp.zeros((8, d), _F32).at[:bsz].set(c)
    out = pl.pallas_call(
        _ada_kernel,
        grid=(n_out // ADA_COLS,),
        in_specs=[
            pl.BlockSpec((8, d), lambda j: (0, 0)),
            pl.BlockSpec((d, ADA_COLS), lambda j: (0, j)),
            pl.BlockSpec((1, ADA_COLS), lambda j: (0, j)),
        ],
        out_specs=pl.BlockSpec((8, ADA_COLS), lambda j: (0, j)),
        out_shape=jax.ShapeDtypeStruct((8, n_out), _F32),
        compiler_params=pltpu.CompilerParams(
            dimension_semantics=("arbitrary",), vmem_limit_bytes=VMEM_LIMIT),
        name="ada",
    )(c_pad, w_ada, b_ada.reshape(1, n_out))
    return out[:bsz]


def _dwconv_chunk(ext_ref, w_ref, r0, rows, lo, hi, hist, ksize):
    span = rows + hist
    first = hist - (ksize - 1)
    xe = ext_ref[r0:r0 + span, lo:hi]
    acc = None
    for b in range(8):
        offsets = [o for o in range(first, hist + 1) if o % 8 == b]
        if not offsets:
            continue
        shifted = xe if b == 0 else pltpu.roll(xe, span - b, 0)
        for o in offsets:
            k = o - first
            term = w_ref[k:k + 1, lo:hi] * shifted[o - b:o - b + rows, :]
            acc = term if acc is None else acc + term
    return acc


def _mix_in_kernel(x_ref, sh_ref, sc_ref, g_ref, w_ref, ca_ref, na_ref, cb_ref, cbb_ref,
                   lg_ref, lb_ref, y_ref, cv_ext, u_ext, ba_buf, conv_b_out, *, tm, da, db):
    nc = MIX_COLS
    @pl.when(pl.program_id(1) == 0)
    def _():
        cv_ext[...] = jnp.zeros(cv_ext.shape, _F32)
        u_ext[0:HIST_B, :] = jnp.zeros((HIST_B, db), _F32)
        ba_buf[...] = jnp.zeros(ba_buf.shape, _F32)
        conv_b_out[...] = jnp.zeros(conv_b_out.shape, _F32)

    x = x_ref[0]
    h = x * _rms(x) * g_ref[...]
    h = h * (1.0 + sc_ref[0]) + sh_ref[0]
    hb = h.astype(_BF16)

    def finish_head(hd):
        lo, hi = hd * HEAD_DIM, (hd + 1) * HEAD_DIM
        for r0 in range(0, tm, CONV_ROWS):
            conv = _dwconv_chunk(cv_ext, ca_ref, r0, CONV_ROWS, lo, hi, HIST_A, KERNEL_A)
            seg = ba_buf[r0:r0 + CONV_ROWS, lo:hi] * conv
            y_ref[0, r0:r0 + CONV_ROWS, lo:hi] = (seg * _rms(seg) * na_ref[:, lo:hi]).astype(_BF16)

    def finish_rows(r0, rows):
        acc = conv_b_out[r0:r0 + rows, :]
        mu = jnp.mean(acc, axis=-1, keepdims=True)
        cen = acc - mu
        var = jnp.mean(cen * cen, axis=-1, keepdims=True)
        yn = cen * lax.rsqrt(var + EPS) * lg_ref[...] + lb_ref[...]
        y_ref[0, r0:r0 + rows, da:da + db] = (yn * jax.nn.sigmoid(yn)).astype(_BF16)

    n_b = db // nc
    heads_per = (da // HEAD_DIM) // n_b
    ln_rows = tm // n_b
    for j in range(n_b):
        lo = j * nc
        val = _dot(hb, w_ref[:, 3 * da + lo:3 * da + lo + nc])
        gate = _dot(hb, w_ref[:, 3 * da + db + lo:3 * da + db + lo + nc])
        for hd in range(j * heads_per, (j + 1) * heads_per):
            finish_head(hd)
        finish_rows(j * ln_rows, ln_rows)
        u_ext[HIST_B:HIST_B + tm, lo:lo + nc] = val * jax.nn.sigmoid(gate)
    cv_ext[0:HIST_A, :] = cv_ext[tm:tm + HIST_A, :]

    conv_chunks = [(lo, r0) for lo in range(0, db, HEAD_DIM) for r0 in range(0, tm, CONV_ROWS)]
    n_a = da // nc
    n_mm = 3 * n_a
    done = 0
    for q in range(n_mm):
        j = q // 2 if q < 2 * n_a else q - 2 * n_a
        lo = j * nc
        if q < 2 * n_a and q % 2 == 0:
            c_part = _dot(hb, w_ref[:, da + lo:da + lo + nc])
        elif q < 2 * n_a:
            v_part = _dot(hb, w_ref[:, 2 * da + lo:2 * da + lo + nc])
            cv_ext[HIST_A:HIST_A + tm, lo:lo + nc] = c_part * v_part
        else:
            ba_buf[:, lo:lo + nc] = _dot(hb, w_ref[:, lo:lo + nc])
        upto = (q + 1) * len(conv_chunks) // n_mm
        for lo_c, r0 in conv_chunks[done:upto]:
            conv = _dwconv_chunk(u_ext, cb_ref, r0, CONV_ROWS, lo_c, lo_c + HEAD_DIM, HIST_B, KERNEL_B)
            conv_b_out[r0:r0 + CONV_ROWS, lo_c:lo_c + HEAD_DIM] = conv + cbb_ref[:, lo_c:lo_c + HEAD_DIM]
        done = upto
    u_ext[0:HIST_B, :] = u_ext[tm:tm + HIST_B, :]


def _mix_in(x, shift, scale, g_pre, w_in_bf, conv_a, norm_a, conv_b, conv_b_bias, ln_g, ln_b):
    bsz, t, d = x.shape
    da = conv_a.shape[1]
    db = conv_b.shape[1]
    tm = TM_MIX
    full = lambda shape: pl.BlockSpec(shape, lambda b, s: (0,) * len(shape))
    per_b = pl.BlockSpec((1, 1, d), lambda b, s: (b, 0, 0))
    n_t = t // tm
    return pl.pallas_call(
        functools.partial(_mix_in_kernel, tm=tm, da=da, db=db),
        grid=(bsz, n_t + 1),
        in_specs=[
            pl.BlockSpec((1, tm, d), lambda b, s: (b, jnp.minimum(s, n_t - 1), 0)),
            per_b, per_b,
            full((1, d)),
            pl.BlockSpec(w_in_bf.shape, lambda b, s: (0, 0), pipeline_mode=pl.Buffered(1)),
            full((KERNEL_A, da)), full((1, da)),
            full((KERNEL_B, db)), full((1, db)), full((1, db)), full((1, db)),
        ],
        out_specs=pl.BlockSpec((1, tm, da + db), lambda b, s: (b, jnp.maximum(s - 1, 0), 0)),
        out_shape=jax.ShapeDtypeStruct((bsz, t, da + db), _BF16),
        scratch_shapes=[pltpu.VMEM((HIST_A + tm, da), _F32), pltpu.VMEM((HIST_B + tm, db), _F32),
                        pltpu.VMEM((tm, da), _F32), pltpu.VMEM((tm, db), _F32)],
        compiler_params=pltpu.CompilerParams(
            dimension_semantics=("arbitrary", "arbitrary"), vmem_limit_bytes=VMEM_LIMIT),
        name="mix_in",
    )(x, shift, scale, g_pre, w_in_bf, conv_a, norm_a, conv_b, conv_b_bias, ln_g, ln_b)


def _mix_out_kernel(y_ref, x_ref, g1_ref, sh2_ref, sc2_ref, npost_ref, npre_ref, wout_ref, wr_ref,
                    br_ref, x1_ref, h2_ref, route_ref, route_t_ref, *, n_groups, per_group):
    o = _dot(y_ref[0], wout_ref[...])
    x1 = x_ref[0] + g1_ref[0] * (o * _rms(o) * npost_ref[...])
    x1_ref[0] = x1
    h2 = x1 * _rms(x1) * npre_ref[...]
    h2 = h2 * (1.0 + sc2_ref[0]) + sh2_ref[0]
    h2_ref[0] = _pack_bf16_pairs(h2)

    logits = _dot(h2.astype(_BF16), wr_ref[...]) + br_ref[...]
    lane = lax.broadcasted_iota(jnp.int32, logits.shape, 1).astype(_F32)
    neg = jnp.float32(-jnp.inf)
    big = jnp.float32(ROUTE_LANES)
    is_coarse = lane < n_groups
    coarse = jnp.where(is_coarse, logits, neg)
    m = jnp.max(coarse, axis=-1, keepdims=True)
    g_sel = jnp.min(jnp.where(coarse == m, lane, big), axis=-1, keepdims=True)
    p_g = 1.0 / jnp.sum(jnp.where(is_coarse, jnp.exp(logits - m), 0.0), axis=-1, keepdims=True)
    lo = n_groups + g_sel * per_group
    fine = jnp.where((lane >= lo) & (lane < lo + per_group), logits, neg)
    v1 = jnp.max(fine, axis=-1, keepdims=True)
    l1 = jnp.min(jnp.where(fine == v1, lane, big), axis=-1, keepdims=True)
    fine2 = jnp.where(lane == l1, neg, fine)
    v2 = jnp.max(fine2, axis=-1, keepdims=True)
    l2 = jnp.min(jnp.where(fine2 == v2, lane, big), axis=-1, keepdims=True)
    e2 = jnp.exp(v2 - v1)
    w1 = p_g / (1.0 + e2)
    w2 = p_g * e2 / (1.0 + e2)
    route = jnp.where(lane == 0, l1 - n_groups,
                      jnp.where(lane == 1, l2 - n_groups,
                                jnp.where(lane == 2, w1, jnp.where(lane == 3, w2, 0.0))))
    route_ref[0] = route
    route_t_ref[...] = jnp.transpose(route)[0:ROUTE_ROWS, :]


def _mix_out(y, x, gate1, shift2, scale2, n_post, n_pre, w_out_bf, w_route, b_route, n_groups,
             per_group):
    bsz, t, d = x.shape
    tm = TM_MIX
    full = lambda shape: pl.BlockSpec(shape, lambda b, s: (0,) * len(shape))
    per_b = pl.BlockSpec((1, 1, d), lambda b, s: (b, 0, 0))
    tile = pl.BlockSpec((1, tm, d), lambda b, s: (b, s, 0))
    return pl.pallas_call(
        functools.partial(_mix_out_kernel, n_groups=n_groups, per_group=per_group),
        grid=(bsz, t // tm),
        in_specs=[
            tile, tile, per_b, per_b, per_b, full((1, d)), full((1, d)),
            pl.BlockSpec(w_out_bf.shape, lambda b, s: (0, 0), pipeline_mode=pl.Buffered(1)),
            full((d, ROUTE_LANES)), full((1, ROUTE_LANES)),
        ],
        out_specs=[tile, pl.BlockSpec((1, tm, d // 2), lambda b, s: (b, s, 0)),
                   pl.BlockSpec((1, tm, ROUTE_LANES), lambda b, s: (b, s, 0)),
                   pl.BlockSpec((ROUTE_ROWS, tm), lambda b, s: (0, b * (t // tm) + s))],
        out_shape=[
            jax.ShapeDtypeStruct((bsz, t, d), _F32),
            jax.ShapeDtypeStruct((bsz, t, d // 2), jnp.uint32),
            jax.ShapeDtypeStruct((bsz, t, ROUTE_LANES), _F32),
            jax.ShapeDtypeStruct((ROUTE_ROWS, bsz * t), _F32),
        ],
        compiler_params=pltpu.CompilerParams(
            dimension_semantics=("arbitrary", "arbitrary"), vmem_limit_bytes=VMEM_LIMIT),
        name="mix_out",
    )(y, x, gate1, shift2, scale2, n_post, n_pre, w_out_bf, w_route, b_route)


def _rank_kernel(rt_ref, u_ref, dest_ref, cnt_ref, carry, pbase, *, tq, tm):
    phase = pl.program_id(0)
    j = pl.program_id(1)
    sub = lax.broadcasted_iota(jnp.int32, (EXPERT_ROWS, tq), 0).astype(_F32)
    oh0 = sub == rt_ref[0:1, :]
    oh1 = sub == rt_ref[1:2, :]
    both = jnp.where(oh0, 1.0, 0.0) + jnp.where(oh1, 1.0, 0.0)
    tile_counts = jnp.sum(both, axis=1, keepdims=True)

    @pl.when(jnp.logical_and(phase == 0, j == 0))
    def _():
        carry[...] = jnp.zeros(carry.shape, _F32)

    @pl.when(jnp.logical_and(phase == 1, j == 0))
    def _():
        cnt = carry[...]
        cnt_ref[...] = cnt
        tiles = jnp.floor((cnt + (tm - 1)) * (1.0 / tm))
        hi = jnp.floor(tiles * (1.0 / 16.0))
        lo = tiles - 16.0 * hi
        r = lax.broadcasted_iota(jnp.int32, (EXPERT_ROWS, EXPERT_ROWS), 0)
        c = lax.broadcasted_iota(jnp.int32, (EXPERT_ROWS, EXPERT_ROWS), 1)
        before = jnp.where(c < r, 1.0, 0.0).astype(_BF16)
        start_tiles = 16.0 * _dot(before, hi.astype(_BF16)) + _dot(before, lo.astype(_BF16))
        pbase[...] = start_tiles * tm
        carry[...] = jnp.zeros(carry.shape, _F32)

    @pl.when(phase == 1)
    def _():
        earlier = _dot(both.astype(_BF16), u_ref[...])
        slot = earlier + carry[:, 0:1] + pbase[:, 0:1]
        d0 = jnp.sum(jnp.where(oh0, slot, 0.0), axis=0, keepdims=True)
        d1 = jnp.sum(jnp.where(oh1, slot, 0.0), axis=0, keepdims=True)
        row = lax.broadcasted_iota(jnp.int32, (ROUTE_ROWS, tq), 0)
        dest_ref[...] = jnp.where(row == 0, d0, jnp.where(row == 1, d1, 0.0)).astype(jnp.int32)

    carry[...] = carry[...] + tile_counts


def _rank(route_t, tm):
    n_tok = route_t.shape[1]
    tq = TQ_RANK
    assert tm & (tm - 1) == 0, "segment padding must be a power of two for exact f32 arithmetic"
    idx = jnp.arange(tq, dtype=jnp.int32)
    earlier_mask = (idx[:, None] < idx[None, :]).astype(_BF16)
    return pl.pallas_call(
        functools.partial(_rank_kernel, tq=tq, tm=tm),
        grid=(2, n_tok // tq),
        in_specs=[
            pl.BlockSpec((ROUTE_ROWS, tq), lambda p, j: (0, j)),
            pl.BlockSpec((tq, tq), lambda p, j: (0, 0)),
        ],
        out_specs=[pl.BlockSpec((ROUTE_ROWS, tq), lambda p, j: (0, j * p)),
                   pl.BlockSpec((EXPERT_ROWS, ROUTE_LANES), lambda p, j: (0, 0))],
        out_shape=[jax.ShapeDtypeStruct((ROUTE_ROWS, n_tok), jnp.int32),
                   jax.ShapeDtypeStruct((EXPERT_ROWS, ROUTE_LANES), _F32)],
        scratch_shapes=[pltpu.VMEM((EXPERT_ROWS, ROUTE_LANES), _F32),
                        pltpu.VMEM((EXPERT_ROWS, ROUTE_LANES), _F32)],
        compiler_params=pltpu.CompilerParams(
            dimension_semantics=("arbitrary", "arbitrary"), vmem_limit_bytes=VMEM_LIMIT),
        name="rank",
    )(route_t, earlier_mask)


def _row_copy(src, dst, sem):
    return pltpu.make_async_copy(src, dst, sem)


def _scatter_kernel(ps_ref, pn_ref, used_ref, d0_ref, d1_ref, h_ref, xs_hbm, zrow, sem, psem, *, tk,
                    tm):
    i = pl.program_id(0)

    @pl.when(i == 0)
    def _():
        zrow[...] = jnp.zeros(zrow.shape, zrow.dtype)

    @pl.when(i == pl.num_programs(0) - 1)
    def _():
        n_rows = xs_hbm.shape[0]

        def fill(t, carry):
            dst = xs_hbm.at[pl.ds(pl.multiple_of(t * tm, tm), tm)]
            _row_copy(zrow, dst, psem).start()
            _row_copy(zrow, dst, psem).wait()
            return carry

        lax.fori_loop(used_ref[0], n_rows // tm, fill, 0)

    for r in range(tk):
        src = h_ref.at[pl.ds(r, 1)]
        _row_copy(src, xs_hbm.at[pl.ds(d0_ref[0, 0, r], 1)], sem).start(priority=0)
        _row_copy(src, xs_hbm.at[pl.ds(d1_ref[0, 0, r], 1)], sem).start(priority=1)

    pad_start = ps_ref[i]
    n_pad = pn_ref[i]
    zsrc = zrow.at[pl.ds(0, 1)]

    def start_pad(r, carry):
        _row_copy(zsrc, xs_hbm.at[pl.ds(pad_start + r, 1)], psem).start()
        return carry

    def wait_pad(r, carry):
        _row_copy(zsrc, xs_hbm.at[pl.ds(0, 1)], psem).wait()
        return carry

    lax.fori_loop(0, n_pad, start_pad, 0)
    lax.fori_loop(0, n_pad, wait_pad, 0)
    for _ in range(TOP_K):
        _row_copy(h_ref, xs_hbm.at[pl.ds(0, tk)], sem).wait()


def _scatter(h2_flat, d0_tiles, d1_tiles, pad_start, pad_len, used_tiles, n_rows):
    n_tok, d = h2_flat.shape
    tk = TM_CMB
    tm = TM_EXP
    smem_tile = pl.BlockSpec((1, 1, tk), lambda i, ps, pn, used: (i, 0, 0), memory_space=pltpu.SMEM)
    return pl.pallas_call(
        functools.partial(_scatter_kernel, tk=tk, tm=tm),
        grid_spec=pltpu.PrefetchScalarGridSpec(
            num_scalar_prefetch=3,
            grid=(n_tok // tk,),
            in_specs=[smem_tile, smem_tile, pl.BlockSpec((tk, d), lambda i, ps, pn, used: (i, 0))],
            out_specs=pl.BlockSpec(memory_space=pl.ANY),
            scratch_shapes=[pltpu.VMEM((tm, d), h2_flat.dtype), pltpu.SemaphoreType.DMA(()),
                            pltpu.SemaphoreType.DMA(())],
        ),
        out_shape=jax.ShapeDtypeStruct((n_rows, d), h2_flat.dtype),
        compiler_params=pltpu.CompilerParams(
            dimension_semantics=("arbitrary",), vmem_limit_bytes=VMEM_LIMIT),
        name="scatter",
    )(pad_start, pad_len, used_tiles, d0_tiles, d1_tiles, h2_flat)


def _expert_kernel(tv_ref, par_ref, glo_ref, ghi_ref, sege_ref, ntot_ref, last_ref,
                   xs_ref, wg_hbm, wu_hbm, wd_hbm, ys_ref,
                   wgb, wub, wdb, sg, su, sd, sem, *, layer, nch):
    del last_ref
    i = pl.program_id(0)
    rg = wgb.shape[1] // nch
    rd = wdb.shape[1] // nch

    def chunk_copies(g):
        e = sege_ref[g // nch]
        c = g % nch
        s = g % 2
        rows_g = pl.ds(pl.multiple_of(c * rg, rg), rg)
        rows_d = pl.ds(pl.multiple_of(c * rd, rd), rd)
        return (_row_copy(wg_hbm.at[layer, e, rows_g], sg.at[s], sem.at[s]),
                _row_copy(wu_hbm.at[layer, e, rows_g], su.at[s], sem.at[s]),
                _row_copy(wd_hbm.at[layer, e, rows_d], sd.at[s], sem.at[s]))

    def start_chunk(g):
        for cp in chunk_copies(g):
            cp.start()

    def process_chunk(g, carry):
        for cp in chunk_copies(g):
            cp.wait()
        c = g % nch
        s = g % 2
        p = (g // nch) % 2
        wgb[p, pl.ds(pl.multiple_of(c * rg, rg), rg), :] = sg[s].astype(_BF16)
        wub[p, pl.ds(pl.multiple_of(c * rg, rg), rg), :] = su[s].astype(_BF16)
        wdb[p, pl.ds(pl.multiple_of(c * rd, rd), rd), :] = sd[s].astype(_BF16)

        @pl.when(g + 2 < ntot_ref[0])
        def _():
            start_chunk(g + 2)
        return carry

    @pl.when(i == 0)
    def _():
        start_chunk(0)
        start_chunk(1)
        lax.fori_loop(0, nch, process_chunk, 0)

    valid = tv_ref[i] == 1

    @pl.when(valid)
    def _():
        lax.fori_loop(glo_ref[i], ghi_ref[i], process_chunk, 0)
        p = par_ref[i]
        lo, hi = _unpack_bf16_pairs(xs_ref[...])
        xb = jnp.concatenate([lo.astype(_BF16), hi.astype(_BF16)], axis=1)
        g = _dot(xb, wgb[p])
        u = _dot(xb, wub[p])
        hid = (g * jax.nn.sigmoid(g) * u).astype(_BF16)
        ys_ref[...] = _pack_bf16_pairs(_dot(hid, wdb[p]))

    @pl.when(jnp.logical_not(valid))
    def _():
        ys_ref[...] = jnp.zeros(ys_ref.shape, ys_ref.dtype)


def _experts(xs, tables, w_gate, w_up, w_down, layer):
    n_rows, dp = xs.shape
    d = w_gate.shape[2]
    tm = TM_EXP
    f = w_gate.shape[3]
    nch = W_CHUNKS
    any_spec = pl.BlockSpec(memory_space=pl.ANY)
    return pl.pallas_call(
        functools.partial(_expert_kernel, layer=layer, nch=nch),
        grid_spec=pltpu.PrefetchScalarGridSpec(
            num_scalar_prefetch=7,
            grid=(n_rows // tm,),
            in_specs=[
                pl.BlockSpec((tm, dp), lambda i, *s: (jnp.minimum(i, s[6][0]), 0)),
                any_spec, any_spec, any_spec,
            ],
            out_specs=pl.BlockSpec((tm, dp), lambda i, *s: (i, 0)),
            scratch_shapes=[
                pltpu.VMEM((2, d, f), _BF16), pltpu.VMEM((2, d, f), _BF16), pltpu.VMEM((2, f, d), _BF16),
                pltpu.VMEM((2, d // nch, f), _F32), pltpu.VMEM((2, d // nch, f), _F32),
                pltpu.VMEM((2, f // nch, d), _F32), pltpu.SemaphoreType.DMA((2,)),
            ],
        ),
        out_shape=jax.ShapeDtypeStruct((n_rows, dp), jnp.uint32),
        compiler_params=pltpu.CompilerParams(
            dimension_semantics=("arbitrary",), vmem_limit_bytes=VMEM_LIMIT),
        name="experts",
    )(*tables, xs, w_gate, w_up, w_down)


def _start_row_gather(src_hbm, idx_ref, dst, sem, n_rows):
    for r in range(n_rows):
        _row_copy(src_hbm.at[pl.ds(idx_ref[0, 0, r], 1)], dst.at[pl.ds(r, 1)], sem).start(
            priority=r % 2)


def _wait_row_gather(src_hbm, dst, sem, n_rows):
    _row_copy(src_hbm.at[pl.ds(0, n_rows)], dst, sem).wait()


def _combine_kernel(d_cur, d_nxt, ys_hbm, x1_ref, route_ref, g2_ref, npost_ref, o_ref, rbuf, sem, *,
                    tk):
    i = pl.program_id(0)
    n = pl.num_programs(0)
    slot = i % 2

    @pl.when(i == 0)
    def _():
        _start_row_gather(ys_hbm, d_cur, rbuf.at[0], sem.at[0], TOP_K * tk)

    _wait_row_gather(ys_hbm, rbuf.at[slot], sem.at[slot], TOP_K * tk)

    @pl.when(i + 1 < n)
    def _():
        _start_row_gather(ys_hbm, d_nxt, rbuf.at[1 - slot], sem.at[1 - slot], TOP_K * tk)

    route = route_ref[0]
    lo0, hi0 = _unpack_bf16_pairs(rbuf[slot, 0:tk, :])
    lo1, hi1 = _unpack_bf16_pairs(rbuf[slot, tk:2 * tk, :])
    w0 = route[:, 2:3]
    w1 = route[:, 3:4]
    y = jnp.concatenate([w0 * lo0 + w1 * lo1, w0 * hi0 + w1 * hi1], axis=1)
    o_ref[0] = x1_ref[0] + g2_ref[0] * (y * _rms(y) * npost_ref[...])


def _combine(ys, dest_tiles, x1, route, gate2, n_post):
    bsz, t, d = x1.shape
    tk = TM_CMB
    per_seq = t // tk
    n_tiles = bsz * per_seq
    smem_tile = lambda imap: pl.BlockSpec((1, 1, TOP_K * tk), imap, memory_space=pltpu.SMEM)
    return pl.pallas_call(
        functools.partial(_combine_kernel, tk=tk),
        grid=(n_tiles,),
        in_specs=[
            smem_tile(lambda i: (i, 0, 0)),
            smem_tile(lambda i: (jnp.minimum(i + 1, n_tiles - 1), 0, 0)),
            pl.BlockSpec(memory_space=pl.ANY),
            pl.BlockSpec((1, tk, d), lambda i: (i // per_seq, i % per_seq, 0)),
            pl.BlockSpec((1, tk, ROUTE_LANES), lambda i: (i // per_seq, i % per_seq, 0)),
            pl.BlockSpec((1, 1, d), lambda i: (i // per_seq, 0, 0)),
            pl.BlockSpec((1, d), lambda i: (0, 0)),
        ],
        out_specs=pl.BlockSpec((1, tk, d), lambda i: (i // per_seq, i % per_seq, 0)),
        out_shape=jax.ShapeDtypeStruct((bsz, t, d), _F32),
        scratch_shapes=[pltpu.VMEM((2, TOP_K * tk, d // 2), jnp.uint32),
                        pltpu.SemaphoreType.DMA((2,))],
        compiler_params=pltpu.CompilerParams(
            dimension_semantics=("arbitrary",), vmem_limit_bytes=VMEM_LIMIT),
        name="combine",
    )(dest_tiles, dest_tiles, ys, x1, route, gate2, n_post)


def _segment_tables(counts, tm, n_tiles, n_steps, nch):
    n_experts = counts.shape[0]
    i32 = jnp.int32
    nt = (counts + tm - 1) // tm
    tend = jnp.cumsum(nt)
    tstart = tend - nt
    tile_idx = jnp.arange(n_tiles, dtype=i32)
    valid = tile_idx < tend[-1]
    nonempty = nt > 0
    seg_of_e = jnp.cumsum(nonempty.astype(i32)) - 1
    n_seg = seg_of_e[-1] + 1
    ids = jnp.arange(n_experts, dtype=i32)
    seg_expert = jnp.sum(
        jnp.where(nonempty[None, :] & (seg_of_e[None, :] == ids[:, None]), ids[None, :], 0), axis=1)
    owns = (tstart[None, :] <= tile_idx[:, None]) & (tile_idx[:, None] < tend[None, :])
    pick = lambda v: jnp.sum(jnp.where(owns, v[None, :], 0), axis=1)
    k = pick(seg_of_e)
    j = tile_idx - pick(tstart)
    ntt = jnp.maximum(pick(nt), 1)
    has_next = valid & (k + 1 < n_seg)
    glo = jnp.where(has_next, (k + 1) * nch + j * nch // ntt, 0)
    ghi = jnp.where(has_next, (k + 1) * nch + (j + 1) * nch // ntt, 0)
    expert_tables = (valid.astype(i32), (k % 2).astype(i32), glo.astype(i32), ghi.astype(i32),
                     seg_expert.astype(i32), (n_seg * nch).reshape(1).astype(i32),
                     (tend[-1:] - 1).astype(i32))
    pad_start = jnp.zeros((n_steps,), i32).at[:n_experts].set(tstart * tm + counts)
    pad_len = jnp.zeros((n_steps,), i32).at[:n_experts].set(nt * tm - counts)
    return expert_tables, pad_start, pad_len, tend[-1:].astype(i32)


def kernel(x, c, w_ada, b_ada, norm_mix_pre, norm_mix_post, w_in, conv_a, norm_a_out, conv_b, conv_b_bias, ln_b_gain, ln_b_bias, w_out, norm_moe_pre, norm_moe_post, w_router_group, b_router_group, w_router_expert, b_router_expert, w_expert_gate, w_expert_up, w_expert_down):
    bsz, t, d = x.shape
    depth = w_ada.shape[0]
    n_groups = w_router_group.shape[2]
    per_group = w_router_expert.shape[3]
    n_experts = n_groups * per_group
    n_tok = bsz * t
    n_assign = n_tok * TOP_K
    n_tiles = (n_assign + n_experts * (TM_EXP - 1) + TM_EXP - 1) // TM_EXP
    row = lambda v: v.reshape(1, -1)

    for l in range(depth):
        mod = _ada(c, w_ada[l], b_ada[l])
        shift1, scale1, gate1, shift2, scale2, gate2 = [
            m.reshape(bsz, 1, d) for m in jnp.split(mod, N_MOD, axis=-1)]

        y = _mix_in(x, shift1, scale1, row(norm_mix_pre[l]), w_in[l].astype(_BF16), conv_a[l],
                    row(norm_a_out[l]), conv_b[l], row(conv_b_bias[l]), row(ln_b_gain[l]),
                    row(ln_b_bias[l]))

        n_route = n_groups + n_experts
        w_fine = jnp.transpose(w_router_expert[l], (1, 0, 2)).reshape(d, n_experts)
        w_route = jnp.zeros((d, ROUTE_LANES), _F32).at[:, :n_route].set(
            jnp.concatenate([w_router_group[l], w_fine], axis=1)).astype(_BF16)
        b_route = jnp.zeros((1, ROUTE_LANES), _F32).at[0, :n_route].set(
            jnp.concatenate([b_router_group[l], b_router_expert[l].reshape(-1)]))
        x1, h2, route, route_t = _mix_out(y, x, gate1, shift2, scale2, row(norm_mix_post[l]),
                                          row(norm_moe_pre[l]), w_out[l].astype(_BF16), w_route,
                                          b_route, n_groups, per_group)

        dest, counts_f = _rank(route_t, TM_EXP)
        counts = counts_f[:n_experts, 0].astype(jnp.int32)
        n_ctiles = n_tok // TM_CMB
        assert n_ctiles >= n_experts
        expert_tables, pad_start, pad_len, used_tiles = _segment_tables(
            counts, TM_EXP, n_tiles, n_ctiles, W_CHUNKS)
        d0_tiles = dest[0].reshape(n_ctiles, 1, TM_CMB)
        d1_tiles = dest[1].reshape(n_ctiles, 1, TM_CMB)
        xs = _scatter(h2.reshape(n_tok, d // 2), d0_tiles, d1_tiles, pad_start, pad_len, used_tiles,
                      n_tiles * TM_EXP)
        ys = _experts(xs, expert_tables, w_expert_gate, w_expert_up, w_expert_down, l)

        dest_tiles = jnp.concatenate([d0_tiles, d1_tiles], axis=2)
        x = _combine(ys, dest_tiles, x1, route, gate2, row(norm_moe_post[l]))
    return x
```

```python
import functools

import jax
import jax.numpy as jnp
from jax import lax
from jax.experimental import pallas as pl
from jax.experimental.pallas import tpu as pltpu

EPS = 1e-6
HEAD_DIM = 128
KERNEL_A = 3
KERNEL_B = 31
N_MOD = 6
TOP_K = 2
ROUTE_LANES = 128
ROUTE_ROWS = 8
EXPERT_ROWS = 128
TQ_RANK = 1024
HIST_A = 8
HIST_B = 32
CONV_ROWS = 128
MIX_COLS = 256
W_CHUNKS = 8

TM_MIX = 512
TM_EXP = 256
TM_CMB = 256
ADA_COLS = 1024
VMEM_LIMIT = 56 * 1024 * 1024

_BF16 = jnp.bfloat16
_F32 = jnp.float32


def _dot(a, b):
    return jnp.dot(a, b, preferred_element_type=_F32)


def _rms(v):
    return lax.rsqrt(jnp.mean(v * v, axis=-1, keepdims=True) + EPS)


def _pack_bf16_pairs(v):
    n = v.shape[1] // 2
    bits = pltpu.bitcast(v.astype(_BF16).astype(_F32), jnp.uint32)
    return (bits[:, :n] >> 16) | bits[:, n:]


def _unpack_bf16_pairs(p):
    lo = pltpu.bitcast(p << 16, _F32)
    hi = pltpu.bitcast(p & jnp.uint32(0xFFFF0000), _F32)
    return lo, hi


def _ada_kernel(c_ref, w_ref, b_ref, o_ref):
    c = c_ref[...]
    act = (c * jax.nn.sigmoid(c)).astype(_BF16)
    o_ref[...] = _dot(act, w_ref[...].astype(_BF16)) + b_ref[...]


def _ada(c, w_ada, b_ada):
    bsz, d = c.shape
    n_out = w_ada.shape[1]
    c_pad = jnp.zeros((8, d), _F32).at[:bsz].set(c)
    out = pl.pallas_call(
        _ada_kernel,
        grid=(n_out // ADA_COLS,),
        in_specs=[
            pl.BlockSpec((8, d), lambda j: (0, 0)),
            pl.BlockSpec((d, ADA_COLS), lambda j: (0, j)),
            pl.BlockSpec((1, ADA_COLS), lambda j: (0, j)),
        ],
        out_specs=pl.BlockSpec((8, ADA_COLS), lambda j: (0, j)),
        out_shape=jax.ShapeDtypeStruct((8, n_out), _F32),
        compiler_params=pltpu.CompilerParams(
            dimension_semantics=("arbitrary",), vmem_limit_bytes=VMEM_LIMIT),
        name="ada",
    )(c_pad, w_ada, b_ada.reshape(1, n_out))
    return out[:bsz]


def _dwconv_chunk(ext_ref, w_ref, r0, rows, lo, hi, hist, ksize):
    span = rows + hist
    first = hist - (ksize - 1)
    xe = ext_ref[r0:r0 + span, lo:hi]
    acc = None
    for b in range(8):
        offsets = [o for o in range(first, hist + 1) if o % 8 == b]
        if not offsets:
            continue
        shifted = xe if b == 0 else pltpu.roll(xe, span - b, 0)
        for o in offsets:
            k = o - first
            term = w_ref[k:k + 1, lo:hi] * shifted[o - b:o - b + rows, :]
            acc = term if acc is None else acc + term
    return acc


def _mix_in_kernel(x_ref, sh_ref, gs_ref, w_ref, ca_ref, na_ref, cb_ref, cbb_ref,
                   lg_ref, lb_ref, y_ref, cv_ext, u_ext, ba_buf, conv_b_out, *, tm, da, db):
    nc = MIX_COLS
    @pl.when(pl.program_id(1) == 0)
    def _():
        cv_ext[...] = jnp.zeros(cv_ext.shape, _F32)
        u_ext[0:HIST_B, :] = jnp.zeros((HIST_B, db), _F32)
        ba_buf[...] = jnp.zeros(ba_buf.shape, _F32)
        conv_b_out[...] = jnp.zeros(conv_b_out.shape, _F32)

    x = x_ref[0]
    h = x * _rms(x) * gs_ref[0] + sh_ref[0]
    hb = h.astype(_BF16)

    def finish_head(hd):
        lo, hi = hd * HEAD_DIM, (hd + 1) * HEAD_DIM
        for r0 in range(0, tm, CONV_ROWS):
            conv = _dwconv_chunk(cv_ext, ca_ref, r0, CONV_ROWS, lo, hi, HIST_A, KERNEL_A)
            seg = ba_buf[r0:r0 + CONV_ROWS, lo:hi] * conv
            y_ref[0, r0:r0 + CONV_ROWS, lo:hi] = (seg * _rms(seg) * na_ref[:, lo:hi]).astype(_BF16)

    def finish_rows(r0, rows):
        acc = conv_b_out[r0:r0 + rows, :]
        mu = jnp.mean(acc, axis=-1, keepdims=True)
        cen = acc - mu
        var = jnp.mean(cen * cen, axis=-1, keepdims=True)
        yn = cen * lax.rsqrt(var + EPS) * lg_ref[...] + lb_ref[...]
        y_ref[0, r0:r0 + rows, da:da + db] = (yn * jax.nn.sigmoid(yn)).astype(_BF16)

    n_b = db // nc
    heads_per = (da // HEAD_DIM) // n_b
    ln_rows = tm // n_b
    for j in range(n_b):
        lo = j * nc
        val = _dot(hb, w_ref[:, 3 * da + lo:3 * da + lo + nc])
        gate = _dot(hb, w_ref[:, 3 * da + db + lo:3 * da + db + lo + nc])
        for hd in range(j * heads_per, (j + 1) * heads_per):
            finish_head(hd)
        finish_rows(j * ln_rows, ln_rows)
        u_ext[HIST_B:HIST_B + tm, lo:lo + nc] = val * jax.nn.sigmoid(gate)
    cv_ext[0:HIST_A, :] = cv_ext[tm:tm + HIST_A, :]

    conv_chunks = [(lo, r0) for lo in range(0, db, HEAD_DIM) for r0 in range(0, tm, CONV_ROWS)]
    n_a = da // nc
    n_mm = 3 * n_a
    done = 0
    for q in range(n_mm):
        j = q // 2 if q < 2 * n_a else q - 2 * n_a
        lo = j * nc
        if q < 2 * n_a and q % 2 == 0:
            c_part = _dot(hb, w_ref[:, da + lo:da + lo + nc])
        elif q < 2 * n_a:
            v_part = _dot(hb, w_ref[:, 2 * da + lo:2 * da + lo + nc])
            cv_ext[HIST_A:HIST_A + tm, lo:lo + nc] = c_part * v_part
        else:
            ba_buf[:, lo:lo + nc] = _dot(hb, w_ref[:, lo:lo + nc])
        upto = (q + 1) * len(conv_chunks) // n_mm
        for lo_c, r0 in conv_chunks[done:upto]:
            conv = _dwconv_chunk(u_ext, cb_ref, r0, CONV_ROWS, lo_c, lo_c + HEAD_DIM, HIST_B, KERNEL_B)
            conv_b_out[r0:r0 + CONV_ROWS, lo_c:lo_c + HEAD_DIM] = conv + cbb_ref[:, lo_c:lo_c + HEAD_DIM]
        done = upto
    u_ext[0:HIST_B, :] = u_ext[tm:tm + HIST_B, :]


def _mix_in(x, shift, gain_scale, w_in_bf, conv_a, norm_a, conv_b, conv_b_bias, ln_g, ln_b):
    bsz, t, d = x.shape
    da = conv_a.shape[1]
    db = conv_b.shape[1]
    tm = TM_MIX
    full = lambda shape: pl.BlockSpec(shape, lambda b, s: (0,) * len(shape))
    per_b = pl.BlockSpec((1, 1, d), lambda b, s: (b, 0, 0))
    n_t = t // tm
    return pl.pallas_call(
        functools.partial(_mix_in_kernel, tm=tm, da=da, db=db),
        grid=(bsz, n_t + 1),
        in_specs=[
            pl.BlockSpec((1, tm, d), lambda b, s: (b, jnp.minimum(s, n_t - 1), 0)),
            per_b, per_b,
            pl.BlockSpec(w_in_bf.shape, lambda b, s: (0, 0), pipeline_mode=pl.Buffered(1)),
            full((KERNEL_A, da)), full((1, da)),
            full((KERNEL_B, db)), full((1, db)), full((1, db)), full((1, db)),
        ],
        out_specs=pl.BlockSpec((1, tm, da + db), lambda b, s: (b, jnp.maximum(s - 1, 0), 0)),
        out_shape=jax.ShapeDtypeStruct((bsz, t, da + db), _BF16),
        scratch_shapes=[pltpu.VMEM((HIST_A + tm, da), _F32), pltpu.VMEM((HIST_B + tm, db), _F32),
                        pltpu.VMEM((tm, da), _F32), pltpu.VMEM((tm, db), _F32)],
        compiler_params=pltpu.CompilerParams(
            dimension_semantics=("arbitrary", "arbitrary"), vmem_limit_bytes=VMEM_LIMIT),
        name="mix_in",
    )(x, shift, gain_scale, w_in_bf, conv_a, norm_a, conv_b, conv_b_bias, ln_g, ln_b)


def _route_tile(hb, wr, br, n_groups, per_group):
    logits = _dot(hb, wr) + br
    lane = lax.broadcasted_iota(jnp.int32, logits.shape, 1).astype(_F32)
    neg = jnp.float32(-jnp.inf)
    big = jnp.float32(ROUTE_LANES)
    is_coarse = lane < n_groups
    coarse = jnp.where(is_coarse, logits, neg)
    m = jnp.max(coarse, axis=-1, keepdims=True)
    g_sel = jnp.min(jnp.where(coarse == m, lane, big), axis=-1, keepdims=True)
    p_g = 1.0 / jnp.sum(jnp.where(is_coarse, jnp.exp(logits - m), 0.0), axis=-1, keepdims=True)
    lo = n_groups + g_sel * per_group
    fine = jnp.where((lane >= lo) & (lane < lo + per_group), logits, neg)
    v1 = jnp.max(fine, axis=-1, keepdims=True)
    l1 = jnp.min(jnp.where(fine == v1, lane, big), axis=-1, keepdims=True)
    fine2 = jnp.where(lane == l1, neg, fine)
    v2 = jnp.max(fine2, axis=-1, keepdims=True)
    l2 = jnp.min(jnp.where(fine2 == v2, lane, big), axis=-1, keepdims=True)
    e2 = jnp.exp(v2 - v1)
    w1 = p_g / (1.0 + e2)
    w2 = p_g * e2 / (1.0 + e2)
    route = jnp.where(lane == 0, l1 - n_groups,
                      jnp.where(lane == 1, l2 - n_groups,
                                jnp.where(lane == 2, w1, jnp.where(lane == 3, w2, 0.0))))
    return route


def _mix_out_kernel(y_ref, x_ref, gp_ref, sh2_ref, gs_ref, wout_ref, wr_ref, br_ref,
                    x1_ref, h2_ref, route_ref, route_t_ref, *, n_groups, per_group):
    o = _dot(y_ref[0], wout_ref[...])
    x1 = x_ref[0] + o * _rms(o) * gp_ref[0]
    x1_ref[0] = x1
    h2 = x1 * _rms(x1) * gs_ref[0] + sh2_ref[0]
    h2_ref[0] = _pack_bf16_pairs(h2)
    route = _route_tile(h2.astype(_BF16), wr_ref[...], br_ref[...], n_groups, per_group)
    route_ref[0] = route
    route_t_ref[...] = jnp.transpose(route)[0:ROUTE_ROWS, :]


def _mix_out(y, x, gate_post, shift2, gain_scale, w_out_bf, w_route, b_route, n_groups, per_group):
    bsz, t, d = x.shape
    tm = TM_MIX
    full = lambda shape: pl.BlockSpec(shape, lambda b, s: (0,) * len(shape))
    per_b = pl.BlockSpec((1, 1, d), lambda b, s: (b, 0, 0))
    tile = pl.BlockSpec((1, tm, d), lambda b, s: (b, s, 0))
    return pl.pallas_call(
        functools.partial(_mix_out_kernel, n_groups=n_groups, per_group=per_group),
        grid=(bsz, t // tm),
        in_specs=[
            tile, tile, per_b, per_b, per_b,
            pl.BlockSpec(w_out_bf.shape, lambda b, s: (0, 0), pipeline_mode=pl.Buffered(1)),
            full((d, ROUTE_LANES)), full((1, ROUTE_LANES)),
        ],
        out_specs=[tile, pl.BlockSpec((1, tm, d // 2), lambda b, s: (b, s, 0)),
                   pl.BlockSpec((1, tm, ROUTE_LANES), lambda b, s: (b, s, 0)),
                   pl.BlockSpec((ROUTE_ROWS, tm), lambda b, s: (0, b * (t // tm) + s))],
        out_shape=[
            jax.ShapeDtypeStruct((bsz, t, d), _F32),
            jax.ShapeDtypeStruct((bsz, t, d // 2), jnp.uint32),
            jax.ShapeDtypeStruct((bsz, t, ROUTE_LANES), _F32),
            jax.ShapeDtypeStruct((ROUTE_ROWS, bsz * t), _F32),
        ],
        compiler_params=pltpu.CompilerParams(
            dimension_semantics=("arbitrary", "arbitrary"), vmem_limit_bytes=VMEM_LIMIT),
        name="mix_out",
    )(y, x, gate_post, shift2, gain_scale, w_out_bf, w_route, b_route)


def _rank_kernel(rt_ref, u_ref, dest_ref, cnt_ref, carry, pbase, *, tq, tm):
    phase = pl.program_id(0)
    j = pl.program_id(1)
    sub = lax.broadcasted_iota(jnp.int32, (EXPERT_ROWS, tq), 0).astype(_F32)
    oh0 = sub == rt_ref[0:1, :]
    oh1 = sub == rt_ref[1:2, :]
    both = jnp.where(oh0, 1.0, 0.0) + jnp.where(oh1, 1.0, 0.0)
    tile_counts = jnp.sum(both, axis=1, keepdims=True)

    @pl.when(jnp.logical_and(phase == 0, j == 0))
    def _():
        carry[...] = jnp.zeros(carry.shape, _F32)

    @pl.when(jnp.logical_and(phase == 1, j == 0))
    def _():
        cnt = carry[...]
        cnt_ref[...] = cnt
        tiles = jnp.floor((cnt + (tm - 1)) * (1.0 / tm))
        hi = jnp.floor(tiles * (1.0 / 16.0))
        lo = tiles - 16.0 * hi
        r = lax.broadcasted_iota(jnp.int32, (EXPERT_ROWS, EXPERT_ROWS), 0)
        c = lax.broadcasted_iota(jnp.int32, (EXPERT_ROWS, EXPERT_ROWS), 1)
        before = jnp.where(c < r, 1.0, 0.0).astype(_BF16)
        start_tiles = 16.0 * _dot(before, hi.astype(_BF16)) + _dot(before, lo.astype(_BF16))
        pbase[...] = start_tiles * tm
        carry[...] = jnp.zeros(carry.shape, _F32)

    @pl.when(phase == 1)
    def _():
        earlier = _dot(both.astype(_BF16), u_ref[...])
        slot = earlier + carry[:, 0:1] + pbase[:, 0:1]
        d0 = jnp.sum(jnp.where(oh0, slot, 0.0), axis=0, keepdims=True)
        d1 = jnp.sum(jnp.where(oh1, slot, 0.0), axis=0, keepdims=True)
        row = lax.broadcasted_iota(jnp.int32, (ROUTE_ROWS, tq), 0)
        dest_ref[...] = jnp.where(row == 0, d0, jnp.where(row == 1, d1, 0.0)).astype(jnp.int32)

    carry[...] = carry[...] + tile_counts


def _rank(route_t, tm):
    n_tok = route_t.shape[1]
    tq = TQ_RANK
    assert tm & (tm - 1) == 0, "segment padding must be a power of two for exact f32 arithmetic"
    idx = jnp.arange(tq, dtype=jnp.int32)
    earlier_mask = (idx[:, None] < idx[None, :]).astype(_BF16)
    return pl.pallas_call(
        functools.partial(_rank_kernel, tq=tq, tm=tm),
        grid=(2, n_tok // tq),
        in_specs=[
            pl.BlockSpec((ROUTE_ROWS, tq), lambda p, j: (0, j)),
            pl.BlockSpec((tq, tq), lambda p, j: (0, 0)),
        ],
        out_specs=[pl.BlockSpec((ROUTE_ROWS, tq), lambda p, j: (0, j * p)),
                   pl.BlockSpec((EXPERT_ROWS, ROUTE_LANES), lambda p, j: (0, 0))],
        out_shape=[jax.ShapeDtypeStruct((ROUTE_ROWS, n_tok), jnp.int32),
                   jax.ShapeDtypeStruct((EXPERT_ROWS, ROUTE_LANES), _F32)],
        scratch_shapes=[pltpu.VMEM((EXPERT_ROWS, ROUTE_LANES), _F32),
                        pltpu.VMEM((EXPERT_ROWS, ROUTE_LANES), _F32)],
        compiler_params=pltpu.CompilerParams(
            dimension_semantics=("arbitrary", "arbitrary"), vmem_limit_bytes=VMEM_LIMIT),
        name="rank",
    )(route_t, earlier_mask)


def _row_copy(src, dst, sem):
    return pltpu.make_async_copy(src, dst, sem)


def _scatter_kernel(ps_ref, pn_ref, used_ref, d0_ref, d1_ref, h_ref, xs_hbm, zrow, sem, psem, *, tk,
                    tm):
    i = pl.program_id(0)

    @pl.when(i == 0)
    def _():
        zrow[...] = jnp.zeros(zrow.shape, zrow.dtype)

    @pl.when(i == pl.num_programs(0) - 1)
    def _():
        n_rows = xs_hbm.shape[0]

        def fill(t, carry):
            dst = xs_hbm.at[pl.ds(pl.multiple_of(t * tm, tm), tm)]
            _row_copy(zrow, dst, psem).start()
            _row_copy(zrow, dst, psem).wait()
            return carry

        lax.fori_loop(used_ref[0], n_rows // tm, fill, 0)

    for r in range(tk):
        src = h_ref.at[pl.ds(r, 1)]
        _row_copy(src, xs_hbm.at[pl.ds(d0_ref[0, 0, r], 1)], sem).start(priority=0)
        _row_copy(src, xs_hbm.at[pl.ds(d1_ref[0, 0, r], 1)], sem).start(priority=1)

    pad_start = ps_ref[i]
    n_pad = pn_ref[i]
    zsrc = zrow.at[pl.ds(0, 1)]

    def start_pad(r, carry):
        _row_copy(zsrc, xs_hbm.at[pl.ds(pad_start + r, 1)], psem).start()
        return carry

    def wait_pad(r, carry):
        _row_copy(zsrc, xs_hbm.at[pl.ds(0, 1)], psem).wait()
        return carry

    lax.fori_loop(0, n_pad, start_pad, 0)
    lax.fori_loop(0, n_pad, wait_pad, 0)
    for _ in range(TOP_K):
        _row_copy(h_ref, xs_hbm.at[pl.ds(0, tk)], sem).wait()


def _scatter(h2_flat, d0_tiles, d1_tiles, pad_start, pad_len, used_tiles, n_rows):
    n_tok, d = h2_flat.shape
    tk = TM_CMB
    tm = TM_EXP
    smem_tile = pl.BlockSpec((1, 1, tk), lambda i, ps, pn, used: (i, 0, 0), memory_space=pltpu.SMEM)
    return pl.pallas_call(
        functools.partial(_scatter_kernel, tk=tk, tm=tm),
        grid_spec=pltpu.PrefetchScalarGridSpec(
            num_scalar_prefetch=3,
            grid=(n_tok // tk,),
            in_specs=[smem_tile, smem_tile, pl.BlockSpec((tk, d), lambda i, ps, pn, used: (i, 0))],
            out_specs=pl.BlockSpec(memory_space=pl.ANY),
            scratch_shapes=[pltpu.VMEM((tm, d), h2_flat.dtype), pltpu.SemaphoreType.DMA(()),
                            pltpu.SemaphoreType.DMA(())],
        ),
        out_shape=jax.ShapeDtypeStruct((n_rows, d), h2_flat.dtype),
        compiler_params=pltpu.CompilerParams(
            dimension_semantics=("arbitrary",), vmem_limit_bytes=VMEM_LIMIT),
        name="scatter",
    )(pad_start, pad_len, used_tiles, d0_tiles, d1_tiles, h2_flat)


def _expert_kernel(tv_ref, par_ref, glo_ref, ghi_ref, sege_ref, ntot_ref, last_ref,
                   xs_ref, wg_hbm, wu_hbm, wd_hbm, ys_ref,
                   wgb, wub, wdb, sg, su, sd, sem, *, layer, nch):
    del last_ref
    i = pl.program_id(0)
    rg = wgb.shape[1] // nch
    rd = wdb.shape[1] // nch

    def chunk_copies(g):
        e = sege_ref[g // nch]
        c = g % nch
        s = g % 2
        rows_g = pl.ds(pl.multiple_of(c * rg, rg), rg)
        rows_d = pl.ds(pl.multiple_of(c * rd, rd), rd)
        return (_row_copy(wg_hbm.at[layer, e, rows_g], sg.at[s], sem.at[s]),
                _row_copy(wu_hbm.at[layer, e, rows_g], su.at[s], sem.at[s]),
                _row_copy(wd_hbm.at[layer, e, rows_d], sd.at[s], sem.at[s]))

    def start_chunk(g):
        for cp in chunk_copies(g):
            cp.start()

    def process_chunk(g, carry):
        for cp in chunk_copies(g):
            cp.wait()
        c = g % nch
        s = g % 2
        p = (g // nch) % 2
        wgb[p, pl.ds(pl.multiple_of(c * rg, rg), rg), :] = sg[s].astype(_BF16)
        wub[p, pl.ds(pl.multiple_of(c * rg, rg), rg), :] = su[s].astype(_BF16)
        wdb[p, pl.ds(pl.multiple_of(c * rd, rd), rd), :] = sd[s].astype(_BF16)

        @pl.when(g + 2 < ntot_ref[0])
        def _():
            start_chunk(g + 2)
        return carry

    @pl.when(i == 0)
    def _():
        start_chunk(0)
        start_chunk(1)
        lax.fori_loop(0, nch, process_chunk, 0)

    valid = tv_ref[i] == 1

    @pl.when(valid)
    def _():
        lax.fori_loop(glo_ref[i], ghi_ref[i], process_chunk, 0)
        p = par_ref[i]
        lo, hi = _unpack_bf16_pairs(xs_ref[...])
        xb = jnp.concatenate([lo.astype(_BF16), hi.astype(_BF16)], axis=1)
        g = _dot(xb, wgb[p])
        u = _dot(xb, wub[p])
        hid = (g * jax.nn.sigmoid(g) * u).astype(_BF16)
        ys_ref[...] = _pack_bf16_pairs(_dot(hid, wdb[p]))

    @pl.when(jnp.logical_not(valid))
    def _():
        ys_ref[...] = jnp.zeros(ys_ref.shape, ys_ref.dtype)


def _experts(xs, tables, w_gate, w_up, w_down, layer):
    n_rows, dp = xs.shape
    d = w_gate.shape[2]
    tm = TM_EXP
    f = w_gate.shape[3]
    nch = W_CHUNKS
    any_spec = pl.BlockSpec(memory_space=pl.ANY)
    return pl.pallas_call(
        functools.partial(_expert_kernel, layer=layer, nch=nch),
        grid_spec=pltpu.PrefetchScalarGridSpec(
            num_scalar_prefetch=7,
            grid=(n_rows // tm,),
            in_specs=[
                pl.BlockSpec((tm, dp), lambda i, *s: (jnp.minimum(i, s[6][0]), 0)),
                any_spec, any_spec, any_spec,
            ],
            out_specs=pl.BlockSpec((tm, dp), lambda i, *s: (i, 0)),
            scratch_shapes=[
                pltpu.VMEM((2, d, f), _BF16), pltpu.VMEM((2, d, f), _BF16), pltpu.VMEM((2, f, d), _BF16),
                pltpu.VMEM((2, d // nch, f), _F32), pltpu.VMEM((2, d // nch, f), _F32),
                pltpu.VMEM((2, f // nch, d), _F32), pltpu.SemaphoreType.DMA((2,)),
            ],
        ),
        out_shape=jax.ShapeDtypeStruct((n_rows, dp), jnp.uint32),
        compiler_params=pltpu.CompilerParams(
            dimension_semantics=("arbitrary",), vmem_limit_bytes=VMEM_LIMIT),
        name="experts",
    )(*tables, xs, w_gate, w_up, w_down)


def _start_row_gather(src_hbm, idx_ref, dst, sem, n_rows):
    for r in range(n_rows):
        _row_copy(src_hbm.at[pl.ds(idx_ref[0, 0, r], 1)], dst.at[pl.ds(r, 1)], sem).start(
            priority=r % 2)


def _wait_row_gather(src_hbm, dst, sem, n_rows):
    _row_copy(src_hbm.at[pl.ds(0, n_rows)], dst, sem).wait()


def _combine_kernel(d_cur, d_nxt, ys_hbm, x1_ref, route_ref, gp_ref, o_ref, rbuf, sem, *,
                    tk):
    i = pl.program_id(0)
    n = pl.num_programs(0)
    slot = i % 2

    @pl.when(i == 0)
    def _():
        _start_row_gather(ys_hbm, d_cur, rbuf.at[0], sem.at[0], TOP_K * tk)

    _wait_row_gather(ys_hbm, rbuf.at[slot], sem.at[slot], TOP_K * tk)

    @pl.when(i + 1 < n)
    def _():
        _start_row_gather(ys_hbm, d_nxt, rbuf.at[1 - slot], sem.at[1 - slot], TOP_K * tk)

    route = route_ref[0]
    lo0, hi0 = _unpack_bf16_pairs(rbuf[slot, 0:tk, :])
    lo1, hi1 = _unpack_bf16_pairs(rbuf[slot, tk:2 * tk, :])
    w0 = route[:, 2:3]
    w1 = route[:, 3:4]
    y = jnp.concatenate([w0 * lo0 + w1 * lo1, w0 * hi0 + w1 * hi1], axis=1)
    o_ref[0] = x1_ref[0] + y * _rms(y) * gp_ref[0]


def _combine(ys, dest_tiles, x1, route, gate_post):
    bsz, t, d = x1.shape
    tk = TM_CMB
    per_seq = t // tk
    n_tiles = bsz * per_seq
    smem_tile = lambda imap: pl.BlockSpec((1, 1, TOP_K * tk), imap, memory_space=pltpu.SMEM)
    return pl.pallas_call(
        functools.partial(_combine_kernel, tk=tk),
        grid=(n_tiles,),
        in_specs=[
            smem_tile(lambda i: (i, 0, 0)),
            smem_tile(lambda i: (jnp.minimum(i + 1, n_tiles - 1), 0, 0)),
            pl.BlockSpec(memory_space=pl.ANY),
            pl.BlockSpec((1, tk, d), lambda i: (i // per_seq, i % per_seq, 0)),
            pl.BlockSpec((1, tk, ROUTE_LANES), lambda i: (i // per_seq, i % per_seq, 0)),
            pl.BlockSpec((1, 1, d), lambda i: (i // per_seq, 0, 0)),
        ],
        out_specs=pl.BlockSpec((1, tk, d), lambda i: (i // per_seq, i % per_seq, 0)),
        out_shape=jax.ShapeDtypeStruct((bsz, t, d), _F32),
        scratch_shapes=[pltpu.VMEM((2, TOP_K * tk, d // 2), jnp.uint32),
                        pltpu.SemaphoreType.DMA((2,))],
        compiler_params=pltpu.CompilerParams(
            dimension_semantics=("arbitrary",), vmem_limit_bytes=VMEM_LIMIT),
        name="combine",
    )(dest_tiles, dest_tiles, ys, x1, route, gate_post)


def _segment_tables(counts, tm, n_tiles, n_steps, nch):
    n_experts = counts.shape[0]
    i32 = jnp.int32
    nt = (counts + tm - 1) // tm
    tend = jnp.cumsum(nt)
    tstart = tend - nt
    tile_idx = jnp.arange(n_tiles, dtype=i32)
    valid = tile_idx < tend[-1]
    nonempty = nt > 0
    seg_of_e = jnp.cumsum(nonempty.astype(i32)) - 1
    n_seg = seg_of_e[-1] + 1
    ids = jnp.arange(n_experts, dtype=i32)
    seg_expert = jnp.sum(
        jnp.where(nonempty[None, :] & (seg_of_e[None, :] == ids[:, None]), ids[None, :], 0), axis=1)
    owns = (tstart[None, :] <= tile_idx[:, None]) & (tile_idx[:, None] < tend[None, :])
    pick = lambda v: jnp.sum(jnp.where(owns, v[None, :], 0), axis=1)
    k = pick(seg_of_e)
    j = tile_idx - pick(tstart)
    ntt = jnp.maximum(pick(nt), 1)
    has_next = valid & (k + 1 < n_seg)
    glo = jnp.where(has_next, (k + 1) * nch + j * nch // ntt, 0)
    ghi = jnp.where(has_next, (k + 1) * nch + (j + 1) * nch // ntt, 0)
    expert_tables = (valid.astype(i32), (k % 2).astype(i32), glo.astype(i32), ghi.astype(i32),
                     seg_expert.astype(i32), (n_seg * nch).reshape(1).astype(i32),
                     (tend[-1:] - 1).astype(i32))
    pad_start = jnp.zeros((n_steps,), i32).at[:n_experts].set(tstart * tm + counts)
    pad_len = jnp.zeros((n_steps,), i32).at[:n_experts].set(nt * tm - counts)
    return expert_tables, pad_start, pad_len, tend[-1:].astype(i32)


def kernel(x, c, w_ada, b_ada, norm_mix_pre, norm_mix_post, w_in, conv_a, norm_a_out, conv_b, conv_b_bias, ln_b_gain, ln_b_bias, w_out, norm_moe_pre, norm_moe_post, w_router_group, b_router_group, w_router_expert, b_router_expert, w_expert_gate, w_expert_up, w_expert_down):
    bsz, t, d = x.shape
    depth = w_ada.shape[0]
    n_groups = w_router_group.shape[2]
    per_group = w_router_expert.shape[3]
    n_experts = n_groups * per_group
    n_tok = bsz * t
    n_assign = n_tok * TOP_K
    n_tiles = (n_assign + n_experts * (TM_EXP - 1) + TM_EXP - 1) // TM_EXP
    row = lambda v: v.reshape(1, -1)

    for l in range(depth):
        mod = _ada(c, w_ada[l], b_ada[l])
        shift1, scale1, gate1, shift2, scale2, gate2 = [
            m.reshape(bsz, 1, d) for m in jnp.split(mod, N_MOD, axis=-1)]

        y = _mix_in(x, shift1, norm_mix_pre[l] * (1.0 + scale1), w_in[l].astype(_BF16), conv_a[l],
                    row(norm_a_out[l]), conv_b[l], row(conv_b_bias[l]), row(ln_b_gain[l]),
                    row(ln_b_bias[l]))

        n_route = n_groups + n_experts
        w_fine = jnp.transpose(w_router_expert[l], (1, 0, 2)).reshape(d, n_experts)
        w_route = jnp.zeros((d, ROUTE_LANES), _F32).at[:, :n_route].set(
            jnp.concatenate([w_router_group[l], w_fine], axis=1)).astype(_BF16)
        b_route = jnp.zeros((1, ROUTE_LANES), _F32).at[0, :n_route].set(
            jnp.concatenate([b_router_group[l], b_router_expert[l].reshape(-1)]))
        x1, h2, route, route_t = _mix_out(
            y, x, gate1 * norm_mix_post[l], shift2, norm_moe_pre[l] * (1.0 + scale2),
            w_out[l].astype(_BF16), w_route, b_route, n_groups, per_group)

        dest, counts_f = _rank(route_t, TM_EXP)
        counts = counts_f[:n_experts, 0].astype(jnp.int32)
        n_ctiles = n_tok // TM_CMB
        assert n_ctiles >= n_experts
        expert_tables, pad_start, pad_len, used_tiles = _segment_tables(
            counts, TM_EXP, n_tiles, n_ctiles, W_CHUNKS)
        d0_tiles = dest[0].reshape(n_ctiles, 1, TM_CMB)
        d1_tiles = dest[1].reshape(n_ctiles, 1, TM_CMB)
        xs = _scatter(h2.reshape(n_tok, d // 2), d0_tiles, d1_tiles, pad_start, pad_len, used_tiles,
                      n_tiles * TM_EXP)
        ys = _experts(xs, expert_tables, w_expert_gate, w_expert_up, w_expert_down, l)

        dest_tiles = jnp.concatenate([d0_tiles, d1_tiles], axis=2)
        x = _combine(ys, dest_tiles, x1, route, gate2 * norm_moe_post[l])
    return x
```

```python
import functools

import jax
import jax.numpy as jnp
from jax import lax
from jax.experimental import pallas as pl
from jax.experimental.pallas import tpu as pltpu

EPS = 1e-6
HEAD_DIM = 128
KERNEL_A = 3
KERNEL_B = 31
N_MOD = 6
TOP_K = 2
ROUTE_LANES = 128
ROUTE_ROWS = 8
EXPERT_ROWS = 128
TQ_RANK = 1024
HIST_A = 8
HIST_B = 32
CONV_ROWS = 128
MIX_COLS = 256
W_CHUNKS = 8

TM_MIX = 512
TM_EXP = 256
TM_CMB = 256
ADA_COLS = 1024
VMEM_LIMIT = 56 * 1024 * 1024

_BF16 = jnp.bfloat16
_F32 = jnp.float32


def _dot(a, b):
    return jnp.dot(a, b, preferred_element_type=_F32)


def _rms(v):
    return lax.rsqrt(jnp.mean(v * v, axis=-1, keepdims=True) + EPS)


def _pack_bf16_pairs(v):
    n = v.shape[1] // 2
    bits = pltpu.bitcast(v.astype(_BF16).astype(_F32), jnp.uint32)
    return (bits[:, :n] >> 16) | bits[:, n:]


def _unpack_bf16_pairs(p):
    lo = pltpu.bitcast(p << 16, _F32)
    hi = pltpu.bitcast(p & jnp.uint32(0xFFFF0000), _F32)
    return lo, hi


def _ada_kernel(c_ref, w_ref, b_ref, o_ref):
    c = c_ref[...]
    act = (c * jax.nn.sigmoid(c)).astype(_BF16)
    o_ref[...] = _dot(act, w_ref[...].astype(_BF16)) + b_ref[...]


def _ada(c, w_ada, b_ada):
    bsz, d = c.shape
    n_out = w_ada.shape[1]
    c_pad = jnp.zeros((8, d), _F32).at[:bsz].set(c)
    out = pl.pallas_call(
        _ada_kernel,
        grid=(n_out // ADA_COLS,),
        in_specs=[
            pl.BlockSpec((8, d), lambda j: (0, 0)),
            pl.BlockSpec((d, ADA_COLS), lambda j: (0, j)),
            pl.BlockSpec((1, ADA_COLS), lambda j: (0, j)),
        ],
        out_specs=pl.BlockSpec((8, ADA_COLS), lambda j: (0, j)),
        out_shape=jax.ShapeDtypeStruct((8, n_out), _F32),
        compiler_params=pltpu.CompilerParams(
            dimension_semantics=("arbitrary",), vmem_limit_bytes=VMEM_LIMIT),
        name="ada",
    )(c_pad, w_ada, b_ada.reshape(1, n_out))
    return out[:bsz]


def _dwconv_chunk(ext_ref, w_ref, r0, rows, lo, hi, hist, ksize):
    span = rows + hist
    first = hist - (ksize - 1)
    xe = ext_ref[r0:r0 + span, lo:hi]
    acc = None
    for b in range(8):
        offsets = [o for o in range(first, hist + 1) if o % 8 == b]
        if not offsets:
            continue
        shifted = xe if b == 0 else pltpu.roll(xe, span - b, 0)
        for o in offsets:
            k = o - first
            term = w_ref[k:k + 1, lo:hi] * shifted[o - b:o - b + rows, :]
            acc = term if acc is None else acc + term
    return acc


def _mix_in_kernel(x_ref, sh_ref, gs_ref, w_ref, ca_ref, na_ref, cb_ref, cbb_ref,
                   lg_ref, lb_ref, y_ref, cv_ext, u_ext, ba_buf, conv_b_out, *, tm, da, db):
    nc = MIX_COLS
    @pl.when(pl.program_id(1) == 0)
    def _():
        cv_ext[...] = jnp.zeros(cv_ext.shape, _F32)
        u_ext[0:HIST_B, :] = jnp.zeros((HIST_B, db), _F32)
        ba_buf[...] = jnp.zeros(ba_buf.shape, _F32)
        conv_b_out[...] = jnp.zeros(conv_b_out.shape, _F32)

    x = x_ref[0]
    h = x * _rms(x) * gs_ref[0] + sh_ref[0]
    hb = h.astype(_BF16)

    def finish_head(hd):
        lo, hi = hd * HEAD_DIM, (hd + 1) * HEAD_DIM
        for r0 in range(0, tm, CONV_ROWS):
            conv = _dwconv_chunk(cv_ext, ca_ref, r0, CONV_ROWS, lo, hi, HIST_A, KERNEL_A)
            seg = ba_buf[r0:r0 + CONV_ROWS, lo:hi] * conv
            y_ref[0, r0:r0 + CONV_ROWS, lo:hi] = (seg * _rms(seg) * na_ref[:, lo:hi]).astype(_BF16)

    def finish_rows(r0, rows):
        acc = conv_b_out[r0:r0 + rows, :]
        mu = jnp.mean(acc, axis=-1, keepdims=True)
        cen = acc - mu
        var = jnp.mean(cen * cen, axis=-1, keepdims=True)
        yn = cen * lax.rsqrt(var + EPS) * lg_ref[...] + lb_ref[...]
        y_ref[0, r0:r0 + rows, da:da + db] = (yn * jax.nn.sigmoid(yn)).astype(_BF16)

    n_b = db // nc
    heads_per = (da // HEAD_DIM) // n_b
    ln_rows = tm // n_b
    for j in range(n_b):
        lo = j * nc
        val = _dot(hb, w_ref[:, 3 * da + lo:3 * da + lo + nc])
        gate = _dot(hb, w_ref[:, 3 * da + db + lo:3 * da + db + lo + nc])
        for hd in range(j * heads_per, (j + 1) * heads_per):
            finish_head(hd)
        finish_rows(j * ln_rows, ln_rows)
        u_ext[HIST_B:HIST_B + tm, lo:lo + nc] = val * jax.nn.sigmoid(gate)
    cv_ext[0:HIST_A, :] = cv_ext[tm:tm + HIST_A, :]

    conv_chunks = [(lo, r0) for lo in range(0, db, HEAD_DIM) for r0 in range(0, tm, CONV_ROWS)]
    n_a = da // nc
    n_mm = 3 * n_a
    done = 0
    for q in range(n_mm):
        j = q // 2 if q < 2 * n_a else q - 2 * n_a
        lo = j * nc
        if q < 2 * n_a and q % 2 == 0:
            c_part = _dot(hb, w_ref[:, da + lo:da + lo + nc])
        elif q < 2 * n_a:
            v_part = _dot(hb, w_ref[:, 2 * da + lo:2 * da + lo + nc])
            cv_ext[HIST_A:HIST_A + tm, lo:lo + nc] = c_part * v_part
        else:
            ba_buf[:, lo:lo + nc] = _dot(hb, w_ref[:, lo:lo + nc])
        upto = (q + 1) * len(conv_chunks) // n_mm
        for lo_c, r0 in conv_chunks[done:upto]:
            conv = _dwconv_chunk(u_ext, cb_ref, r0, CONV_ROWS, lo_c, lo_c + HEAD_DIM, HIST_B, KERNEL_B)
            conv_b_out[r0:r0 + CONV_ROWS, lo_c:lo_c + HEAD_DIM] = conv + cbb_ref[:, lo_c:lo_c + HEAD_DIM]
        done = upto
    u_ext[0:HIST_B, :] = u_ext[tm:tm + HIST_B, :]


def _mix_in(x, shift, gain_scale, w_in_bf, conv_a, norm_a, conv_b, conv_b_bias, ln_g, ln_b):
    bsz, t, d = x.shape
    da = conv_a.shape[1]
    db = conv_b.shape[1]
    tm = TM_MIX
    full = lambda shape: pl.BlockSpec(shape, lambda b, s: (0,) * len(shape))
    per_b = pl.BlockSpec((1, 1, d), lambda b, s: (b, 0, 0))
    n_t = t // tm
    return pl.pallas_call(
        functools.partial(_mix_in_kernel, tm=tm, da=da, db=db),
        grid=(bsz, n_t + 1),
        in_specs=[
            pl.BlockSpec((1, tm, d), lambda b, s: (b, jnp.minimum(s, n_t - 1), 0)),
            per_b, per_b,
            pl.BlockSpec(w_in_bf.shape, lambda b, s: (0, 0), pipeline_mode=pl.Buffered(1)),
            full((KERNEL_A, da)), full((1, da)),
            full((KERNEL_B, db)), full((1, db)), full((1, db)), full((1, db)),
        ],
        out_specs=pl.BlockSpec((1, tm, da + db), lambda b, s: (b, jnp.maximum(s - 1, 0), 0)),
        out_shape=jax.ShapeDtypeStruct((bsz, t, da + db), _BF16),
        scratch_shapes=[pltpu.VMEM((HIST_A + tm, da), _F32), pltpu.VMEM((HIST_B + tm, db), _F32),
                        pltpu.VMEM((tm, da), _F32), pltpu.VMEM((tm, db), _F32)],
        compiler_params=pltpu.CompilerParams(
            dimension_semantics=("arbitrary", "arbitrary"), vmem_limit_bytes=VMEM_LIMIT),
        name="mix_in",
    )(x, shift, gain_scale, w_in_bf, conv_a, norm_a, conv_b, conv_b_bias, ln_g, ln_b)


def _route_tile(hb, wr, br, n_groups, per_group):
    logits = _dot(hb, wr) + br
    lane = lax.broadcasted_iota(jnp.int32, logits.shape, 1).astype(_F32)
    neg = jnp.float32(-jnp.inf)
    big = jnp.float32(ROUTE_LANES)
    is_coarse = lane < n_groups
    coarse = jnp.where(is_coarse, logits, neg)
    m = jnp.max(coarse, axis=-1, keepdims=True)
    g_sel = jnp.min(jnp.where(coarse == m, lane, big), axis=-1, keepdims=True)
    p_g = 1.0 / jnp.sum(jnp.where(is_coarse, jnp.exp(logits - m), 0.0), axis=-1, keepdims=True)
    lo = n_groups + g_sel * per_group
    fine = jnp.where((lane >= lo) & (lane < lo + per_group), logits, neg)
    v1 = jnp.max(fine, axis=-1, keepdims=True)
    l1 = jnp.min(jnp.where(fine == v1, lane, big), axis=-1, keepdims=True)
    fine2 = jnp.where(lane == l1, neg, fine)
    v2 = jnp.max(fine2, axis=-1, keepdims=True)
    l2 = jnp.min(jnp.where(fine2 == v2, lane, big), axis=-1, keepdims=True)
    e2 = jnp.exp(v2 - v1)
    w1 = p_g / (1.0 + e2)
    w2 = p_g * e2 / (1.0 + e2)
    route = jnp.where(lane == 0, l1 - n_groups,
                      jnp.where(lane == 1, l2 - n_groups,
                                jnp.where(lane == 2, w1, jnp.where(lane == 3, w2, 0.0))))
    return route


def _mix_out_kernel(y_ref, x_ref, gp_ref, sh2_ref, gs_ref, wout_ref, wr_ref, br_ref,
                    x1_ref, h2_ref, route_ref, route_t_ref, o_a, o_b, hb_buf, *, tm, n_groups,
                    per_group):
    half = tm // 2
    nc = MIX_COLS
    n_chunks = wout_ref.shape[1] // nc
    rows = half // n_chunks

    def finish_rows(o_half, base, q):
        r0 = q * rows
        o = o_half[r0:r0 + rows, :]
        x1 = x_ref[0, base + r0:base + r0 + rows, :] + o * _rms(o) * gp_ref[0]
        x1_ref[0, base + r0:base + r0 + rows, :] = x1
        h2 = x1 * _rms(x1) * gs_ref[0] + sh2_ref[0]
        h2_ref[0, base + r0:base + r0 + rows, :] = _pack_bf16_pairs(h2)
        hb_buf[base + r0:base + r0 + rows, :] = h2.astype(_BF16)

    y_lo = y_ref[0, 0:half, :]
    y_hi = y_ref[0, half:tm, :]
    for q in range(n_chunks):
        o_a[:, q * nc:(q + 1) * nc] = _dot(y_lo, wout_ref[:, q * nc:(q + 1) * nc])
    for q in range(n_chunks):
        o_b[:, q * nc:(q + 1) * nc] = _dot(y_hi, wout_ref[:, q * nc:(q + 1) * nc])
        finish_rows(o_a, 0, q)
    for q in range(n_chunks):
        finish_rows(o_b, half, q)
    route = _route_tile(hb_buf[...], wr_ref[...], br_ref[...], n_groups, per_group)
    route_ref[0] = route
    route_t_ref[...] = jnp.transpose(route)[0:ROUTE_ROWS, :]


def _mix_out(y, x, gate_post, shift2, gain_scale, w_out_bf, w_route, b_route, n_groups, per_group):
    bsz, t, d = x.shape
    tm = TM_MIX
    full = lambda shape: pl.BlockSpec(shape, lambda b, s: (0,) * len(shape))
    per_b = pl.BlockSpec((1, 1, d), lambda b, s: (b, 0, 0))
    tile = pl.BlockSpec((1, tm, d), lambda b, s: (b, s, 0))
    return pl.pallas_call(
        functools.partial(_mix_out_kernel, tm=tm, n_groups=n_groups, per_group=per_group),
        grid=(bsz, t // tm),
        in_specs=[
            tile, tile, per_b, per_b, per_b,
            pl.BlockSpec(w_out_bf.shape, lambda b, s: (0, 0), pipeline_mode=pl.Buffered(1)),
            full((d, ROUTE_LANES)), full((1, ROUTE_LANES)),
        ],
        out_specs=[tile, pl.BlockSpec((1, tm, d // 2), lambda b, s: (b, s, 0)),
                   pl.BlockSpec((1, tm, ROUTE_LANES), lambda b, s: (b, s, 0)),
                   pl.BlockSpec((ROUTE_ROWS, tm), lambda b, s: (0, b * (t // tm) + s))],
        out_shape=[
            jax.ShapeDtypeStruct((bsz, t, d), _F32),
            jax.ShapeDtypeStruct((bsz, t, d // 2), jnp.uint32),
            jax.ShapeDtypeStruct((bsz, t, ROUTE_LANES), _F32),
            jax.ShapeDtypeStruct((ROUTE_ROWS, bsz * t), _F32),
        ],
        scratch_shapes=[pltpu.VMEM((tm // 2, d), _F32), pltpu.VMEM((tm // 2, d), _F32),
                        pltpu.VMEM((tm, d), _BF16)],
        compiler_params=pltpu.CompilerParams(
            dimension_semantics=("arbitrary", "arbitrary"), vmem_limit_bytes=VMEM_LIMIT),
        name="mix_out",
    )(y, x, gate_post, shift2, gain_scale, w_out_bf, w_route, b_route)


def _rank_kernel(rt_ref, u_ref, dest_ref, cnt_ref, carry, pbase, *, tq, tm):
    phase = pl.program_id(0)
    j = pl.program_id(1)
    sub = lax.broadcasted_iota(jnp.int32, (EXPERT_ROWS, tq), 0).astype(_F32)
    oh0 = sub == rt_ref[0:1, :]
    oh1 = sub == rt_ref[1:2, :]
    both = jnp.where(oh0, 1.0, 0.0) + jnp.where(oh1, 1.0, 0.0)
    tile_counts = jnp.sum(both, axis=1, keepdims=True)

    @pl.when(jnp.logical_and(phase == 0, j == 0))
    def _():
        carry[...] = jnp.zeros(carry.shape, _F32)

    @pl.when(jnp.logical_and(phase == 1, j == 0))
    def _():
        cnt = carry[...]
        cnt_ref[...] = cnt
        tiles = jnp.floor((cnt + (tm - 1)) * (1.0 / tm))
        hi = jnp.floor(tiles * (1.0 / 16.0))
        lo = tiles - 16.0 * hi
        r = lax.broadcasted_iota(jnp.int32, (EXPERT_ROWS, EXPERT_ROWS), 0)
        c = lax.broadcasted_iota(jnp.int32, (EXPERT_ROWS, EXPERT_ROWS), 1)
        before = jnp.where(c < r, 1.0, 0.0).astype(_BF16)
        start_tiles = 16.0 * _dot(before, hi.astype(_BF16)) + _dot(before, lo.astype(_BF16))
        pbase[...] = start_tiles * tm
        carry[...] = jnp.zeros(carry.shape, _F32)

    @pl.when(phase == 1)
    def _():
        earlier = _dot(both.astype(_BF16), u_ref[...])
        slot = earlier + carry[:, 0:1] + pbase[:, 0:1]
        d0 = jnp.sum(jnp.where(oh0, slot, 0.0), axis=0, keepdims=True)
        d1 = jnp.sum(jnp.where(oh1, slot, 0.0), axis=0, keepdims=True)
        row = lax.broadcasted_iota(jnp.int32, (ROUTE_ROWS, tq), 0)
        dest_ref[...] = jnp.where(row == 0, d0, jnp.where(row == 1, d1, 0.0)).astype(jnp.int32)

    carry[...] = carry[...] + tile_counts


def _rank(route_t, tm):
    n_tok = route_t.shape[1]
    tq = TQ_RANK
    assert tm & (tm - 1) == 0, "segment padding must be a power of two for exact f32 arithmetic"
    idx = jnp.arange(tq, dtype=jnp.int32)
    earlier_mask = (idx[:, None] < idx[None, :]).astype(_BF16)
    return pl.pallas_call(
        functools.partial(_rank_kernel, tq=tq, tm=tm),
        grid=(2, n_tok // tq),
        in_specs=[
            pl.BlockSpec((ROUTE_ROWS, tq), lambda p, j: (0, j)),
            pl.BlockSpec((tq, tq), lambda p, j: (0, 0)),
        ],
        out_specs=[pl.BlockSpec((ROUTE_ROWS, tq), lambda p, j: (0, j * p)),
                   pl.BlockSpec((EXPERT_ROWS, ROUTE_LANES), lambda p, j: (0, 0))],
        out_shape=[jax.ShapeDtypeStruct((ROUTE_ROWS, n_tok), jnp.int32),
                   jax.ShapeDtypeStruct((EXPERT_ROWS, ROUTE_LANES), _F32)],
        scratch_shapes=[pltpu.VMEM((EXPERT_ROWS, ROUTE_LANES), _F32),
                        pltpu.VMEM((EXPERT_ROWS, ROUTE_LANES), _F32)],
        compiler_params=pltpu.CompilerParams(
            dimension_semantics=("arbitrary", "arbitrary"), vmem_limit_bytes=VMEM_LIMIT),
        name="rank",
    )(route_t, earlier_mask)


def _row_copy(src, dst, sem):
    return pltpu.make_async_copy(src, dst, sem)


def _scatter_kernel(ps_ref, pn_ref, used_ref, d0_ref, d1_ref, h_ref, xs_hbm, zrow, sem, psem, *, tk,
                    tm):
    i = pl.program_id(0)

    @pl.when(i == 0)
    def _():
        zrow[...] = jnp.zeros(zrow.shape, zrow.dtype)

    @pl.when(i == pl.num_programs(0) - 1)
    def _():
        n_rows = xs_hbm.shape[0]

        def fill(t, carry):
            dst = xs_hbm.at[pl.ds(pl.multiple_of(t * tm, tm), tm)]
            _row_copy(zrow, dst, psem).start()
            _row_copy(zrow, dst, psem).wait()
            return carry

        lax.fori_loop(used_ref[0], n_rows // tm, fill, 0)

    for r in range(tk):
        src = h_ref.at[pl.ds(r, 1)]
        _row_copy(src, xs_hbm.at[pl.ds(d0_ref[0, 0, r], 1)], sem).start(priority=0)
        _row_copy(src, xs_hbm.at[pl.ds(d1_ref[0, 0, r], 1)], sem).start(priority=1)

    pad_start = ps_ref[i]
    n_pad = pn_ref[i]
    zsrc = zrow.at[pl.ds(0, 1)]

    def start_pad(r, carry):
        _row_copy(zsrc, xs_hbm.at[pl.ds(pad_start + r, 1)], psem).start()
        return carry

    def wait_pad(r, carry):
        _row_copy(zsrc, xs_hbm.at[pl.ds(0, 1)], psem).wait()
        return carry

    lax.fori_loop(0, n_pad, start_pad, 0)
    lax.fori_loop(0, n_pad, wait_pad, 0)
    for _ in range(TOP_K):
        _row_copy(h_ref, xs_hbm.at[pl.ds(0, tk)], sem).wait()


def _scatter(h2_flat, d0_tiles, d1_tiles, pad_start, pad_len, used_tiles, n_rows):
    n_tok, d = h2_flat.shape
    tk = TM_CMB
    tm = TM_EXP
    smem_tile = pl.BlockSpec((1, 1, tk), lambda i, ps, pn, used: (i, 0, 0), memory_space=pltpu.SMEM)
    return pl.pallas_call(
        functools.partial(_scatter_kernel, tk=tk, tm=tm),
        grid_spec=pltpu.PrefetchScalarGridSpec(
            num_scalar_prefetch=3,
            grid=(n_tok // tk,),
            in_specs=[smem_tile, smem_tile, pl.BlockSpec((tk, d), lambda i, ps, pn, used: (i, 0))],
            out_specs=pl.BlockSpec(memory_space=pl.ANY),
            scratch_shapes=[pltpu.VMEM((tm, d), h2_flat.dtype), pltpu.SemaphoreType.DMA(()),
                            pltpu.SemaphoreType.DMA(())],
        ),
        out_shape=jax.ShapeDtypeStruct((n_rows, d), h2_flat.dtype),
        compiler_params=pltpu.CompilerParams(
            dimension_semantics=("arbitrary",), vmem_limit_bytes=VMEM_LIMIT),
        name="scatter",
    )(pad_start, pad_len, used_tiles, d0_tiles, d1_tiles, h2_flat)


def _expert_kernel(tv_ref, par_ref, glo_ref, ghi_ref, sege_ref, ntot_ref, last_ref,
                   xs_ref, wg_hbm, wu_hbm, wd_hbm, ys_ref,
                   wgb, wub, wdb, sg, su, sd, sem, *, layer, nch):
    del last_ref
    i = pl.program_id(0)
    rg = wgb.shape[1] // nch
    rd = wdb.shape[1] // nch

    def chunk_copies(g):
        e = sege_ref[g // nch]
        c = g % nch
        s = g % 2
        rows_g = pl.ds(pl.multiple_of(c * rg, rg), rg)
        rows_d = pl.ds(pl.multiple_of(c * rd, rd), rd)
        return (_row_copy(wg_hbm.at[layer, e, rows_g], sg.at[s], sem.at[s]),
                _row_copy(wu_hbm.at[layer, e, rows_g], su.at[s], sem.at[s]),
                _row_copy(wd_hbm.at[layer, e, rows_d], sd.at[s], sem.at[s]))

    def start_chunk(g):
        for cp in chunk_copies(g):
            cp.start()

    def process_chunk(g, carry):
        for cp in chunk_copies(g):
            cp.wait()
        c = g % nch
        s = g % 2
        p = (g // nch) % 2
        wgb[p, pl.ds(pl.multiple_of(c * rg, rg), rg), :] = sg[s].astype(_BF16)
        wub[p, pl.ds(pl.multiple_of(c * rg, rg), rg), :] = su[s].astype(_BF16)
        wdb[p, pl.ds(pl.multiple_of(c * rd, rd), rd), :] = sd[s].astype(_BF16)

        @pl.when(g + 2 < ntot_ref[0])
        def _():
            start_chunk(g + 2)
        return carry

    @pl.when(i == 0)
    def _():
        start_chunk(0)
        start_chunk(1)
        lax.fori_loop(0, nch, process_chunk, 0)

    valid = tv_ref[i] == 1

    @pl.when(valid)
    def _():
        lax.fori_loop(glo_ref[i], ghi_ref[i], process_chunk, 0)
        p = par_ref[i]
        lo, hi = _unpack_bf16_pairs(xs_ref[...])
        xb = jnp.concatenate([lo.astype(_BF16), hi.astype(_BF16)], axis=1)
        g = _dot(xb, wgb[p])
        u = _dot(xb, wub[p])
        hid = (g * jax.nn.sigmoid(g) * u).astype(_BF16)
        ys_ref[...] = _pack_bf16_pairs(_dot(hid, wdb[p]))

    @pl.when(jnp.logical_not(valid))
    def _():
        ys_ref[...] = jnp.zeros(ys_ref.shape, ys_ref.dtype)


def _experts(xs, tables, w_gate, w_up, w_down, layer):
    n_rows, dp = xs.shape
    d = w_gate.shape[2]
    tm = TM_EXP
    f = w_gate.shape[3]
    nch = W_CHUNKS
    any_spec = pl.BlockSpec(memory_space=pl.ANY)
    return pl.pallas_call(
        functools.partial(_expert_kernel, layer=layer, nch=nch),
        grid_spec=pltpu.PrefetchScalarGridSpec(
            num_scalar_prefetch=7,
            grid=(n_rows // tm,),
            in_specs=[
                pl.BlockSpec((tm, dp), lambda i, *s: (jnp.minimum(i, s[6][0]), 0)),
                any_spec, any_spec, any_spec,
            ],
            out_specs=pl.BlockSpec((tm, dp), lambda i, *s: (i, 0)),
            scratch_shapes=[
                pltpu.VMEM((2, d, f), _BF16), pltpu.VMEM((2, d, f), _BF16), pltpu.VMEM((2, f, d), _BF16),
                pltpu.VMEM((2, d // nch, f), _F32), pltpu.VMEM((2, d // nch, f), _F32),
                pltpu.VMEM((2, f // nch, d), _F32), pltpu.SemaphoreType.DMA((2,)),
            ],
        ),
        out_shape=jax.ShapeDtypeStruct((n_rows, dp), jnp.uint32),
        compiler_params=pltpu.CompilerParams(
            dimension_semantics=("arbitrary",), vmem_limit_bytes=VMEM_LIMIT),
        name="experts",
    )(*tables, xs, w_gate, w_up, w_down)


def _start_row_gather(src_hbm, idx_ref, dst, sem, n_rows):
    for r in range(n_rows):
        _row_copy(src_hbm.at[pl.ds(idx_ref[0, 0, r], 1)], dst.at[pl.ds(r, 1)], sem).start(
            priority=r % 2)


def _wait_row_gather(src_hbm, dst, sem, n_rows):
    _row_copy(src_hbm.at[pl.ds(0, n_rows)], dst, sem).wait()


def _combine_kernel(d_cur, d_nxt, ys_hbm, x1_ref, route_ref, gp_ref, o_ref, rbuf, sem, *,
                    tk):
    i = pl.program_id(0)
    n = pl.num_programs(0)
    slot = i % 2

    @pl.when(i == 0)
    def _():
        _start_row_gather(ys_hbm, d_cur, rbuf.at[0], sem.at[0], TOP_K * tk)

    _wait_row_gather(ys_hbm, rbuf.at[slot], sem.at[slot], TOP_K * tk)

    @pl.when(i + 1 < n)
    def _():
        _start_row_gather(ys_hbm, d_nxt, rbuf.at[1 - slot], sem.at[1 - slot], TOP_K * tk)

    route = route_ref[0]
    lo0, hi0 = _unpack_bf16_pairs(rbuf[slot, 0:tk, :])
    lo1, hi1 = _unpack_bf16_pairs(rbuf[slot, tk:2 * tk, :])
    w0 = route[:, 2:3]
    w1 = route[:, 3:4]
    y = jnp.concatenate([w0 * lo0 + w1 * lo1, w0 * hi0 + w1 * hi1], axis=1)
    o_ref[0] = x1_ref[0] + y * _rms(y) * gp_ref[0]


def _combine(ys, dest_tiles, x1, route, gate_post):
    bsz, t, d = x1.shape
    tk = TM_CMB
    per_seq = t // tk
    n_tiles = bsz * per_seq
    smem_tile = lambda imap: pl.BlockSpec((1, 1, TOP_K * tk), imap, memory_space=pltpu.SMEM)
    return pl.pallas_call(
        functools.partial(_combine_kernel, tk=tk),
        grid=(n_tiles,),
        in_specs=[
            smem_tile(lambda i: (i, 0, 0)),
            smem_tile(lambda i: (jnp.minimum(i + 1, n_tiles - 1), 0, 0)),
            pl.BlockSpec(memory_space=pl.ANY),
            pl.BlockSpec((1, tk, d), lambda i: (i // per_seq, i % per_seq, 0)),
            pl.BlockSpec((1, tk, ROUTE_LANES), lambda i: (i // per_seq, i % per_seq, 0)),
            pl.BlockSpec((1, 1, d), lambda i: (i // per_seq, 0, 0)),
        ],
        out_specs=pl.BlockSpec((1, tk, d), lambda i: (i // per_seq, i % per_seq, 0)),
        out_shape=jax.ShapeDtypeStruct((bsz, t, d), _F32),
        scratch_shapes=[pltpu.VMEM((2, TOP_K * tk, d // 2), jnp.uint32),
                        pltpu.SemaphoreType.DMA((2,))],
        compiler_params=pltpu.CompilerParams(
            dimension_semantics=("arbitrary",), vmem_limit_bytes=VMEM_LIMIT),
        name="combine",
    )(dest_tiles, dest_tiles, ys, x1, route, gate_post)


def _segment_tables(counts, tm, n_tiles, n_steps, nch):
    n_experts = counts.shape[0]
    i32 = jnp.int32
    nt = (counts + tm - 1) // tm
    tend = jnp.cumsum(nt)
    tstart = tend - nt
    tile_idx = jnp.arange(n_tiles, dtype=i32)
    valid = tile_idx < tend[-1]
    nonempty = nt > 0
    seg_of_e = jnp.cumsum(nonempty.astype(i32)) - 1
    n_seg = seg_of_e[-1] + 1
    ids = jnp.arange(n_experts, dtype=i32)
    seg_expert = jnp.sum(
        jnp.where(nonempty[None, :] & (seg_of_e[None, :] == ids[:, None]), ids[None, :], 0), axis=1)
    owns = (tstart[None, :] <= tile_idx[:, None]) & (tile_idx[:, None] < tend[None, :])
    pick = lambda v: jnp.sum(jnp.where(owns, v[None, :], 0), axis=1)
    k = pick(seg_of_e)
    j = tile_idx - pick(tstart)
    ntt = jnp.maximum(pick(nt), 1)
    has_next = valid & (k + 1 < n_seg)
    glo = jnp.where(has_next, (k + 1) * nch + j * nch // ntt, 0)
    ghi = jnp.where(has_next, (k + 1) * nch + (j + 1) * nch // ntt, 0)
    expert_tables = (valid.astype(i32), (k % 2).astype(i32), glo.astype(i32), ghi.astype(i32),
                     seg_expert.astype(i32), (n_seg * nch).reshape(1).astype(i32),
                     (tend[-1:] - 1).astype(i32))
    pad_start = jnp.zeros((n_steps,), i32).at[:n_experts].set(tstart * tm + counts)
    pad_len = jnp.zeros((n_steps,), i32).at[:n_experts].set(nt * tm - counts)
    return expert_tables, pad_start, pad_len, tend[-1:].astype(i32)


def kernel(x, c, w_ada, b_ada, norm_mix_pre, norm_mix_post, w_in, conv_a, norm_a_out, conv_b, conv_b_bias, ln_b_gain, ln_b_bias, w_out, norm_moe_pre, norm_moe_post, w_router_group, b_router_group, w_router_expert, b_router_expert, w_expert_gate, w_expert_up, w_expert_down):
    bsz, t, d = x.shape
    depth = w_ada.shape[0]
    n_groups = w_router_group.shape[2]
    per_group = w_router_expert.shape[3]
    n_experts = n_groups * per_group
    n_tok = bsz * t
    n_assign = n_tok * TOP_K
    n_tiles = (n_assign + n_experts * (TM_EXP - 1) + TM_EXP - 1) // TM_EXP
    row = lambda v: v.reshape(1, -1)

    for l in range(depth):
        mod = _ada(c, w_ada[l], b_ada[l])
        shift1, scale1, gate1, shift2, scale2, gate2 = [
            m.reshape(bsz, 1, d) for m in jnp.split(mod, N_MOD, axis=-1)]

        y = _mix_in(x, shift1, norm_mix_pre[l] * (1.0 + scale1), w_in[l].astype(_BF16), conv_a[l],
                    row(norm_a_out[l]), conv_b[l], row(conv_b_bias[l]), row(ln_b_gain[l]),
                    row(ln_b_bias[l]))

        n_route = n_groups + n_experts
        w_fine = jnp.transpose(w_router_expert[l], (1, 0, 2)).reshape(d, n_experts)
        w_route = jnp.zeros((d, ROUTE_LANES), _F32).at[:, :n_route].set(
            jnp.concatenate([w_router_group[l], w_fine], axis=1)).astype(_BF16)
        b_route = jnp.zeros((1, ROUTE_LANES), _F32).at[0, :n_route].set(
            jnp.concatenate([b_router_group[l], b_router_expert[l].reshape(-1)]))
        x1, h2, route, route_t = _mix_out(
            y, x, gate1 * norm_mix_post[l], shift2, norm_moe_pre[l] * (1.0 + scale2),
            w_out[l].astype(_BF16), w_route, b_route, n_groups, per_group)

        dest, counts_f = _rank(route_t, TM_EXP)
        counts = counts_f[:n_experts, 0].astype(jnp.int32)
        n_ctiles = n_tok // TM_CMB
        assert n_ctiles >= n_experts
        expert_tables, pad_start, pad_len, used_tiles = _segment_tables(
            counts, TM_EXP, n_tiles, n_ctiles, W_CHUNKS)
        d0_tiles = dest[0].reshape(n_ctiles, 1, TM_CMB)
        d1_tiles = dest[1].reshape(n_ctiles, 1, TM_CMB)
        xs = _scatter(h2.reshape(n_tok, d // 2), d0_tiles, d1_tiles, pad_start, pad_len, used_tiles,
                      n_tiles * TM_EXP)
        ys = _experts(xs, expert_tables, w_expert_gate, w_expert_up, w_expert_down, l)

        dest_tiles = jnp.concatenate([d0_tiles, d1_tiles], axis=2)
        x = _combine(ys, dest_tiles, x1, route, gate2 * norm_moe_post[l])
    return x
```

```python
import functools

import jax
import jax.numpy as jnp
from jax import lax
from jax.experimental import pallas as pl
from jax.experimental.pallas import tpu as pltpu

EPS = 1e-6
HEAD_DIM = 128
KERNEL_A = 3
KERNEL_B = 31
N_MOD = 6
TOP_K = 2
ROUTE_LANES = 128
ROUTE_ROWS = 8
EXPERT_ROWS = 128
TQ_RANK = 1024
HIST_A = 8
HIST_B = 32
CONV_ROWS = 128
MIX_COLS = 256
OUT_PARTS = 2
W_CHUNKS = 8

TM_MIX = 512
TM_EXP = 256
TM_CMB = 256
ADA_COLS = 1024
VMEM_LIMIT = 56 * 1024 * 1024

_BF16 = jnp.bfloat16
_F32 = jnp.float32


def _dot(a, b):
    return jnp.dot(a, b, preferred_element_type=_F32)


def _rms(v):
    return lax.rsqrt(jnp.mean(v * v, axis=-1, keepdims=True) + EPS)


def _pack_bf16_pairs(v):
    n = v.shape[1] // 2
    bits = pltpu.bitcast(v.astype(_BF16).astype(_F32), jnp.uint32)
    return (bits[:, :n] >> 16) | bits[:, n:]


def _unpack_bf16_pairs(p):
    lo = pltpu.bitcast(p << 16, _F32)
    hi = pltpu.bitcast(p & jnp.uint32(0xFFFF0000), _F32)
    return lo, hi


def _ada_kernel(c_ref, w_ref, b_ref, o_ref):
    c = c_ref[...]
    act = (c * jax.nn.sigmoid(c)).astype(_BF16)
    o_ref[...] = _dot(act, w_ref[...].astype(_BF16)) + b_ref[...]


def _ada(c, w_ada, b_ada):
    bsz, d = c.shape
    n_out = w_ada.shape[1]
    c_pad = jnp.zeros((8, d), _F32).at[:bsz].set(c)
    out = pl.pallas_call(
        _ada_kernel,
        grid=(n_out // ADA_COLS,),
        in_specs=[
            pl.BlockSpec((8, d), lambda j: (0, 0)),
            pl.BlockSpec((d, ADA_COLS), lambda j: (0, j)),
            pl.BlockSpec((1, ADA_COLS), lambda j: (0, j)),
        ],
        out_specs=pl.BlockSpec((8, ADA_COLS), lambda j: (0, j)),
        out_shape=jax.ShapeDtypeStruct((8, n_out), _F32),
        compiler_params=pltpu.CompilerParams(
            dimension_semantics=("arbitrary",), vmem_limit_bytes=VMEM_LIMIT),
        name="ada",
    )(c_pad, w_ada, b_ada.reshape(1, n_out))
    return out[:bsz]


def _dwconv_chunk(ext_ref, w_ref, r0, rows, lo, hi, hist, ksize):
    span = rows + hist
    first = hist - (ksize - 1)
    xe = ext_ref[r0:r0 + span, lo:hi]
    acc = None
    for b in range(8):
        offsets = [o for o in range(first, hist + 1) if o % 8 == b]
        if not offsets:
            continue
        shifted = xe if b == 0 else pltpu.roll(xe, span - b, 0)
        for o in offsets:
            k = o - first
            term = w_ref[k:k + 1, lo:hi] * shifted[o - b:o - b + rows, :]
            acc = term if acc is None else acc + term
    return acc


def _mix_in_kernel(x_ref, sh_ref, gs_ref, w_ref, ca_ref, na_ref, cb_ref, cbb_ref,
                   lg_ref, lb_ref, y_ref, cv_ext, u_ext, ba_buf, conv_b_out, *, tm, da, db):
    nc = MIX_COLS
    @pl.when(pl.program_id(1) == 0)
    def _():
        cv_ext[...] = jnp.zeros(cv_ext.shape, _F32)
        u_ext[0:HIST_B, :] = jnp.zeros((HIST_B, db), _F32)
        ba_buf[...] = jnp.zeros(ba_buf.shape, _F32)
        conv_b_out[...] = jnp.zeros(conv_b_out.shape, _F32)

    x = x_ref[0]
    h = x * _rms(x) * gs_ref[0] + sh_ref[0]
    hb = h.astype(_BF16)

    def finish_head(hd):
        lo, hi = hd * HEAD_DIM, (hd + 1) * HEAD_DIM
        for r0 in range(0, tm, CONV_ROWS):
            conv = _dwconv_chunk(cv_ext, ca_ref, r0, CONV_ROWS, lo, hi, HIST_A, KERNEL_A)
            seg = ba_buf[r0:r0 + CONV_ROWS, lo:hi] * conv
            y_ref[0, r0:r0 + CONV_ROWS, lo:hi] = (seg * _rms(seg) * na_ref[:, lo:hi]).astype(_BF16)

    def finish_rows(r0, rows):
        acc = conv_b_out[r0:r0 + rows, :]
        mu = jnp.mean(acc, axis=-1, keepdims=True)
        cen = acc - mu
        var = jnp.mean(cen * cen, axis=-1, keepdims=True)
        yn = cen * lax.rsqrt(var + EPS) * lg_ref[...] + lb_ref[...]
        y_ref[0, r0:r0 + rows, da:da + db] = (yn * jax.nn.sigmoid(yn)).astype(_BF16)

    n_b = db // nc
    heads_per = (da // HEAD_DIM) // n_b
    ln_rows = tm // n_b
    for j in range(n_b):
        lo = j * nc
        val = _dot(hb, w_ref[:, 3 * da + lo:3 * da + lo + nc])
        gate = _dot(hb, w_ref[:, 3 * da + db + lo:3 * da + db + lo + nc])
        for hd in range(j * heads_per, (j + 1) * heads_per):
            finish_head(hd)
        finish_rows(j * ln_rows, ln_rows)
        u_ext[HIST_B:HIST_B + tm, lo:lo + nc] = val * jax.nn.sigmoid(gate)
    cv_ext[0:HIST_A, :] = cv_ext[tm:tm + HIST_A, :]

    conv_chunks = [(lo, r0) for lo in range(0, db, HEAD_DIM) for r0 in range(0, tm, CONV_ROWS)]
    n_a = da // nc
    n_mm = 3 * n_a
    done = 0
    for q in range(n_mm):
        j = q // 2 if q < 2 * n_a else q - 2 * n_a
        lo = j * nc
        if q < 2 * n_a and q % 2 == 0:
            c_part = _dot(hb, w_ref[:, da + lo:da + lo + nc])
        elif q < 2 * n_a:
            v_part = _dot(hb, w_ref[:, 2 * da + lo:2 * da + lo + nc])
            cv_ext[HIST_A:HIST_A + tm, lo:lo + nc] = c_part * v_part
        else:
            ba_buf[:, lo:lo + nc] = _dot(hb, w_ref[:, lo:lo + nc])
        upto = (q + 1) * len(conv_chunks) // n_mm
        for lo_c, r0 in conv_chunks[done:upto]:
            conv = _dwconv_chunk(u_ext, cb_ref, r0, CONV_ROWS, lo_c, lo_c + HEAD_DIM, HIST_B, KERNEL_B)
            conv_b_out[r0:r0 + CONV_ROWS, lo_c:lo_c + HEAD_DIM] = conv + cbb_ref[:, lo_c:lo_c + HEAD_DIM]
        done = upto
    u_ext[0:HIST_B, :] = u_ext[tm:tm + HIST_B, :]


def _mix_in(x, shift, gain_scale, w_in_bf, conv_a, norm_a, conv_b, conv_b_bias, ln_g, ln_b):
    bsz, t, d = x.shape
    da = conv_a.shape[1]
    db = conv_b.shape[1]
    tm = TM_MIX
    full = lambda shape: pl.BlockSpec(shape, lambda b, s: (0,) * len(shape))
    per_b = pl.BlockSpec((1, 1, d), lambda b, s: (b, 0, 0))
    n_t = t // tm
    return pl.pallas_call(
        functools.partial(_mix_in_kernel, tm=tm, da=da, db=db),
        grid=(bsz, n_t + 1),
        in_specs=[
            pl.BlockSpec((1, tm, d), lambda b, s: (b, jnp.minimum(s, n_t - 1), 0)),
            per_b, per_b,
            pl.BlockSpec(w_in_bf.shape, lambda b, s: (0, 0), pipeline_mode=pl.Buffered(1)),
            full((KERNEL_A, da)), full((1, da)),
            full((KERNEL_B, db)), full((1, db)), full((1, db)), full((1, db)),
        ],
        out_specs=pl.BlockSpec((1, tm, da + db), lambda b, s: (b, jnp.maximum(s - 1, 0), 0)),
        out_shape=jax.ShapeDtypeStruct((bsz, t, da + db), _BF16),
        scratch_shapes=[pltpu.VMEM((HIST_A + tm, da), _F32), pltpu.VMEM((HIST_B + tm, db), _F32),
                        pltpu.VMEM((tm, da), _F32), pltpu.VMEM((tm, db), _F32)],
        compiler_params=pltpu.CompilerParams(
            dimension_semantics=("arbitrary", "arbitrary"), vmem_limit_bytes=VMEM_LIMIT),
        name="mix_in",
    )(x, shift, gain_scale, w_in_bf, conv_a, norm_a, conv_b, conv_b_bias, ln_g, ln_b)


def _route_tile(hb, wr, br, n_groups, per_group):
    logits = _dot(hb, wr) + br
    lane = lax.broadcasted_iota(jnp.int32, logits.shape, 1).astype(_F32)
    neg = jnp.float32(-jnp.inf)
    big = jnp.float32(ROUTE_LANES)
    is_coarse = lane < n_groups
    coarse = jnp.where(is_coarse, logits, neg)
    m = jnp.max(coarse, axis=-1, keepdims=True)
    g_sel = jnp.min(jnp.where(coarse == m, lane, big), axis=-1, keepdims=True)
    p_g = 1.0 / jnp.sum(jnp.where(is_coarse, jnp.exp(logits - m), 0.0), axis=-1, keepdims=True)
    lo = n_groups + g_sel * per_group
    fine = jnp.where((lane >= lo) & (lane < lo + per_group), logits, neg)
    v1 = jnp.max(fine, axis=-1, keepdims=True)
    l1 = jnp.min(jnp.where(fine == v1, lane, big), axis=-1, keepdims=True)
    fine2 = jnp.where(lane == l1, neg, fine)
    v2 = jnp.max(fine2, axis=-1, keepdims=True)
    l2 = jnp.min(jnp.where(fine2 == v2, lane, big), axis=-1, keepdims=True)
    e2 = jnp.exp(v2 - v1)
    w1 = p_g / (1.0 + e2)
    w2 = p_g * e2 / (1.0 + e2)
    route = jnp.where(lane == 0, l1 - n_groups,
                      jnp.where(lane == 1, l2 - n_groups,
                                jnp.where(lane == 2, w1, jnp.where(lane == 3, w2, 0.0))))
    return route


def _mix_out_kernel(y_ref, x_ref, gp_ref, sh2_ref, gs_ref, wout_ref, wr_ref, br_ref,
                    x1_ref, h2_ref, route_ref, route_t_ref, o_a, o_b, hb_buf, *, tm, n_groups,
                    per_group):
    part = tm // OUT_PARTS
    nc = MIX_COLS
    n_chunks = wout_ref.shape[1] // nc
    rows = part // n_chunks
    bufs = (o_a, o_b)

    def finish_rows(o_part, base, q):
        r0 = q * rows
        o = o_part[r0:r0 + rows, :]
        x1 = x_ref[0, base + r0:base + r0 + rows, :] + o * _rms(o) * gp_ref[0]
        x1_ref[0, base + r0:base + r0 + rows, :] = x1
        h2 = x1 * _rms(x1) * gs_ref[0] + sh2_ref[0]
        h2_ref[0, base + r0:base + r0 + rows, :] = _pack_bf16_pairs(h2)
        hb_buf[base + r0:base + r0 + rows, :] = h2.astype(_BF16)

    for k in range(OUT_PARTS + 1):
        y_part = y_ref[0, k * part:(k + 1) * part, :] if k < OUT_PARTS else None
        for q in range(n_chunks):
            if k < OUT_PARTS:
                bufs[k % 2][:, q * nc:(q + 1) * nc] = _dot(y_part, wout_ref[:, q * nc:(q + 1) * nc])
            if k > 0:
                finish_rows(bufs[(k - 1) % 2], (k - 1) * part, q)
    route = _route_tile(hb_buf[...], wr_ref[...], br_ref[...], n_groups, per_group)
    route_ref[0] = route
    route_t_ref[...] = jnp.transpose(route)[0:ROUTE_ROWS, :]


def _mix_out(y, x, gate_post, shift2, gain_scale, w_out_bf, w_route, b_route, n_groups, per_group):
    bsz, t, d = x.shape
    tm = TM_MIX
    full = lambda shape: pl.BlockSpec(shape, lambda b, s: (0,) * len(shape))
    per_b = pl.BlockSpec((1, 1, d), lambda b, s: (b, 0, 0))
    tile = pl.BlockSpec((1, tm, d), lambda b, s: (b, s, 0))
    return pl.pallas_call(
        functools.partial(_mix_out_kernel, tm=tm, n_groups=n_groups, per_group=per_group),
        grid=(bsz, t // tm),
        in_specs=[
            tile, tile, per_b, per_b, per_b,
            pl.BlockSpec(w_out_bf.shape, lambda b, s: (0, 0), pipeline_mode=pl.Buffered(1)),
            full((d, ROUTE_LANES)), full((1, ROUTE_LANES)),
        ],
        out_specs=[tile, pl.BlockSpec((1, tm, d // 2), lambda b, s: (b, s, 0)),
                   pl.BlockSpec((1, tm, ROUTE_LANES), lambda b, s: (b, s, 0)),
                   pl.BlockSpec((ROUTE_ROWS, tm), lambda b, s: (0, b * (t // tm) + s))],
        out_shape=[
            jax.ShapeDtypeStruct((bsz, t, d), _F32),
            jax.ShapeDtypeStruct((bsz, t, d // 2), jnp.uint32),
            jax.ShapeDtypeStruct((bsz, t, ROUTE_LANES), _F32),
            jax.ShapeDtypeStruct((ROUTE_ROWS, bsz * t), _F32),
        ],
        scratch_shapes=[pltpu.VMEM((tm // OUT_PARTS, d), _F32), pltpu.VMEM((tm // OUT_PARTS, d), _F32),
                        pltpu.VMEM((tm, d), _BF16)],
        compiler_params=pltpu.CompilerParams(
            dimension_semantics=("arbitrary", "arbitrary"), vmem_limit_bytes=VMEM_LIMIT),
        name="mix_out",
    )(y, x, gate_post, shift2, gain_scale, w_out_bf, w_route, b_route)


def _rank_kernel(rt_ref, u_ref, dest_ref, cnt_ref, carry, pbase, *, tq, tm):
    phase = pl.program_id(0)
    j = pl.program_id(1)
    sub = lax.broadcasted_iota(jnp.int32, (EXPERT_ROWS, tq), 0).astype(_F32)
    oh0 = sub == rt_ref[0:1, :]
    oh1 = sub == rt_ref[1:2, :]
    both = jnp.where(oh0, 1.0, 0.0) + jnp.where(oh1, 1.0, 0.0)
    tile_counts = jnp.sum(both, axis=1, keepdims=True)

    @pl.when(jnp.logical_and(phase == 0, j == 0))
    def _():
        carry[...] = jnp.zeros(carry.shape, _F32)

    @pl.when(jnp.logical_and(phase == 1, j == 0))
    def _():
        cnt = carry[...]
        cnt_ref[...] = cnt
        tiles = jnp.floor((cnt + (tm - 1)) * (1.0 / tm))
        hi = jnp.floor(tiles * (1.0 / 16.0))
        lo = tiles - 16.0 * hi
        r = lax.broadcasted_iota(jnp.int32, (EXPERT_ROWS, EXPERT_ROWS), 0)
        c = lax.broadcasted_iota(jnp.int32, (EXPERT_ROWS, EXPERT_ROWS), 1)
        before = jnp.where(c < r, 1.0, 0.0).astype(_BF16)
        start_tiles = 16.0 * _dot(before, hi.astype(_BF16)) + _dot(before, lo.astype(_BF16))
        pbase[...] = start_tiles * tm
        carry[...] = jnp.zeros(carry.shape, _F32)

    @pl.when(phase == 1)
    def _():
        earlier = _dot(both.astype(_BF16), u_ref[...])
        slot = earlier + carry[:, 0:1] + pbase[:, 0:1]
        d0 = jnp.sum(jnp.where(oh0, slot, 0.0), axis=0, keepdims=True)
        d1 = jnp.sum(jnp.where(oh1, slot, 0.0), axis=0, keepdims=True)
        row = lax.broadcasted_iota(jnp.int32, (ROUTE_ROWS, tq), 0)
        dest_ref[...] = jnp.where(row == 0, d0, jnp.where(row == 1, d1, 0.0)).astype(jnp.int32)

    carry[...] = carry[...] + tile_counts


def _rank(route_t, tm):
    n_tok = route_t.shape[1]
    tq = TQ_RANK
    assert tm & (tm - 1) == 0, "segment padding must be a power of two for exact f32 arithmetic"
    idx = jnp.arange(tq, dtype=jnp.int32)
    earlier_mask = (idx[:, None] < idx[None, :]).astype(_BF16)
    return pl.pallas_call(
        functools.partial(_rank_kernel, tq=tq, tm=tm),
        grid=(2, n_tok // tq),
        in_specs=[
            pl.BlockSpec((ROUTE_ROWS, tq), lambda p, j: (0, j)),
            pl.BlockSpec((tq, tq), lambda p, j: (0, 0)),
        ],
        out_specs=[pl.BlockSpec((ROUTE_ROWS, tq), lambda p, j: (0, j * p)),
                   pl.BlockSpec((EXPERT_ROWS, ROUTE_LANES), lambda p, j: (0, 0))],
        out_shape=[jax.ShapeDtypeStruct((ROUTE_ROWS, n_tok), jnp.int32),
                   jax.ShapeDtypeStruct((EXPERT_ROWS, ROUTE_LANES), _F32)],
        scratch_shapes=[pltpu.VMEM((EXPERT_ROWS, ROUTE_LANES), _F32),
                        pltpu.VMEM((EXPERT_ROWS, ROUTE_LANES), _F32)],
        compiler_params=pltpu.CompilerParams(
            dimension_semantics=("arbitrary", "arbitrary"), vmem_limit_bytes=VMEM_LIMIT),
        name="rank",
    )(route_t, earlier_mask)


def _row_copy(src, dst, sem):
    return pltpu.make_async_copy(src, dst, sem)


def _scatter_kernel(ps_ref, pn_ref, used_ref, d0_ref, d1_ref, h_ref, xs_hbm, zrow, sem, psem, *, tk,
                    tm):
    i = pl.program_id(0)

    @pl.when(i == 0)
    def _():
        zrow[...] = jnp.zeros(zrow.shape, zrow.dtype)

    @pl.when(i == pl.num_programs(0) - 1)
    def _():
        n_rows = xs_hbm.shape[0]

        def fill(t, carry):
            dst = xs_hbm.at[pl.ds(pl.multiple_of(t * tm, tm), tm)]
            _row_copy(zrow, dst, psem).start()
            _row_copy(zrow, dst, psem).wait()
            return carry

        lax.fori_loop(used_ref[0], n_rows // tm, fill, 0)

    for r in range(tk):
        src = h_ref.at[pl.ds(r, 1)]
        _row_copy(src, xs_hbm.at[pl.ds(d0_ref[0, 0, r], 1)], sem).start(priority=0)
        _row_copy(src, xs_hbm.at[pl.ds(d1_ref[0, 0, r], 1)], sem).start(priority=1)

    pad_start = ps_ref[i]
    n_pad = pn_ref[i]
    zsrc = zrow.at[pl.ds(0, 1)]

    def start_pad(r, carry):
        _row_copy(zsrc, xs_hbm.at[pl.ds(pad_start + r, 1)], psem).start()
        return carry

    def wait_pad(r, carry):
        _row_copy(zsrc, xs_hbm.at[pl.ds(0, 1)], psem).wait()
        return carry

    lax.fori_loop(0, n_pad, start_pad, 0)
    lax.fori_loop(0, n_pad, wait_pad, 0)
    for _ in range(TOP_K):
        _row_copy(h_ref, xs_hbm.at[pl.ds(0, tk)], sem).wait()


def _scatter(h2_flat, d0_tiles, d1_tiles, pad_start, pad_len, used_tiles, n_rows):
    n_tok, d = h2_flat.shape
    tk = TM_CMB
    tm = TM_EXP
    smem_tile = pl.BlockSpec((1, 1, tk), lambda i, ps, pn, used: (i, 0, 0), memory_space=pltpu.SMEM)
    return pl.pallas_call(
        functools.partial(_scatter_kernel, tk=tk, tm=tm),
        grid_spec=pltpu.PrefetchScalarGridSpec(
            num_scalar_prefetch=3,
            grid=(n_tok // tk,),
            in_specs=[smem_tile, smem_tile, pl.BlockSpec((tk, d), lambda i, ps, pn, used: (i, 0))],
            out_specs=pl.BlockSpec(memory_space=pl.ANY),
            scratch_shapes=[pltpu.VMEM((tm, d), h2_flat.dtype), pltpu.SemaphoreType.DMA(()),
                            pltpu.SemaphoreType.DMA(())],
        ),
        out_shape=jax.ShapeDtypeStruct((n_rows, d), h2_flat.dtype),
        compiler_params=pltpu.CompilerParams(
            dimension_semantics=("arbitrary",), vmem_limit_bytes=VMEM_LIMIT),
        name="scatter",
    )(pad_start, pad_len, used_tiles, d0_tiles, d1_tiles, h2_flat)


def _expert_kernel(tv_ref, par_ref, glo_ref, ghi_ref, sege_ref, ntot_ref, last_ref, half_ref,
                   xs_ref, wg_hbm, wu_hbm, wd_hbm, ys_ref,
                   wgb, wub, wdb, sg, su, sd, sem, *, layer, nch):
    del last_ref
    i = pl.program_id(0)
    rg = wgb.shape[1] // nch
    rd = wdb.shape[1] // nch

    def chunk_copies(g):
        e = sege_ref[g // nch]
        c = g % nch
        s = g % 2
        rows_g = pl.ds(pl.multiple_of(c * rg, rg), rg)
        rows_d = pl.ds(pl.multiple_of(c * rd, rd), rd)
        return (_row_copy(wg_hbm.at[layer, e, rows_g], sg.at[s], sem.at[s]),
                _row_copy(wu_hbm.at[layer, e, rows_g], su.at[s], sem.at[s]),
                _row_copy(wd_hbm.at[layer, e, rows_d], sd.at[s], sem.at[s]))

    def start_chunk(g):
        for cp in chunk_copies(g):
            cp.start()

    def process_chunk(g, carry):
        for cp in chunk_copies(g):
            cp.wait()
        c = g % nch
        s = g % 2
        p = (g // nch) % 2
        wgb[p, pl.ds(pl.multiple_of(c * rg, rg), rg), :] = sg[s].astype(_BF16)
        wub[p, pl.ds(pl.multiple_of(c * rg, rg), rg), :] = su[s].astype(_BF16)
        wdb[p, pl.ds(pl.multiple_of(c * rd, rd), rd), :] = sd[s].astype(_BF16)

        @pl.when(g + 2 < ntot_ref[0])
        def _():
            start_chunk(g + 2)
        return carry

    @pl.when(i == 0)
    def _():
        start_chunk(0)
        start_chunk(1)
        lax.fori_loop(0, nch, process_chunk, 0)

    valid = tv_ref[i] == 1

    def tile_mlp(rows):
        p = par_ref[i]
        lo, hi = _unpack_bf16_pairs(xs_ref[0:rows, :])
        xb = jnp.concatenate([lo.astype(_BF16), hi.astype(_BF16)], axis=1)
        g = _dot(xb, wgb[p])
        u = _dot(xb, wub[p])
        hid = (g * jax.nn.sigmoid(g) * u).astype(_BF16)
        ys_ref[0:rows, :] = _pack_bf16_pairs(_dot(hid, wdb[p]))

    @pl.when(valid)
    def _():
        lax.fori_loop(glo_ref[i], ghi_ref[i], process_chunk, 0)
        tm = ys_ref.shape[0]
        half_only = half_ref[i] == 1

        @pl.when(jnp.logical_not(half_only))
        def _():
            tile_mlp(tm)

        @pl.when(half_only)
        def _():
            tile_mlp(tm // 2)
            ys_ref[tm // 2:tm, :] = jnp.zeros((tm - tm // 2, ys_ref.shape[1]), ys_ref.dtype)

    @pl.when(jnp.logical_not(valid))
    def _():
        ys_ref[...] = jnp.zeros(ys_ref.shape, ys_ref.dtype)


def _experts(xs, tables, w_gate, w_up, w_down, layer):
    n_rows, dp = xs.shape
    d = w_gate.shape[2]
    tm = TM_EXP
    f = w_gate.shape[3]
    nch = W_CHUNKS
    any_spec = pl.BlockSpec(memory_space=pl.ANY)
    return pl.pallas_call(
        functools.partial(_expert_kernel, layer=layer, nch=nch),
        grid_spec=pltpu.PrefetchScalarGridSpec(
            num_scalar_prefetch=8,
            grid=(n_rows // tm,),
            in_specs=[
                pl.BlockSpec((tm, dp), lambda i, *s: (jnp.minimum(i, s[6][0]), 0)),
                any_spec, any_spec, any_spec,
            ],
            out_specs=pl.BlockSpec((tm, dp), lambda i, *s: (i, 0)),
            scratch_shapes=[
                pltpu.VMEM((2, d, f), _BF16), pltpu.VMEM((2, d, f), _BF16), pltpu.VMEM((2, f, d), _BF16),
                pltpu.VMEM((2, d // nch, f), _F32), pltpu.VMEM((2, d // nch, f), _F32),
                pltpu.VMEM((2, f // nch, d), _F32), pltpu.SemaphoreType.DMA((2,)),
            ],
        ),
        out_shape=jax.ShapeDtypeStruct((n_rows, dp), jnp.uint32),
        compiler_params=pltpu.CompilerParams(
            dimension_semantics=("arbitrary",), vmem_limit_bytes=VMEM_LIMIT),
        name="experts",
    )(*tables, xs, w_gate, w_up, w_down)


def _start_row_gather(src_hbm, idx_ref, dst, sem, n_rows):
    for r in range(n_rows):
        _row_copy(src_hbm.at[pl.ds(idx_ref[0, 0, r], 1)], dst.at[pl.ds(r, 1)], sem).start(
            priority=r % 2)


def _wait_row_gather(src_hbm, dst, sem, n_rows):
    _row_copy(src_hbm.at[pl.ds(0, n_rows)], dst, sem).wait()


def _combine_kernel(d_cur, d_nxt, ys_hbm, x1_ref, route_ref, gp_ref, o_ref, rbuf, sem, *,
                    tk):
    i = pl.program_id(0)
    n = pl.num_programs(0)
    slot = i % 2

    @pl.when(i == 0)
    def _():
        _start_row_gather(ys_hbm, d_cur, rbuf.at[0], sem.at[0], TOP_K * tk)

    _wait_row_gather(ys_hbm, rbuf.at[slot], sem.at[slot], TOP_K * tk)

    @pl.when(i + 1 < n)
    def _():
        _start_row_gather(ys_hbm, d_nxt, rbuf.at[1 - slot], sem.at[1 - slot], TOP_K * tk)

    route = route_ref[0]
    lo0, hi0 = _unpack_bf16_pairs(rbuf[slot, 0:tk, :])
    lo1, hi1 = _unpack_bf16_pairs(rbuf[slot, tk:2 * tk, :])
    w0 = route[:, 2:3]
    w1 = route[:, 3:4]
    y = jnp.concatenate([w0 * lo0 + w1 * lo1, w0 * hi0 + w1 * hi1], axis=1)
    o_ref[0] = x1_ref[0] + y * _rms(y) * gp_ref[0]


def _combine(ys, dest_tiles, x1, route, gate_post):
    bsz, t, d = x1.shape
    tk = TM_CMB
    per_seq = t // tk
    n_tiles = bsz * per_seq
    smem_tile = lambda imap: pl.BlockSpec((1, 1, TOP_K * tk), imap, memory_space=pltpu.SMEM)
    return pl.pallas_call(
        functools.partial(_combine_kernel, tk=tk),
        grid=(n_tiles,),
        in_specs=[
            smem_tile(lambda i: (i, 0, 0)),
            smem_tile(lambda i: (jnp.minimum(i + 1, n_tiles - 1), 0, 0)),
            pl.BlockSpec(memory_space=pl.ANY),
            pl.BlockSpec((1, tk, d), lambda i: (i // per_seq, i % per_seq, 0)),
            pl.BlockSpec((1, tk, ROUTE_LANES), lambda i: (i // per_seq, i % per_seq, 0)),
            pl.BlockSpec((1, 1, d), lambda i: (i // per_seq, 0, 0)),
        ],
        out_specs=pl.BlockSpec((1, tk, d), lambda i: (i // per_seq, i % per_seq, 0)),
        out_shape=jax.ShapeDtypeStruct((bsz, t, d), _F32),
        scratch_shapes=[pltpu.VMEM((2, TOP_K * tk, d // 2), jnp.uint32),
                        pltpu.SemaphoreType.DMA((2,))],
        compiler_params=pltpu.CompilerParams(
            dimension_semantics=("arbitrary",), vmem_limit_bytes=VMEM_LIMIT),
        name="combine",
    )(dest_tiles, dest_tiles, ys, x1, route, gate_post)


def _segment_tables(counts, tm, n_tiles, n_steps, nch):
    n_experts = counts.shape[0]
    i32 = jnp.int32
    nt = (counts + tm - 1) // tm
    tend = jnp.cumsum(nt)
    tstart = tend - nt
    tile_idx = jnp.arange(n_tiles, dtype=i32)
    valid = tile_idx < tend[-1]
    nonempty = nt > 0
    seg_of_e = jnp.cumsum(nonempty.astype(i32)) - 1
    n_seg = seg_of_e[-1] + 1
    ids = jnp.arange(n_experts, dtype=i32)
    seg_expert = jnp.sum(
        jnp.where(nonempty[None, :] & (seg_of_e[None, :] == ids[:, None]), ids[None, :], 0), axis=1)
    owns = (tstart[None, :] <= tile_idx[:, None]) & (tile_idx[:, None] < tend[None, :])
    pick = lambda v: jnp.sum(jnp.where(owns, v[None, :], 0), axis=1)
    k = pick(seg_of_e)
    j = tile_idx - pick(tstart)
    ntt = jnp.maximum(pick(nt), 1)
    has_next = valid & (k + 1 < n_seg)
    glo = jnp.where(has_next, (k + 1) * nch + j * nch // ntt, 0)
    ghi = jnp.where(has_next, (k + 1) * nch + (j + 1) * nch // ntt, 0)
    real_rows = jnp.clip(pick(counts) - j * tm, 0, tm)
    half_only = valid & (real_rows <= tm // 2)
    expert_tables = (valid.astype(i32), (k % 2).astype(i32), glo.astype(i32), ghi.astype(i32),
                     seg_expert.astype(i32), (n_seg * nch).reshape(1).astype(i32),
                     (tend[-1:] - 1).astype(i32), half_only.astype(i32))
    pad_start = jnp.zeros((n_steps,), i32).at[:n_experts].set(tstart * tm + counts)
    pad_len = jnp.zeros((n_steps,), i32).at[:n_experts].set(nt * tm - counts)
    return expert_tables, pad_start, pad_len, tend[-1:].astype(i32)


def kernel(x, c, w_ada, b_ada, norm_mix_pre, norm_mix_post, w_in, conv_a, norm_a_out, conv_b, conv_b_bias, ln_b_gain, ln_b_bias, w_out, norm_moe_pre, norm_moe_post, w_router_group, b_router_group, w_router_expert, b_router_expert, w_expert_gate, w_expert_up, w_expert_down):
    bsz, t, d = x.shape
    depth = w_ada.shape[0]
    n_groups = w_router_group.shape[2]
    per_group = w_router_expert.shape[3]
    n_experts = n_groups * per_group
    n_tok = bsz * t
    n_assign = n_tok * TOP_K
    n_tiles = (n_assign + n_experts * (TM_EXP - 1) + TM_EXP - 1) // TM_EXP
    row = lambda v: v.reshape(1, -1)

    for l in range(depth):
        mod = _ada(c, w_ada[l], b_ada[l])
        shift1, scale1, gate1, shift2, scale2, gate2 = [
            m.reshape(bsz, 1, d) for m in jnp.split(mod, N_MOD, axis=-1)]

        y = _mix_in(x, shift1, norm_mix_pre[l] * (1.0 + scale1), w_in[l].astype(_BF16), conv_a[l],
                    row(norm_a_out[l]), conv_b[l], row(conv_b_bias[l]), row(ln_b_gain[l]),
                    row(ln_b_bias[l]))

        n_route = n_groups + n_experts
        w_fine = jnp.transpose(w_router_expert[l], (1, 0, 2)).reshape(d, n_experts)
        w_route = jnp.zeros((d, ROUTE_LANES), _F32).at[:, :n_route].set(
            jnp.concatenate([w_router_group[l], w_fine], axis=1)).astype(_BF16)
        b_route = jnp.zeros((1, ROUTE_LANES), _F32).at[0, :n_route].set(
            jnp.concatenate([b_router_group[l], b_router_expert[l].reshape(-1)]))
        x1, h2, route, route_t = _mix_out(
            y, x, gate1 * norm_mix_post[l], shift2, norm_moe_pre[l] * (1.0 + scale2),
            w_out[l].astype(_BF16), w_route, b_route, n_groups, per_group)

        dest, counts_f = _rank(route_t, TM_EXP)
        counts = counts_f[:n_experts, 0].astype(jnp.int32)
        n_ctiles = n_tok // TM_CMB
        assert n_ctiles >= n_experts
        expert_tables, pad_start, pad_len, used_tiles = _segment_tables(
            counts, TM_EXP, n_tiles, n_ctiles, W_CHUNKS)
        d0_tiles = dest[0].reshape(n_ctiles, 1, TM_CMB)
        d1_tiles = dest[1].reshape(n_ctiles, 1, TM_CMB)
        xs = _scatter(h2.reshape(n_tok, d // 2), d0_tiles, d1_tiles, pad_start, pad_len, used_tiles,
                      n_tiles * TM_EXP)
        ys = _experts(xs, expert_tables, w_expert_gate, w_expert_up, w_expert_down, l)

        dest_tiles = jnp.concatenate([d0_tiles, d1_tiles], axis=2)
        x = _combine(ys, dest_tiles, x1, route, gate2 * norm_moe_post[l])
    return x
```

```python
import functools

import jax
import jax.numpy as jnp
from jax import lax
from jax.experimental import pallas as pl
from jax.experimental.pallas import tpu as pltpu

EPS = 1e-6
HEAD_DIM = 128
KERNEL_A = 3
KERNEL_B = 31
N_MOD = 6
TOP_K = 2
ROUTE_LANES = 128
ROUTE_ROWS = 8
EXPERT_ROWS = 128
TQ_RANK = 1024
HIST_A = 8
HIST_B = 32
CONV_ROWS = 128
MIX_COLS = 256
OUT_PARTS = 2
W_CHUNKS = 8

TM_MIX = 512
TM_EXP = 512
EXP_ROW_GROUP = 128
TM_CMB = 256
ADA_COLS = 1024
VMEM_LIMIT = 56 * 1024 * 1024

_BF16 = jnp.bfloat16
_F32 = jnp.float32


def _dot(a, b):
    return jnp.dot(a, b, preferred_element_type=_F32)


def _rms(v):
    return lax.rsqrt(jnp.mean(v * v, axis=-1, keepdims=True) + EPS)


def _pack_bf16_pairs(v):
    n = v.shape[1] // 2
    bits = pltpu.bitcast(v.astype(_BF16).astype(_F32), jnp.uint32)
    return (bits[:, :n] >> 16) | bits[:, n:]


def _unpack_bf16_pairs(p):
    lo = pltpu.bitcast(p << 16, _F32)
    hi = pltpu.bitcast(p & jnp.uint32(0xFFFF0000), _F32)
    return lo, hi


def _ada_kernel(c_ref, w_ref, b_ref, o_ref):
    c = c_ref[...]
    act = (c * jax.nn.sigmoid(c)).astype(_BF16)
    o_ref[...] = _dot(act, w_ref[...].astype(_BF16)) + b_ref[...]


def _ada(c, w_ada, b_ada):
    bsz, d = c.shape
    n_out = w_ada.shape[1]
    c_pad = jnp.zeros((8, d), _F32).at[:bsz].set(c)
    out = pl.pallas_call(
        _ada_kernel,
        grid=(n_out // ADA_COLS,),
        in_specs=[
            pl.BlockSpec((8, d), lambda j: (0, 0)),
            pl.BlockSpec((d, ADA_COLS), lambda j: (0, j)),
            pl.BlockSpec((1, ADA_COLS), lambda j: (0, j)),
        ],
        out_specs=pl.BlockSpec((8, ADA_COLS), lambda j: (0, j)),
        out_shape=jax.ShapeDtypeStruct((8, n_out), _F32),
        compiler_params=pltpu.CompilerParams(
            dimension_semantics=("arbitrary",), vmem_limit_bytes=VMEM_LIMIT),
        name="ada",
    )(c_pad, w_ada, b_ada.reshape(1, n_out))
    return out[:bsz]


def _dwconv_chunk(ext_ref, w_ref, r0, rows, lo, hi, hist, ksize):
    span = rows + hist
    first = hist - (ksize - 1)
    xe = ext_ref[r0:r0 + span, lo:hi]
    acc = None
    for b in range(8):
        offsets = [o for o in range(first, hist + 1) if o % 8 == b]
        if not offsets:
            continue
        shifted = xe if b == 0 else pltpu.roll(xe, span - b, 0)
        for o in offsets:
            k = o - first
            term = w_ref[k:k + 1, lo:hi] * shifted[o - b:o - b + rows, :]
            acc = term if acc is None else acc + term
    return acc


def _mix_in_kernel(x_ref, sh_ref, gs_ref, w_ref, ca_ref, na_ref, cb_ref, cbb_ref,
                   lg_ref, lb_ref, y_ref, cv_ext, u_ext, ba_buf, conv_b_out, *, tm, da, db):
    nc = MIX_COLS
    @pl.when(pl.program_id(1) == 0)
    def _():
        cv_ext[...] = jnp.zeros(cv_ext.shape, _F32)
        u_ext[0:HIST_B, :] = jnp.zeros((HIST_B, db), _F32)
        ba_buf[...] = jnp.zeros(ba_buf.shape, _F32)
        conv_b_out[...] = jnp.zeros(conv_b_out.shape, _F32)

    x = x_ref[0]
    h = x * _rms(x) * gs_ref[0] + sh_ref[0]
    hb = h.astype(_BF16)

    def finish_head(hd):
        lo, hi = hd * HEAD_DIM, (hd + 1) * HEAD_DIM
        for r0 in range(0, tm, CONV_ROWS):
            conv = _dwconv_chunk(cv_ext, ca_ref, r0, CONV_ROWS, lo, hi, HIST_A, KERNEL_A)
            seg = ba_buf[r0:r0 + CONV_ROWS, lo:hi] * conv
            y_ref[0, r0:r0 + CONV_ROWS, lo:hi] = (seg * _rms(seg) * na_ref[:, lo:hi]).astype(_BF16)

    def finish_rows(r0, rows):
        acc = conv_b_out[r0:r0 + rows, :]
        mu = jnp.mean(acc, axis=-1, keepdims=True)
        cen = acc - mu
        var = jnp.mean(cen * cen, axis=-1, keepdims=True)
        yn = cen * lax.rsqrt(var + EPS) * lg_ref[...] + lb_ref[...]
        y_ref[0, r0:r0 + rows, da:da + db] = (yn * jax.nn.sigmoid(yn)).astype(_BF16)

    n_b = db // nc
    heads_per = (da // HEAD_DIM) // n_b
    ln_rows = tm // n_b
    for j in range(n_b):
        lo = j * nc
        val = _dot(hb, w_ref[:, 3 * da + lo:3 * da + lo + nc])
        gate = _dot(hb, w_ref[:, 3 * da + db + lo:3 * da + db + lo + nc])
        for hd in range(j * heads_per, (j + 1) * heads_per):
            finish_head(hd)
        finish_rows(j * ln_rows, ln_rows)
        u_ext[HIST_B:HIST_B + tm, lo:lo + nc] = val * jax.nn.sigmoid(gate)
    cv_ext[0:HIST_A, :] = cv_ext[tm:tm + HIST_A, :]

    conv_chunks = [(lo, r0) for lo in range(0, db, HEAD_DIM) for r0 in range(0, tm, CONV_ROWS)]
    n_a = da // nc
    n_mm = 3 * n_a
    done = 0
    for q in range(n_mm):
        j = q // 2 if q < 2 * n_a else q - 2 * n_a
        lo = j * nc
        if q < 2 * n_a and q % 2 == 0:
            c_part = _dot(hb, w_ref[:, da + lo:da + lo + nc])
        elif q < 2 * n_a:
            v_part = _dot(hb, w_ref[:, 2 * da + lo:2 * da + lo + nc])
            cv_ext[HIST_A:HIST_A + tm, lo:lo + nc] = c_part * v_part
        else:
            ba_buf[:, lo:lo + nc] = _dot(hb, w_ref[:, lo:lo + nc])
        upto = (q + 1) * len(conv_chunks) // n_mm
        for lo_c, r0 in conv_chunks[done:upto]:
            conv = _dwconv_chunk(u_ext, cb_ref, r0, CONV_ROWS, lo_c, lo_c + HEAD_DIM, HIST_B, KERNEL_B)
            conv_b_out[r0:r0 + CONV_ROWS, lo_c:lo_c + HEAD_DIM] = conv + cbb_ref[:, lo_c:lo_c + HEAD_DIM]
        done = upto
    u_ext[0:HIST_B, :] = u_ext[tm:tm + HIST_B, :]


def _mix_in(x, shift, gain_scale, w_in_bf, conv_a, norm_a, conv_b, conv_b_bias, ln_g, ln_b):
    bsz, t, d = x.shape
    da = conv_a.shape[1]
    db = conv_b.shape[1]
    tm = TM_MIX
    full = lambda shape: pl.BlockSpec(shape, lambda b, s: (0,) * len(shape))
    per_b = pl.BlockSpec((1, 1, d), lambda b, s: (b, 0, 0))
    n_t = t // tm
    return pl.pallas_call(
        functools.partial(_mix_in_kernel, tm=tm, da=da, db=db),
        grid=(bsz, n_t + 1),
        in_specs=[
            pl.BlockSpec((1, tm, d), lambda b, s: (b, jnp.minimum(s, n_t - 1), 0)),
            per_b, per_b,
            pl.BlockSpec(w_in_bf.shape, lambda b, s: (0, 0), pipeline_mode=pl.Buffered(1)),
            full((KERNEL_A, da)), full((1, da)),
            full((KERNEL_B, db)), full((1, db)), full((1, db)), full((1, db)),
        ],
        out_specs=pl.BlockSpec((1, tm, da + db), lambda b, s: (b, jnp.maximum(s - 1, 0), 0)),
        out_shape=jax.ShapeDtypeStruct((bsz, t, da + db), _BF16),
        scratch_shapes=[pltpu.VMEM((HIST_A + tm, da), _F32), pltpu.VMEM((HIST_B + tm, db), _F32),
                        pltpu.VMEM((tm, da), _F32), pltpu.VMEM((tm, db), _F32)],
        compiler_params=pltpu.CompilerParams(
            dimension_semantics=("arbitrary", "arbitrary"), vmem_limit_bytes=VMEM_LIMIT),
        name="mix_in",
    )(x, shift, gain_scale, w_in_bf, conv_a, norm_a, conv_b, conv_b_bias, ln_g, ln_b)


def _route_tile(hb, wr, br, n_groups, per_group):
    logits = _dot(hb, wr) + br
    lane = lax.broadcasted_iota(jnp.int32, logits.shape, 1).astype(_F32)
    neg = jnp.float32(-jnp.inf)
    big = jnp.float32(ROUTE_LANES)
    is_coarse = lane < n_groups
    coarse = jnp.where(is_coarse, logits, neg)
    m = jnp.max(coarse, axis=-1, keepdims=True)
    g_sel = jnp.min(jnp.where(coarse == m, lane, big), axis=-1, keepdims=True)
    p_g = 1.0 / jnp.sum(jnp.where(is_coarse, jnp.exp(logits - m), 0.0), axis=-1, keepdims=True)
    lo = n_groups + g_sel * per_group
    fine = jnp.where((lane >= lo) & (lane < lo + per_group), logits, neg)
    v1 = jnp.max(fine, axis=-1, keepdims=True)
    l1 = jnp.min(jnp.where(fine == v1, lane, big), axis=-1, keepdims=True)
    fine2 = jnp.where(lane == l1, neg, fine)
    v2 = jnp.max(fine2, axis=-1, keepdims=True)
    l2 = jnp.min(jnp.where(fine2 == v2, lane, big), axis=-1, keepdims=True)
    e2 = jnp.exp(v2 - v1)
    w1 = p_g / (1.0 + e2)
    w2 = p_g * e2 / (1.0 + e2)
    route = jnp.where(lane == 0, l1 - n_groups,
                      jnp.where(lane == 1, l2 - n_groups,
                                jnp.where(lane == 2, w1, jnp.where(lane == 3, w2, 0.0))))
    return route


def _mix_out_kernel(y_ref, x_ref, gp_ref, sh2_ref, gs_ref, wout_ref, wr_ref, br_ref,
                    x1_ref, h2_ref, route_ref, route_t_ref, o_a, o_b, hb_buf, *, tm, n_groups,
                    per_group):
    part = tm // OUT_PARTS
    nc = MIX_COLS
    n_chunks = wout_ref.shape[1] // nc
    rows = part // n_chunks
    bufs = (o_a, o_b)

    def finish_rows(o_part, base, q):
        r0 = q * rows
        o = o_part[r0:r0 + rows, :]
        x1 = x_ref[0, base + r0:base + r0 + rows, :] + o * _rms(o) * gp_ref[0]
        x1_ref[0, base + r0:base + r0 + rows, :] = x1
        h2 = x1 * _rms(x1) * gs_ref[0] + sh2_ref[0]
        h2_ref[0, base + r0:base + r0 + rows, :] = _pack_bf16_pairs(h2)
        hb_buf[base + r0:base + r0 + rows, :] = h2.astype(_BF16)

    for k in range(OUT_PARTS + 1):
        y_part = y_ref[0, k * part:(k + 1) * part, :] if k < OUT_PARTS else None
        for q in range(n_chunks):
            if k < OUT_PARTS:
                bufs[k % 2][:, q * nc:(q + 1) * nc] = _dot(y_part, wout_ref[:, q * nc:(q + 1) * nc])
            if k > 0:
                finish_rows(bufs[(k - 1) % 2], (k - 1) * part, q)
    route = _route_tile(hb_buf[...], wr_ref[...], br_ref[...], n_groups, per_group)
    route_ref[0] = route
    route_t_ref[...] = jnp.transpose(route)[0:ROUTE_ROWS, :]


def _mix_out(y, x, gate_post, shift2, gain_scale, w_out_bf, w_route, b_route, n_groups, per_group):
    bsz, t, d = x.shape
    tm = TM_MIX
    full = lambda shape: pl.BlockSpec(shape, lambda b, s: (0,) * len(shape))
    per_b = pl.BlockSpec((1, 1, d), lambda b, s: (b, 0, 0))
    tile = pl.BlockSpec((1, tm, d), lambda b, s: (b, s, 0))
    return pl.pallas_call(
        functools.partial(_mix_out_kernel, tm=tm, n_groups=n_groups, per_group=per_group),
        grid=(bsz, t // tm),
        in_specs=[
            tile, tile, per_b, per_b, per_b,
            pl.BlockSpec(w_out_bf.shape, lambda b, s: (0, 0), pipeline_mode=pl.Buffered(1)),
            full((d, ROUTE_LANES)), full((1, ROUTE_LANES)),
        ],
        out_specs=[tile, pl.BlockSpec((1, tm, d // 2), lambda b, s: (b, s, 0)),
                   pl.BlockSpec((1, tm, ROUTE_LANES), lambda b, s: (b, s, 0)),
                   pl.BlockSpec((ROUTE_ROWS, tm), lambda b, s: (0, b * (t // tm) + s))],
        out_shape=[
            jax.ShapeDtypeStruct((bsz, t, d), _F32),
            jax.ShapeDtypeStruct((bsz, t, d // 2), jnp.uint32),
            jax.ShapeDtypeStruct((bsz, t, ROUTE_LANES), _F32),
            jax.ShapeDtypeStruct((ROUTE_ROWS, bsz * t), _F32),
        ],
        scratch_shapes=[pltpu.VMEM((tm // OUT_PARTS, d), _F32), pltpu.VMEM((tm // OUT_PARTS, d), _F32),
                        pltpu.VMEM((tm, d), _BF16)],
        compiler_params=pltpu.CompilerParams(
            dimension_semantics=("arbitrary", "arbitrary"), vmem_limit_bytes=VMEM_LIMIT),
        name="mix_out",
    )(y, x, gate_post, shift2, gain_scale, w_out_bf, w_route, b_route)


def _rank_kernel(rt_ref, u_ref, dest_ref, cnt_ref, carry, pbase, *, tq, tm):
    phase = pl.program_id(0)
    j = pl.program_id(1)
    sub = lax.broadcasted_iota(jnp.int32, (EXPERT_ROWS, tq), 0).astype(_F32)
    oh0 = sub == rt_ref[0:1, :]
    oh1 = sub == rt_ref[1:2, :]
    both = jnp.where(oh0, 1.0, 0.0) + jnp.where(oh1, 1.0, 0.0)
    tile_counts = jnp.sum(both, axis=1, keepdims=True)

    @pl.when(jnp.logical_and(phase == 0, j == 0))
    def _():
        carry[...] = jnp.zeros(carry.shape, _F32)

    @pl.when(jnp.logical_and(phase == 1, j == 0))
    def _():
        cnt = carry[...]
        cnt_ref[...] = cnt
        tiles = jnp.floor((cnt + (tm - 1)) * (1.0 / tm))
        hi = jnp.floor(tiles * (1.0 / 16.0))
        lo = tiles - 16.0 * hi
        r = lax.broadcasted_iota(jnp.int32, (EXPERT_ROWS, EXPERT_ROWS), 0)
        c = lax.broadcasted_iota(jnp.int32, (EXPERT_ROWS, EXPERT_ROWS), 1)
        before = jnp.where(c < r, 1.0, 0.0).astype(_BF16)
        start_tiles = 16.0 * _dot(before, hi.astype(_BF16)) + _dot(before, lo.astype(_BF16))
        pbase[...] = start_tiles * tm
        carry[...] = jnp.zeros(carry.shape, _F32)

    @pl.when(phase == 1)
    def _():
        earlier = _dot(both.astype(_BF16), u_ref[...])
        slot = earlier + carry[:, 0:1] + pbase[:, 0:1]
        d0 = jnp.sum(jnp.where(oh0, slot, 0.0), axis=0, keepdims=True)
        d1 = jnp.sum(jnp.where(oh1, slot, 0.0), axis=0, keepdims=True)
        row = lax.broadcasted_iota(jnp.int32, (ROUTE_ROWS, tq), 0)
        dest_ref[...] = jnp.where(row == 0, d0, jnp.where(row == 1, d1, 0.0)).astype(jnp.int32)

    carry[...] = carry[...] + tile_counts


def _rank(route_t, tm):
    n_tok = route_t.shape[1]
    tq = TQ_RANK
    assert tm & (tm - 1) == 0, "segment padding must be a power of two for exact f32 arithmetic"
    idx = jnp.arange(tq, dtype=jnp.int32)
    earlier_mask = (idx[:, None] < idx[None, :]).astype(_BF16)
    return pl.pallas_call(
        functools.partial(_rank_kernel, tq=tq, tm=tm),
        grid=(2, n_tok // tq),
        in_specs=[
            pl.BlockSpec((ROUTE_ROWS, tq), lambda p, j: (0, j)),
            pl.BlockSpec((tq, tq), lambda p, j: (0, 0)),
        ],
        out_specs=[pl.BlockSpec((ROUTE_ROWS, tq), lambda p, j: (0, j * p)),
                   pl.BlockSpec((EXPERT_ROWS, ROUTE_LANES), lambda p, j: (0, 0))],
        out_shape=[jax.ShapeDtypeStruct((ROUTE_ROWS, n_tok), jnp.int32),
                   jax.ShapeDtypeStruct((EXPERT_ROWS, ROUTE_LANES), _F32)],
        scratch_shapes=[pltpu.VMEM((EXPERT_ROWS, ROUTE_LANES), _F32),
                        pltpu.VMEM((EXPERT_ROWS, ROUTE_LANES), _F32)],
        compiler_params=pltpu.CompilerParams(
            dimension_semantics=("arbitrary", "arbitrary"), vmem_limit_bytes=VMEM_LIMIT),
        name="rank",
    )(route_t, earlier_mask)


def _row_copy(src, dst, sem):
    return pltpu.make_async_copy(src, dst, sem)


def _scatter_kernel(ps_ref, pn_ref, used_ref, d0_ref, d1_ref, h_ref, xs_hbm, zrow, sem, psem, *, tk,
                    tm):
    i = pl.program_id(0)

    @pl.when(i == 0)
    def _():
        zrow[...] = jnp.zeros(zrow.shape, zrow.dtype)

    @pl.when(i == pl.num_programs(0) - 1)
    def _():
        n_rows = xs_hbm.shape[0]

        def fill(t, carry):
            dst = xs_hbm.at[pl.ds(pl.multiple_of(t * tm, tm), tm)]
            _row_copy(zrow, dst, psem).start()
            _row_copy(zrow, dst, psem).wait()
            return carry

        lax.fori_loop(used_ref[0], n_rows // tm, fill, 0)

    for r in range(tk):
        src = h_ref.at[pl.ds(r, 1)]
        _row_copy(src, xs_hbm.at[pl.ds(d0_ref[0, 0, r], 1)], sem).start(priority=0)
        _row_copy(src, xs_hbm.at[pl.ds(d1_ref[0, 0, r], 1)], sem).start(priority=1)

    pad_start = ps_ref[i]
    n_pad = pn_ref[i]
    zsrc = zrow.at[pl.ds(0, 1)]

    def start_pad(r, carry):
        _row_copy(zsrc, xs_hbm.at[pl.ds(pad_start + r, 1)], psem).start()
        return carry

    def wait_pad(r, carry):
        _row_copy(zsrc, xs_hbm.at[pl.ds(0, 1)], psem).wait()
        return carry

    lax.fori_loop(0, n_pad, start_pad, 0)
    lax.fori_loop(0, n_pad, wait_pad, 0)
    for _ in range(TOP_K):
        _row_copy(h_ref, xs_hbm.at[pl.ds(0, tk)], sem).wait()


def _scatter(h2_flat, d0_tiles, d1_tiles, pad_start, pad_len, used_tiles, n_rows):
    n_tok, d = h2_flat.shape
    tk = TM_CMB
    tm = TM_EXP
    smem_tile = pl.BlockSpec((1, 1, tk), lambda i, ps, pn, used: (i, 0, 0), memory_space=pltpu.SMEM)
    return pl.pallas_call(
        functools.partial(_scatter_kernel, tk=tk, tm=tm),
        grid_spec=pltpu.PrefetchScalarGridSpec(
            num_scalar_prefetch=3,
            grid=(n_tok // tk,),
            in_specs=[smem_tile, smem_tile, pl.BlockSpec((tk, d), lambda i, ps, pn, used: (i, 0))],
            out_specs=pl.BlockSpec(memory_space=pl.ANY),
            scratch_shapes=[pltpu.VMEM((tm, d), h2_flat.dtype), pltpu.SemaphoreType.DMA(()),
                            pltpu.SemaphoreType.DMA(())],
        ),
        out_shape=jax.ShapeDtypeStruct((n_rows, d), h2_flat.dtype),
        compiler_params=pltpu.CompilerParams(
            dimension_semantics=("arbitrary",), vmem_limit_bytes=VMEM_LIMIT),
        name="scatter",
    )(pad_start, pad_len, used_tiles, d0_tiles, d1_tiles, h2_flat)


def _expert_kernel(tv_ref, par_ref, glo_ref, ghi_ref, sege_ref, ntot_ref, last_ref, grp_ref,
                   xs_ref, wg_hbm, wu_hbm, wd_hbm, ys_ref,
                   wgb, wub, wdb, sg, su, sd, sem, *, layer, nch):
    del last_ref
    i = pl.program_id(0)
    rg = wgb.shape[1] // nch
    rd = wdb.shape[1] // nch

    def chunk_copies(g):
        e = sege_ref[g // nch]
        c = g % nch
        s = g % 2
        rows_g = pl.ds(pl.multiple_of(c * rg, rg), rg)
        rows_d = pl.ds(pl.multiple_of(c * rd, rd), rd)
        return (_row_copy(wg_hbm.at[layer, e, rows_g], sg.at[s], sem.at[s]),
                _row_copy(wu_hbm.at[layer, e, rows_g], su.at[s], sem.at[s]),
                _row_copy(wd_hbm.at[layer, e, rows_d], sd.at[s], sem.at[s]))

    def start_chunk(g):
        for cp in chunk_copies(g):
            cp.start()

    def process_chunk(g, carry):
        for cp in chunk_copies(g):
            cp.wait()
        c = g % nch
        s = g % 2
        p = (g // nch) % 2
        wgb[p, pl.ds(pl.multiple_of(c * rg, rg), rg), :] = sg[s].astype(_BF16)
        wub[p, pl.ds(pl.multiple_of(c * rg, rg), rg), :] = su[s].astype(_BF16)
        wdb[p, pl.ds(pl.multiple_of(c * rd, rd), rd), :] = sd[s].astype(_BF16)

        @pl.when(g + 2 < ntot_ref[0])
        def _():
            start_chunk(g + 2)
        return carry

    @pl.when(i == 0)
    def _():
        start_chunk(0)
        start_chunk(1)
        lax.fori_loop(0, nch, process_chunk, 0)

    valid = tv_ref[i] == 1

    def tile_mlp(rows):
        p = par_ref[i]
        lo, hi = _unpack_bf16_pairs(xs_ref[0:rows, :])
        xb = jnp.concatenate([lo.astype(_BF16), hi.astype(_BF16)], axis=1)
        g = _dot(xb, wgb[p])
        u = _dot(xb, wub[p])
        hid = (g * jax.nn.sigmoid(g) * u).astype(_BF16)
        ys_ref[0:rows, :] = _pack_bf16_pairs(_dot(hid, wdb[p]))

    @pl.when(valid)
    def _():
        lax.fori_loop(glo_ref[i], ghi_ref[i], process_chunk, 0)
        tm = ys_ref.shape[0]
        for n_grp in range(1, tm // EXP_ROW_GROUP + 1):
            rows = n_grp * EXP_ROW_GROUP

            @pl.when(grp_ref[i] == n_grp)
            def _(rows=rows):
                tile_mlp(rows)
                if rows < tm:
                    ys_ref[rows:tm, :] = jnp.zeros((tm - rows, ys_ref.shape[1]), ys_ref.dtype)

    @pl.when(jnp.logical_not(valid))
    def _():
        ys_ref[...] = jnp.zeros(ys_ref.shape, ys_ref.dtype)


def _experts(xs, tables, w_gate, w_up, w_down, layer):
    n_rows, dp = xs.shape
    d = w_gate.shape[2]
    tm = TM_EXP
    f = w_gate.shape[3]
    nch = W_CHUNKS
    any_spec = pl.BlockSpec(memory_space=pl.ANY)
    return pl.pallas_call(
        functools.partial(_expert_kernel, layer=layer, nch=nch),
        grid_spec=pltpu.PrefetchScalarGridSpec(
            num_scalar_prefetch=8,
            grid=(n_rows // tm,),
            in_specs=[
                pl.BlockSpec((tm, dp), lambda i, *s: (jnp.minimum(i, s[6][0]), 0)),
                any_spec, any_spec, any_spec,
            ],
            out_specs=pl.BlockSpec((tm, dp), lambda i, *s: (i, 0)),
            scratch_shapes=[
                pltpu.VMEM((2, d, f), _BF16), pltpu.VMEM((2, d, f), _BF16), pltpu.VMEM((2, f, d), _BF16),
                pltpu.VMEM((2, d // nch, f), _F32), pltpu.VMEM((2, d // nch, f), _F32),
                pltpu.VMEM((2, f // nch, d), _F32), pltpu.SemaphoreType.DMA((2,)),
            ],
        ),
        out_shape=jax.ShapeDtypeStruct((n_rows, dp), jnp.uint32),
        compiler_params=pltpu.CompilerParams(
            dimension_semantics=("arbitrary",), vmem_limit_bytes=VMEM_LIMIT),
        name="experts",
    )(*tables, xs, w_gate, w_up, w_down)


def _start_row_gather(src_hbm, idx_ref, dst, sem, n_rows):
    for r in range(n_rows):
        _row_copy(src_hbm.at[pl.ds(idx_ref[0, 0, r], 1)], dst.at[pl.ds(r, 1)], sem).start(
            priority=r % 2)


def _wait_row_gather(src_hbm, dst, sem, n_rows):
    _row_copy(src_hbm.at[pl.ds(0, n_rows)], dst, sem).wait()


def _combine_kernel(d_cur, d_nxt, ys_hbm, x1_ref, route_ref, gp_ref, o_ref, rbuf, sem, *,
                    tk):
    i = pl.program_id(0)
    n = pl.num_programs(0)
    slot = i % 2

    @pl.when(i == 0)
    def _():
        _start_row_gather(ys_hbm, d_cur, rbuf.at[0], sem.at[0], TOP_K * tk)

    _wait_row_gather(ys_hbm, rbuf.at[slot], sem.at[slot], TOP_K * tk)

    @pl.when(i + 1 < n)
    def _():
        _start_row_gather(ys_hbm, d_nxt, rbuf.at[1 - slot], sem.at[1 - slot], TOP_K * tk)

    route = route_ref[0]
    lo0, hi0 = _unpack_bf16_pairs(rbuf[slot, 0:tk, :])
    lo1, hi1 = _unpack_bf16_pairs(rbuf[slot, tk:2 * tk, :])
    w0 = route[:, 2:3]
    w1 = route[:, 3:4]
    y = jnp.concatenate([w0 * lo0 + w1 * lo1, w0 * hi0 + w1 * hi1], axis=1)
    o_ref[0] = x1_ref[0] + y * _rms(y) * gp_ref[0]


def _combine(ys, dest_tiles, x1, route, gate_post):
    bsz, t, d = x1.shape
    tk = TM_CMB
    per_seq = t // tk
    n_tiles = bsz * per_seq
    smem_tile = lambda imap: pl.BlockSpec((1, 1, TOP_K * tk), imap, memory_space=pltpu.SMEM)
    return pl.pallas_call(
        functools.partial(_combine_kernel, tk=tk),
        grid=(n_tiles,),
        in_specs=[
            smem_tile(lambda i: (i, 0, 0)),
            smem_tile(lambda i: (jnp.minimum(i + 1, n_tiles - 1), 0, 0)),
            pl.BlockSpec(memory_space=pl.ANY),
            pl.BlockSpec((1, tk, d), lambda i: (i // per_seq, i % per_seq, 0)),
            pl.BlockSpec((1, tk, ROUTE_LANES), lambda i: (i // per_seq, i % per_seq, 0)),
            pl.BlockSpec((1, 1, d), lambda i: (i // per_seq, 0, 0)),
        ],
        out_specs=pl.BlockSpec((1, tk, d), lambda i: (i // per_seq, i % per_seq, 0)),
        out_shape=jax.ShapeDtypeStruct((bsz, t, d), _F32),
        scratch_shapes=[pltpu.VMEM((2, TOP_K * tk, d // 2), jnp.uint32),
                        pltpu.SemaphoreType.DMA((2,))],
        compiler_params=pltpu.CompilerParams(
            dimension_semantics=("arbitrary",), vmem_limit_bytes=VMEM_LIMIT),
        name="combine",
    )(dest_tiles, dest_tiles, ys, x1, route, gate_post)


def _segment_tables(counts, tm, n_tiles, n_steps, nch):
    n_experts = counts.shape[0]
    i32 = jnp.int32
    nt = (counts + tm - 1) // tm
    tend = jnp.cumsum(nt)
    tstart = tend - nt
    tile_idx = jnp.arange(n_tiles, dtype=i32)
    valid = tile_idx < tend[-1]
    nonempty = nt > 0
    seg_of_e = jnp.cumsum(nonempty.astype(i32)) - 1
    n_seg = seg_of_e[-1] + 1
    ids = jnp.arange(n_experts, dtype=i32)
    seg_expert = jnp.sum(
        jnp.where(nonempty[None, :] & (seg_of_e[None, :] == ids[:, None]), ids[None, :], 0), axis=1)
    owns = (tstart[None, :] <= tile_idx[:, None]) & (tile_idx[:, None] < tend[None, :])
    pick = lambda v: jnp.sum(jnp.where(owns, v[None, :], 0), axis=1)
    k = pick(seg_of_e)
    j = tile_idx - pick(tstart)
    ntt = jnp.maximum(pick(nt), 1)
    has_next = valid & (k + 1 < n_seg)
    glo = jnp.where(has_next, (k + 1) * nch + j * nch // ntt, 0)
    ghi = jnp.where(has_next, (k + 1) * nch + (j + 1) * nch // ntt, 0)
    real_rows = jnp.clip(pick(counts) - j * tm, 0, tm)
    row_groups = (real_rows + EXP_ROW_GROUP - 1) // EXP_ROW_GROUP
    expert_tables = (valid.astype(i32), (k % 2).astype(i32), glo.astype(i32), ghi.astype(i32),
                     seg_expert.astype(i32), (n_seg * nch).reshape(1).astype(i32),
                     (tend[-1:] - 1).astype(i32), row_groups.astype(i32))
    pad_start = jnp.zeros((n_steps,), i32).at[:n_experts].set(tstart * tm + counts)
    pad_len = jnp.zeros((n_steps,), i32).at[:n_experts].set(nt * tm - counts)
    return expert_tables, pad_start, pad_len, tend[-1:].astype(i32)


def kernel(x, c, w_ada, b_ada, norm_mix_pre, norm_mix_post, w_in, conv_a, norm_a_out, conv_b, conv_b_bias, ln_b_gain, ln_b_bias, w_out, norm_moe_pre, norm_moe_post, w_router_group, b_router_group, w_router_expert, b_router_expert, w_expert_gate, w_expert_up, w_expert_down):
    bsz, t, d = x.shape
    depth = w_ada.shape[0]
    n_groups = w_router_group.shape[2]
    per_group = w_router_expert.shape[3]
    n_experts = n_groups * per_group
    n_tok = bsz * t
    n_assign = n_tok * TOP_K
    n_tiles = (n_assign + n_experts * (TM_EXP - 1) + TM_EXP - 1) // TM_EXP
    row = lambda v: v.reshape(1, -1)

    for l in range(depth):
        mod = _ada(c, w_ada[l], b_ada[l])
        shift1, scale1, gate1, shift2, scale2, gate2 = [
            m.reshape(bsz, 1, d) for m in jnp.split(mod, N_MOD, axis=-1)]

        y = _mix_in(x, shift1, norm_mix_pre[l] * (1.0 + scale1), w_in[l].astype(_BF16), conv_a[l],
                    row(norm_a_out[l]), conv_b[l], row(conv_b_bias[l]), row(ln_b_gain[l]),
                    row(ln_b_bias[l]))

        n_route = n_groups + n_experts
        w_fine = jnp.transpose(w_router_expert[l], (1, 0, 2)).reshape(d, n_experts)
        w_route = jnp.zeros((d, ROUTE_LANES), _F32).at[:, :n_route].set(
            jnp.concatenate([w_router_group[l], w_fine], axis=1)).astype(_BF16)
        b_route = jnp.zeros((1, ROUTE_LANES), _F32).at[0, :n_route].set(
            jnp.concatenate([b_router_group[l], b_router_expert[l].reshape(-1)]))
        x1, h2, route, route_t = _mix_out(
            y, x, gate1 * norm_mix_post[l], shift2, norm_moe_pre[l] * (1.0 + scale2),
            w_out[l].astype(_BF16), w_route, b_route, n_groups, per_group)

        dest, counts_f = _rank(route_t, TM_EXP)
        counts = counts_f[:n_experts, 0].astype(jnp.int32)
        n_ctiles = n_tok // TM_CMB
        assert n_ctiles >= n_experts
        expert_tables, pad_start, pad_len, used_tiles = _segment_tables(
            counts, TM_EXP, n_tiles, n_ctiles, W_CHUNKS)
        d0_tiles = dest[0].reshape(n_ctiles, 1, TM_CMB)
        d1_tiles = dest[1].reshape(n_ctiles, 1, TM_CMB)
        xs = _scatter(h2.reshape(n_tok, d // 2), d0_tiles, d1_tiles, pad_start, pad_len, used_tiles,
                      n_tiles * TM_EXP)
        ys = _experts(xs, expert_tables, w_expert_gate, w_expert_up, w_expert_down, l)

        dest_tiles = jnp.concatenate([d0_tiles, d1_tiles], axis=2)
        x = _combine(ys, dest_tiles, x1, route, gate2 * norm_moe_post[l])
    return x
```

```python
import functools

import jax
import jax.numpy as jnp
from jax import lax
from jax.experimental import pallas as pl
from jax.experimental.pallas import tpu as pltpu

EPS = 1e-6
HEAD_DIM = 128
KERNEL_A = 3
KERNEL_B = 31
N_MOD = 6
TOP_K = 2
ROUTE_LANES = 128
ROUTE_ROWS = 8
EXPERT_ROWS = 128
TQ_RANK = 1024
HIST_A = 8
HIST_B = 32
CONV_ROWS = 128
MIX_COLS = 256
OUT_PARTS = 4
W_CHUNKS = 8

TM_MIX = 512
TM_EXP = 256
TM_CMB = 256
ADA_COLS = 1024
VMEM_LIMIT = 56 * 1024 * 1024

_BF16 = jnp.bfloat16
_F32 = jnp.float32


def _dot(a, b):
    return jnp.dot(a, b, preferred_element_type=_F32)


def _rms(v):
    return lax.rsqrt(jnp.mean(v * v, axis=-1, keepdims=True) + EPS)


def _pack_bf16_pairs(v):
    n = v.shape[1] // 2
    bits = pltpu.bitcast(v.astype(_BF16).astype(_F32), jnp.uint32)
    return (bits[:, :n] >> 16) | bits[:, n:]


def _unpack_bf16_pairs(p):
    lo = pltpu.bitcast(p << 16, _F32)
    hi = pltpu.bitcast(p & jnp.uint32(0xFFFF0000), _F32)
    return lo, hi


def _ada_kernel(c_ref, w_ref, b_ref, o_ref):
    c = c_ref[...]
    act = (c * jax.nn.sigmoid(c)).astype(_BF16)
    o_ref[...] = _dot(act, w_ref[...].astype(_BF16)) + b_ref[...]


def _ada(c, w_ada, b_ada):
    bsz, d = c.shape
    n_out = w_ada.shape[1]
    c_pad = jnp.zeros((8, d), _F32).at[:bsz].set(c)
    out = pl.pallas_call(
        _ada_kernel,
        grid=(n_out // ADA_COLS,),
        in_specs=[
            pl.BlockSpec((8, d), lambda j: (0, 0)),
            pl.BlockSpec((d, ADA_COLS), lambda j: (0, j)),
            pl.BlockSpec((1, ADA_COLS), lambda j: (0, j)),
        ],
        out_specs=pl.BlockSpec((8, ADA_COLS), lambda j: (0, j)),
        out_shape=jax.ShapeDtypeStruct((8, n_out), _F32),
        compiler_params=pltpu.CompilerParams(
            dimension_semantics=("arbitrary",), vmem_limit_bytes=VMEM_LIMIT),
        name="ada",
    )(c_pad, w_ada, b_ada.reshape(1, n_out))
    return out[:bsz]


def _dwconv_chunk(ext_ref, w_ref, r0, rows, lo, hi, hist, ksize):
    span = rows + hist
    first = hist - (ksize - 1)
    xe = ext_ref[r0:r0 + span, lo:hi]
    acc = None
    for b in range(8):
        offsets = [o for o in range(first, hist + 1) if o % 8 == b]
        if not offsets:
            continue
        shifted = xe if b == 0 else pltpu.roll(xe, span - b, 0)
        for o in offsets:
            k = o - first
            term = w_ref[k:k + 1, lo:hi] * shifted[o - b:o - b + rows, :]
            acc = term if acc is None else acc + term
    return acc


def _mix_in_kernel(x_ref, sh_ref, gs_ref, w_ref, ca_ref, na_ref, cb_ref, cbb_ref,
                   lg_ref, lb_ref, y_ref, cv_ext, u_ext, ba_buf, conv_b_out, *, tm, da, db):
    nc = MIX_COLS
    @pl.when(pl.program_id(1) == 0)
    def _():
        cv_ext[...] = jnp.zeros(cv_ext.shape, _F32)
        u_ext[0:HIST_B, :] = jnp.zeros((HIST_B, db), _F32)
        ba_buf[...] = jnp.zeros(ba_buf.shape, _F32)
        conv_b_out[...] = jnp.zeros(conv_b_out.shape, _F32)

    x = x_ref[0]
    h = x * _rms(x) * gs_ref[0] + sh_ref[0]
    hb = h.astype(_BF16)

    def finish_head(hd):
        lo, hi = hd * HEAD_DIM, (hd + 1) * HEAD_DIM
        for r0 in range(0, tm, CONV_ROWS):
            conv = _dwconv_chunk(cv_ext, ca_ref, r0, CONV_ROWS, lo, hi, HIST_A, KERNEL_A)
            seg = ba_buf[r0:r0 + CONV_ROWS, lo:hi] * conv
            y_ref[0, r0:r0 + CONV_ROWS, lo:hi] = (seg * _rms(seg) * na_ref[:, lo:hi]).astype(_BF16)

    def finish_rows(r0, rows):
        acc = conv_b_out[r0:r0 + rows, :]
        mu = jnp.mean(acc, axis=-1, keepdims=True)
        cen = acc - mu
        var = jnp.mean(cen * cen, axis=-1, keepdims=True)
        yn = cen * lax.rsqrt(var + EPS) * lg_ref[...] + lb_ref[...]
        y_ref[0, r0:r0 + rows, da:da + db] = (yn * jax.nn.sigmoid(yn)).astype(_BF16)

    n_b = db // nc
    heads_per = (da // HEAD_DIM) // n_b
    ln_rows = tm // n_b
    for j in range(n_b):
        lo = j * nc
        val = _dot(hb, w_ref[:, 3 * da + lo:3 * da + lo + nc])
        gate = _dot(hb, w_ref[:, 3 * da + db + lo:3 * da + db + lo + nc])
        for hd in range(j * heads_per, (j + 1) * heads_per):
            finish_head(hd)
        finish_rows(j * ln_rows, ln_rows)
        u_ext[HIST_B:HIST_B + tm, lo:lo + nc] = val * jax.nn.sigmoid(gate)
    cv_ext[0:HIST_A, :] = cv_ext[tm:tm + HIST_A, :]

    conv_chunks = [(lo, r0) for lo in range(0, db, HEAD_DIM) for r0 in range(0, tm, CONV_ROWS)]
    n_a = da // nc
    n_mm = 3 * n_a
    done = 0
    for q in range(n_mm):
        j = q // 2 if q < 2 * n_a else q - 2 * n_a
        lo = j * nc
        if q < 2 * n_a and q % 2 == 0:
            c_part = _dot(hb, w_ref[:, da + lo:da + lo + nc])
        elif q < 2 * n_a:
            v_part = _dot(hb, w_ref[:, 2 * da + lo:2 * da + lo + nc])
            cv_ext[HIST_A:HIST_A + tm, lo:lo + nc] = c_part * v_part
        else:
            ba_buf[:, lo:lo + nc] = _dot(hb, w_ref[:, lo:lo + nc])
        upto = (q + 1) * len(conv_chunks) // n_mm
        for lo_c, r0 in conv_chunks[done:upto]:
            conv = _dwconv_chunk(u_ext, cb_ref, r0, CONV_ROWS, lo_c, lo_c + HEAD_DIM, HIST_B, KERNEL_B)
            conv_b_out[r0:r0 + CONV_ROWS, lo_c:lo_c + HEAD_DIM] = conv + cbb_ref[:, lo_c:lo_c + HEAD_DIM]
        done = upto
    u_ext[0:HIST_B, :] = u_ext[tm:tm + HIST_B, :]


def _mix_in(x, shift, gain_scale, w_in_bf, conv_a, norm_a, conv_b, conv_b_bias, ln_g, ln_b):
    bsz, t, d = x.shape
    da = conv_a.shape[1]
    db = conv_b.shape[1]
    tm = TM_MIX
    full = lambda shape: pl.BlockSpec(shape, lambda b, s: (0,) * len(shape))
    per_b = pl.BlockSpec((1, 1, d), lambda b, s: (b, 0, 0))
    n_t = t // tm
    return pl.pallas_call(
        functools.partial(_mix_in_kernel, tm=tm, da=da, db=db),
        grid=(bsz, n_t + 1),
        in_specs=[
            pl.BlockSpec((1, tm, d), lambda b, s: (b, jnp.minimum(s, n_t - 1), 0)),
            per_b, per_b,
            pl.BlockSpec(w_in_bf.shape, lambda b, s: (0, 0), pipeline_mode=pl.Buffered(1)),
            full((KERNEL_A, da)), full((1, da)),
            full((KERNEL_B, db)), full((1, db)), full((1, db)), full((1, db)),
        ],
        out_specs=pl.BlockSpec((1, tm, da + db), lambda b, s: (b, jnp.maximum(s - 1, 0), 0)),
        out_shape=jax.ShapeDtypeStruct((bsz, t, da + db), _BF16),
        scratch_shapes=[pltpu.VMEM((HIST_A + tm, da), _F32), pltpu.VMEM((HIST_B + tm, db), _F32),
                        pltpu.VMEM((tm, da), _F32), pltpu.VMEM((tm, db), _F32)],
        compiler_params=pltpu.CompilerParams(
            dimension_semantics=("arbitrary", "arbitrary"), vmem_limit_bytes=VMEM_LIMIT),
        name="mix_in",
    )(x, shift, gain_scale, w_in_bf, conv_a, norm_a, conv_b, conv_b_bias, ln_g, ln_b)


def _route_tile(hb, wr, br, n_groups, per_group):
    logits = _dot(hb, wr) + br
    lane = lax.broadcasted_iota(jnp.int32, logits.shape, 1).astype(_F32)
    neg = jnp.float32(-jnp.inf)
    big = jnp.float32(ROUTE_LANES)
    is_coarse = lane < n_groups
    coarse = jnp.where(is_coarse, logits, neg)
    m = jnp.max(coarse, axis=-1, keepdims=True)
    g_sel = jnp.min(jnp.where(coarse == m, lane, big), axis=-1, keepdims=True)
    p_g = 1.0 / jnp.sum(jnp.where(is_coarse, jnp.exp(logits - m), 0.0), axis=-1, keepdims=True)
    lo = n_groups + g_sel * per_group
    fine = jnp.where((lane >= lo) & (lane < lo + per_group), logits, neg)
    v1 = jnp.max(fine, axis=-1, keepdims=True)
    l1 = jnp.min(jnp.where(fine == v1, lane, big), axis=-1, keepdims=True)
    fine2 = jnp.where(lane == l1, neg, fine)
    v2 = jnp.max(fine2, axis=-1, keepdims=True)
    l2 = jnp.min(jnp.where(fine2 == v2, lane, big), axis=-1, keepdims=True)
    e2 = jnp.exp(v2 - v1)
    w1 = p_g / (1.0 + e2)
    w2 = p_g * e2 / (1.0 + e2)
    route = jnp.where(lane == 0, l1 - n_groups,
                      jnp.where(lane == 1, l2 - n_groups,
                                jnp.where(lane == 2, w1, jnp.where(lane == 3, w2, 0.0))))
    return route


def _mix_out_kernel(y_ref, x_ref, gp_ref, sh2_ref, gs_ref, wout_ref, wr_ref, br_ref,
                    x1_ref, h2_ref, route_ref, route_t_ref, cnt_ref, o_a, o_b, hb_buf, *, tm,
                    n_groups, per_group):
    part = tm // OUT_PARTS
    nc = MIX_COLS
    n_chunks = wout_ref.shape[1] // nc
    rows = part // n_chunks
    bufs = (o_a, o_b)

    def finish_rows(o_part, base, q):
        r0 = q * rows
        o = o_part[r0:r0 + rows, :]
        x1 = x_ref[0, base + r0:base + r0 + rows, :] + o * _rms(o) * gp_ref[0]
        x1_ref[0, base + r0:base + r0 + rows, :] = x1
        h2 = x1 * _rms(x1) * gs_ref[0] + sh2_ref[0]
        h2_ref[0, base + r0:base + r0 + rows, :] = _pack_bf16_pairs(h2)
        hb_buf[base + r0:base + r0 + rows, :] = h2.astype(_BF16)

    for k in range(OUT_PARTS + 1):
        y_part = y_ref[0, k * part:(k + 1) * part, :] if k < OUT_PARTS else None
        for q in range(n_chunks):
            if k < OUT_PARTS:
                bufs[k % 2][:, q * nc:(q + 1) * nc] = _dot(y_part, wout_ref[:, q * nc:(q + 1) * nc])
            if k > 0:
                finish_rows(bufs[(k - 1) % 2], (k - 1) * part, q)
    route = _route_tile(hb_buf[...], wr_ref[...], br_ref[...], n_groups, per_group)
    route_ref[0] = route
    route_t = jnp.transpose(route)[0:ROUTE_ROWS, :]
    route_t_ref[...] = route_t

    @pl.when(jnp.logical_and(pl.program_id(0) == 0, pl.program_id(1) == 0))
    def _():
        cnt_ref[...] = jnp.zeros(cnt_ref.shape, _F32)

    sub = lax.broadcasted_iota(jnp.int32, (EXPERT_ROWS, tm), 0).astype(_F32)
    chosen = jnp.where(sub == route_t[0:1, :], 1.0, 0.0) + jnp.where(sub == route_t[1:2, :], 1.0, 0.0)
    cnt_ref[...] = cnt_ref[...] + jnp.sum(chosen, axis=1, keepdims=True)


def _mix_out(y, x, gate_post, shift2, gain_scale, w_out_bf, w_route, b_route, n_groups, per_group):
    bsz, t, d = x.shape
    tm = TM_MIX
    full = lambda shape: pl.BlockSpec(shape, lambda b, s: (0,) * len(shape))
    per_b = pl.BlockSpec((1, 1, d), lambda b, s: (b, 0, 0))
    tile = pl.BlockSpec((1, tm, d), lambda b, s: (b, s, 0))
    return pl.pallas_call(
        functools.partial(_mix_out_kernel, tm=tm, n_groups=n_groups, per_group=per_group),
        grid=(bsz, t // tm),
        in_specs=[
            tile, tile, per_b, per_b, per_b,
            pl.BlockSpec(w_out_bf.shape, lambda b, s: (0, 0), pipeline_mode=pl.Buffered(1)),
            full((d, ROUTE_LANES)), full((1, ROUTE_LANES)),
        ],
        out_specs=[tile, pl.BlockSpec((1, tm, d // 2), lambda b, s: (b, s, 0)),
                   pl.BlockSpec((1, tm, ROUTE_LANES), lambda b, s: (b, s, 0)),
                   pl.BlockSpec((ROUTE_ROWS, tm), lambda b, s: (0, b * (t // tm) + s)),
                   full((EXPERT_ROWS, ROUTE_LANES))],
        out_shape=[
            jax.ShapeDtypeStruct((bsz, t, d), _F32),
            jax.ShapeDtypeStruct((bsz, t, d // 2), jnp.uint32),
            jax.ShapeDtypeStruct((bsz, t, ROUTE_LANES), _F32),
            jax.ShapeDtypeStruct((ROUTE_ROWS, bsz * t), _F32),
            jax.ShapeDtypeStruct((EXPERT_ROWS, ROUTE_LANES), _F32),
        ],
        scratch_shapes=[pltpu.VMEM((tm // OUT_PARTS, d), _F32), pltpu.VMEM((tm // OUT_PARTS, d), _F32),
                        pltpu.VMEM((tm, d), _BF16)],
        compiler_params=pltpu.CompilerParams(
            dimension_semantics=("arbitrary", "arbitrary"), vmem_limit_bytes=VMEM_LIMIT),
        name="mix_out",
    )(y, x, gate_post, shift2, gain_scale, w_out_bf, w_route, b_route)


def _rank_kernel(rt_ref, cnt_ref, u_ref, dest_ref, carry, pbase, *, tq, tm):
    sub = lax.broadcasted_iota(jnp.int32, (EXPERT_ROWS, tq), 0).astype(_F32)
    oh0 = sub == rt_ref[0:1, :]
    oh1 = sub == rt_ref[1:2, :]
    both = jnp.where(oh0, 1.0, 0.0) + jnp.where(oh1, 1.0, 0.0)

    @pl.when(pl.program_id(0) == 0)
    def _():
        tiles = jnp.floor((cnt_ref[...] + (tm - 1)) * (1.0 / tm))
        hi = jnp.floor(tiles * (1.0 / 16.0))
        lo = tiles - 16.0 * hi
        r = lax.broadcasted_iota(jnp.int32, (EXPERT_ROWS, EXPERT_ROWS), 0)
        c = lax.broadcasted_iota(jnp.int32, (EXPERT_ROWS, EXPERT_ROWS), 1)
        before = jnp.where(c < r, 1.0, 0.0).astype(_BF16)
        start_tiles = 16.0 * _dot(before, hi.astype(_BF16)) + _dot(before, lo.astype(_BF16))
        pbase[...] = start_tiles * tm
        carry[...] = jnp.zeros(carry.shape, _F32)

    earlier = _dot(both.astype(_BF16), u_ref[...])
    slot = earlier + carry[:, 0:1] + pbase[:, 0:1]
    d0 = jnp.sum(jnp.where(oh0, slot, 0.0), axis=0, keepdims=True)
    d1 = jnp.sum(jnp.where(oh1, slot, 0.0), axis=0, keepdims=True)
    row = lax.broadcasted_iota(jnp.int32, (ROUTE_ROWS, tq), 0)
    dest_ref[...] = jnp.where(row == 0, d0, jnp.where(row == 1, d1, 0.0)).astype(jnp.int32)
    carry[...] = carry[...] + jnp.sum(both, axis=1, keepdims=True)


def _rank(route_t, counts, tm):
    n_tok = route_t.shape[1]
    tq = TQ_RANK
    assert tm & (tm - 1) == 0, "segment padding must be a power of two for exact f32 arithmetic"
    idx = jnp.arange(tq, dtype=jnp.int32)
    earlier_mask = (idx[:, None] < idx[None, :]).astype(_BF16)
    return pl.pallas_call(
        functools.partial(_rank_kernel, tq=tq, tm=tm),
        grid=(n_tok // tq,),
        in_specs=[
            pl.BlockSpec((ROUTE_ROWS, tq), lambda j: (0, j)),
            pl.BlockSpec((EXPERT_ROWS, ROUTE_LANES), lambda j: (0, 0)),
            pl.BlockSpec((tq, tq), lambda j: (0, 0)),
        ],
        out_specs=pl.BlockSpec((ROUTE_ROWS, tq), lambda j: (0, j)),
        out_shape=jax.ShapeDtypeStruct((ROUTE_ROWS, n_tok), jnp.int32),
        scratch_shapes=[pltpu.VMEM((EXPERT_ROWS, ROUTE_LANES), _F32),
                        pltpu.VMEM((EXPERT_ROWS, ROUTE_LANES), _F32)],
        compiler_params=pltpu.CompilerParams(
            dimension_semantics=("arbitrary",), vmem_limit_bytes=VMEM_LIMIT),
        name="rank",
    )(route_t, counts, earlier_mask)


def _row_copy(src, dst, sem):
    return pltpu.make_async_copy(src, dst, sem)


def _scatter_kernel(ps_ref, pn_ref, used_ref, d0_ref, d1_ref, h_ref, xs_hbm, zrow, sem, psem, *, tk,
                    tm):
    i = pl.program_id(0)

    @pl.when(i == 0)
    def _():
        zrow[...] = jnp.zeros(zrow.shape, zrow.dtype)

    @pl.when(i == pl.num_programs(0) - 1)
    def _():
        n_rows = xs_hbm.shape[0]

        def fill(t, carry):
            dst = xs_hbm.at[pl.ds(pl.multiple_of(t * tm, tm), tm)]
            _row_copy(zrow, dst, psem).start()
            _row_copy(zrow, dst, psem).wait()
            return carry

        lax.fori_loop(used_ref[0], n_rows // tm, fill, 0)

    for r in range(tk):
        src = h_ref.at[pl.ds(r, 1)]
        _row_copy(src, xs_hbm.at[pl.ds(d0_ref[0, 0, r], 1)], sem).start(priority=0)
        _row_copy(src, xs_hbm.at[pl.ds(d1_ref[0, 0, r], 1)], sem).start(priority=1)

    pad_start = ps_ref[i]
    n_pad = pn_ref[i]
    zsrc = zrow.at[pl.ds(0, 1)]

    def start_pad(r, carry):
        _row_copy(zsrc, xs_hbm.at[pl.ds(pad_start + r, 1)], psem).start()
        return carry

    def wait_pad(r, carry):
        _row_copy(zsrc, xs_hbm.at[pl.ds(0, 1)], psem).wait()
        return carry

    lax.fori_loop(0, n_pad, start_pad, 0)
    lax.fori_loop(0, n_pad, wait_pad, 0)
    for _ in range(TOP_K):
        _row_copy(h_ref, xs_hbm.at[pl.ds(0, tk)], sem).wait()


def _scatter(h2_flat, d0_tiles, d1_tiles, pad_start, pad_len, used_tiles, n_rows):
    n_tok, d = h2_flat.shape
    tk = TM_CMB
    tm = TM_EXP
    smem_tile = pl.BlockSpec((1, 1, tk), lambda i, ps, pn, used: (i, 0, 0), memory_space=pltpu.SMEM)
    return pl.pallas_call(
        functools.partial(_scatter_kernel, tk=tk, tm=tm),
        grid_spec=pltpu.PrefetchScalarGridSpec(
            num_scalar_prefetch=3,
            grid=(n_tok // tk,),
            in_specs=[smem_tile, smem_tile, pl.BlockSpec((tk, d), lambda i, ps, pn, used: (i, 0))],
            out_specs=pl.BlockSpec(memory_space=pl.ANY),
            scratch_shapes=[pltpu.VMEM((tm, d), h2_flat.dtype), pltpu.SemaphoreType.DMA(()),
                            pltpu.SemaphoreType.DMA(())],
        ),
        out_shape=jax.ShapeDtypeStruct((n_rows, d), h2_flat.dtype),
        compiler_params=pltpu.CompilerParams(
            dimension_semantics=("arbitrary",), vmem_limit_bytes=VMEM_LIMIT),
        name="scatter",
    )(pad_start, pad_len, used_tiles, d0_tiles, d1_tiles, h2_flat)


def _expert_kernel(tv_ref, par_ref, glo_ref, ghi_ref, sege_ref, ntot_ref, last_ref, half_ref,
                   xs_ref, wg_hbm, wu_hbm, wd_hbm, ys_ref,
                   wgb, wub, wdb, sg, su, sd, sem, *, layer, nch):
    del last_ref
    i = pl.program_id(0)
    rg = wgb.shape[1] // nch
    rd = wdb.shape[1] // nch

    def chunk_copies(g):
        e = sege_ref[g // nch]
        c = g % nch
        s = g % 2
        rows_g = pl.ds(pl.multiple_of(c * rg, rg), rg)
        rows_d = pl.ds(pl.multiple_of(c * rd, rd), rd)
        return (_row_copy(wg_hbm.at[layer, e, rows_g], sg.at[s], sem.at[s]),
                _row_copy(wu_hbm.at[layer, e, rows_g], su.at[s], sem.at[s]),
                _row_copy(wd_hbm.at[layer, e, rows_d], sd.at[s], sem.at[s]))

    def start_chunk(g):
        for cp in chunk_copies(g):
            cp.start()

    def process_chunk(g, carry):
        for cp in chunk_copies(g):
            cp.wait()
        c = g % nch
        s = g % 2
        p = (g // nch) % 2
        wgb[p, pl.ds(pl.multiple_of(c * rg, rg), rg), :] = sg[s].astype(_BF16)
        wub[p, pl.ds(pl.multiple_of(c * rg, rg), rg), :] = su[s].astype(_BF16)
        wdb[p, pl.ds(pl.multiple_of(c * rd, rd), rd), :] = sd[s].astype(_BF16)

        @pl.when(g + 2 < ntot_ref[0])
        def _():
            start_chunk(g + 2)
        return carry

    @pl.when(i == 0)
    def _():
        start_chunk(0)
        start_chunk(1)
        lax.fori_loop(0, nch, process_chunk, 0)

    valid = tv_ref[i] == 1

    def tile_mlp(rows):
        p = par_ref[i]
        lo, hi = _unpack_bf16_pairs(xs_ref[0:rows, :])
        xb = jnp.concatenate([lo.astype(_BF16), hi.astype(_BF16)], axis=1)
        g = _dot(xb, wgb[p])
        u = _dot(xb, wub[p])
        hid = (g * jax.nn.sigmoid(g) * u).astype(_BF16)
        ys_ref[0:rows, :] = _pack_bf16_pairs(_dot(hid, wdb[p]))

    @pl.when(valid)
    def _():
        lax.fori_loop(glo_ref[i], ghi_ref[i], process_chunk, 0)
        tm = ys_ref.shape[0]
        half_only = half_ref[i] == 1

        @pl.when(jnp.logical_not(half_only))
        def _():
            tile_mlp(tm)

        @pl.when(half_only)
        def _():
            tile_mlp(tm // 2)
            ys_ref[tm // 2:tm, :] = jnp.zeros((tm - tm // 2, ys_ref.shape[1]), ys_ref.dtype)

    @pl.when(jnp.logical_not(valid))
    def _():
        ys_ref[...] = jnp.zeros(ys_ref.shape, ys_ref.dtype)


def _experts(xs, tables, w_gate, w_up, w_down, layer):
    n_rows, dp = xs.shape
    d = w_gate.shape[2]
    tm = TM_EXP
    f = w_gate.shape[3]
    nch = W_CHUNKS
    any_spec = pl.BlockSpec(memory_space=pl.ANY)
    return pl.pallas_call(
        functools.partial(_expert_kernel, layer=layer, nch=nch),
        grid_spec=pltpu.PrefetchScalarGridSpec(
            num_scalar_prefetch=8,
            grid=(n_rows // tm,),
            in_specs=[
                pl.BlockSpec((tm, dp), lambda i, *s: (jnp.minimum(i, s[6][0]), 0)),
                any_spec, any_spec, any_spec,
            ],
            out_specs=pl.BlockSpec((tm, dp), lambda i, *s: (i, 0)),
            scratch_shapes=[
                pltpu.VMEM((2, d, f), _BF16), pltpu.VMEM((2, d, f), _BF16), pltpu.VMEM((2, f, d), _BF16),
                pltpu.VMEM((2, d // nch, f), _F32), pltpu.VMEM((2, d // nch, f), _F32),
                pltpu.VMEM((2, f // nch, d), _F32), pltpu.SemaphoreType.DMA((2,)),
            ],
        ),
        out_shape=jax.ShapeDtypeStruct((n_rows, dp), jnp.uint32),
        compiler_params=pltpu.CompilerParams(
            dimension_semantics=("arbitrary",), vmem_limit_bytes=VMEM_LIMIT),
        name="experts",
    )(*tables, xs, w_gate, w_up, w_down)


def _start_row_gather(src_hbm, idx_ref, dst, sem, n_rows):
    for r in range(n_rows):
        _row_copy(src_hbm.at[pl.ds(idx_ref[0, 0, r], 1)], dst.at[pl.ds(r, 1)], sem).start(
            priority=r % 2)


def _wait_row_gather(src_hbm, dst, sem, n_rows):
    _row_copy(src_hbm.at[pl.ds(0, n_rows)], dst, sem).wait()


def _combine_kernel(d_cur, d_nxt, ys_hbm, x1_ref, route_ref, gp_ref, o_ref, rbuf, sem, *,
                    tk):
    i = pl.program_id(0)
    n = pl.num_programs(0)
    slot = i % 2

    @pl.when(i == 0)
    def _():
        _start_row_gather(ys_hbm, d_cur, rbuf.at[0], sem.at[0], TOP_K * tk)

    _wait_row_gather(ys_hbm, rbuf.at[slot], sem.at[slot], TOP_K * tk)

    @pl.when(i + 1 < n)
    def _():
        _start_row_gather(ys_hbm, d_nxt, rbuf.at[1 - slot], sem.at[1 - slot], TOP_K * tk)

    route = route_ref[0]
    lo0, hi0 = _unpack_bf16_pairs(rbuf[slot, 0:tk, :])
    lo1, hi1 = _unpack_bf16_pairs(rbuf[slot, tk:2 * tk, :])
    w0 = route[:, 2:3]
    w1 = route[:, 3:4]
    y = jnp.concatenate([w0 * lo0 + w1 * lo1, w0 * hi0 + w1 * hi1], axis=1)
    o_ref[0] = x1_ref[0] + y * _rms(y) * gp_ref[0]


def _combine(ys, dest_tiles, x1, route, gate_post):
    bsz, t, d = x1.shape
    tk = TM_CMB
    per_seq = t // tk
    n_tiles = bsz * per_seq
    smem_tile = lambda imap: pl.BlockSpec((1, 1, TOP_K * tk), imap, memory_space=pltpu.SMEM)
    return pl.pallas_call(
        functools.partial(_combine_kernel, tk=tk),
        grid=(n_tiles,),
        in_specs=[
            smem_tile(lambda i: (i, 0, 0)),
            smem_tile(lambda i: (jnp.minimum(i + 1, n_tiles - 1), 0, 0)),
            pl.BlockSpec(memory_space=pl.ANY),
            pl.BlockSpec((1, tk, d), lambda i: (i // per_seq, i % per_seq, 0)),
            pl.BlockSpec((1, tk, ROUTE_LANES), lambda i: (i // per_seq, i % per_seq, 0)),
            pl.BlockSpec((1, 1, d), lambda i: (i // per_seq, 0, 0)),
        ],
        out_specs=pl.BlockSpec((1, tk, d), lambda i: (i // per_seq, i % per_seq, 0)),
        out_shape=jax.ShapeDtypeStruct((bsz, t, d), _F32),
        scratch_shapes=[pltpu.VMEM((2, TOP_K * tk, d // 2), jnp.uint32),
                        pltpu.SemaphoreType.DMA((2,))],
        compiler_params=pltpu.CompilerParams(
            dimension_semantics=("arbitrary",), vmem_limit_bytes=VMEM_LIMIT),
        name="combine",
    )(dest_tiles, dest_tiles, ys, x1, route, gate_post)


def _segment_tables(counts, tm, n_tiles, n_steps, nch):
    n_experts = counts.shape[0]
    i32 = jnp.int32
    nt = (counts + tm - 1) // tm
    tend = jnp.cumsum(nt)
    tstart = tend - nt
    tile_idx = jnp.arange(n_tiles, dtype=i32)
    valid = tile_idx < tend[-1]
    nonempty = nt > 0
    seg_of_e = jnp.cumsum(nonempty.astype(i32)) - 1
    n_seg = seg_of_e[-1] + 1
    ids = jnp.arange(n_experts, dtype=i32)
    seg_expert = jnp.sum(
        jnp.where(nonempty[None, :] & (seg_of_e[None, :] == ids[:, None]), ids[None, :], 0), axis=1)
    owns = (tstart[None, :] <= tile_idx[:, None]) & (tile_idx[:, None] < tend[None, :])
    pick = lambda v: jnp.sum(jnp.where(owns, v[None, :], 0), axis=1)
    k = pick(seg_of_e)
    j = tile_idx - pick(tstart)
    ntt = jnp.maximum(pick(nt), 1)
    has_next = valid & (k + 1 < n_seg)
    glo = jnp.where(has_next, (k + 1) * nch + j * nch // ntt, 0)
    ghi = jnp.where(has_next, (k + 1) * nch + (j + 1) * nch // ntt, 0)
    real_rows = jnp.clip(pick(counts) - j * tm, 0, tm)
    half_only = valid & (real_rows <= tm // 2)
    expert_tables = (valid.astype(i32), (k % 2).astype(i32), glo.astype(i32), ghi.astype(i32),
                     seg_expert.astype(i32), (n_seg * nch).reshape(1).astype(i32),
                     (tend[-1:] - 1).astype(i32), half_only.astype(i32))
    pad_start = jnp.zeros((n_steps,), i32).at[:n_experts].set(tstart * tm + counts)
    pad_len = jnp.zeros((n_steps,), i32).at[:n_experts].set(nt * tm - counts)
    return expert_tables, pad_start, pad_len, tend[-1:].astype(i32)


def kernel(x, c, w_ada, b_ada, norm_mix_pre, norm_mix_post, w_in, conv_a, norm_a_out, conv_b, conv_b_bias, ln_b_gain, ln_b_bias, w_out, norm_moe_pre, norm_moe_post, w_router_group, b_router_group, w_router_expert, b_router_expert, w_expert_gate, w_expert_up, w_expert_down):
    bsz, t, d = x.shape
    depth = w_ada.shape[0]
    n_groups = w_router_group.shape[2]
    per_group = w_router_expert.shape[3]
    n_experts = n_groups * per_group
    n_tok = bsz * t
    n_assign = n_tok * TOP_K
    n_tiles = (n_assign + n_experts * (TM_EXP - 1) + TM_EXP - 1) // TM_EXP
    row = lambda v: v.reshape(1, -1)

    for l in range(depth):
        mod = _ada(c, w_ada[l], b_ada[l])
        shift1, scale1, gate1, shift2, scale2, gate2 = [
            m.reshape(bsz, 1, d) for m in jnp.split(mod, N_MOD, axis=-1)]

        y = _mix_in(x, shift1, norm_mix_pre[l] * (1.0 + scale1), w_in[l].astype(_BF16), conv_a[l],
                    row(norm_a_out[l]), conv_b[l], row(conv_b_bias[l]), row(ln_b_gain[l]),
                    row(ln_b_bias[l]))

        n_route = n_groups + n_experts
        w_fine = jnp.transpose(w_router_expert[l], (1, 0, 2)).reshape(d, n_experts)
        w_route = jnp.zeros((d, ROUTE_LANES), _F32).at[:, :n_route].set(
            jnp.concatenate([w_router_group[l], w_fine], axis=1)).astype(_BF16)
        b_route = jnp.zeros((1, ROUTE_LANES), _F32).at[0, :n_route].set(
            jnp.concatenate([b_router_group[l], b_router_expert[l].reshape(-1)]))
        x1, h2, route, route_t, counts_f = _mix_out(
            y, x, gate1 * norm_mix_post[l], shift2, norm_moe_pre[l] * (1.0 + scale2),
            w_out[l].astype(_BF16), w_route, b_route, n_groups, per_group)

        dest = _rank(route_t, counts_f, TM_EXP)
        counts = counts_f[:n_experts, 0].astype(jnp.int32)
        n_ctiles = n_tok // TM_CMB
        assert n_ctiles >= n_experts
        expert_tables, pad_start, pad_len, used_tiles = _segment_tables(
            counts, TM_EXP, n_tiles, n_ctiles, W_CHUNKS)
        d0_tiles = dest[0].reshape(n_ctiles, 1, TM_CMB)
        d1_tiles = dest[1].reshape(n_ctiles, 1, TM_CMB)
        xs = _scatter(h2.reshape(n_tok, d // 2), d0_tiles, d1_tiles, pad_start, pad_len, used_tiles,
                      n_tiles * TM_EXP)
        ys = _experts(xs, expert_tables, w_expert_gate, w_expert_up, w_expert_down, l)

        dest_tiles = jnp.concatenate([d0_tiles, d1_tiles], axis=2)
        x = _combine(ys, dest_tiles, x1, route, gate2 * norm_moe_post[l])
    return x
```

```python
import functools

import jax
import jax.numpy as jnp
from jax import lax
from jax.experimental import pallas as pl
from jax.experimental.pallas import tpu as pltpu

EPS = 1e-6
HEAD_DIM = 128
KERNEL_A = 3
KERNEL_B = 31
N_MOD = 6
TOP_K = 2
ROUTE_LANES = 128
ROUTE_ROWS = 8
EXPERT_ROWS = 128
TQ_RANK = 1024
HIST_A = 8
HIST_B = 32
CONV_ROWS = 128
MIX_COLS = 256
OUT_PARTS = 4
W_CHUNKS = 8

TM_MIX = 512
TM_EXP = 256
TM_CMB = 256
TM_SCT = 512
ADA_COLS = 1024
VMEM_LIMIT = 56 * 1024 * 1024

_BF16 = jnp.bfloat16
_F32 = jnp.float32


def _dot(a, b):
    return jnp.dot(a, b, preferred_element_type=_F32)


def _rms(v):
    return lax.rsqrt(jnp.mean(v * v, axis=-1, keepdims=True) + EPS)


def _pack_bf16_pairs(v):
    n = v.shape[1] // 2
    bits = pltpu.bitcast(v.astype(_BF16).astype(_F32), jnp.uint32)
    return (bits[:, :n] >> 16) | bits[:, n:]


def _unpack_bf16_pairs(p):
    lo = pltpu.bitcast(p << 16, _F32)
    hi = pltpu.bitcast(p & jnp.uint32(0xFFFF0000), _F32)
    return lo, hi


def _ada_kernel(c_ref, w_ref, b_ref, o_ref):
    c = c_ref[...]
    act = (c * jax.nn.sigmoid(c)).astype(_BF16)
    o_ref[...] = _dot(act, w_ref[...].astype(_BF16)) + b_ref[...]


def _ada(c, w_ada, b_ada):
    bsz, d = c.shape
    n_out = w_ada.shape[1]
    c_pad = jnp.zeros((8, d), _F32).at[:bsz].set(c)
    out = pl.pallas_call(
        _ada_kernel,
        grid=(n_out // ADA_COLS,),
        in_specs=[
            pl.BlockSpec((8, d), lambda j: (0, 0)),
            pl.BlockSpec((d, ADA_COLS), lambda j: (0, j)),
            pl.BlockSpec((1, ADA_COLS), lambda j: (0, j)),
        ],
        out_specs=pl.BlockSpec((8, ADA_COLS), lambda j: (0, j)),
        out_shape=jax.ShapeDtypeStruct((8, n_out), _F32),
        compiler_params=pltpu.CompilerParams(
            dimension_semantics=("arbitrary",), vmem_limit_bytes=VMEM_LIMIT),
        name="ada",
    )(c_pad, w_ada, b_ada.reshape(1, n_out))
    return out[:bsz]


def _dwconv_chunk(ext_ref, w_ref, r0, rows, lo, hi, hist, ksize):
    span = rows + hist
    first = hist - (ksize - 1)
    xe = ext_ref[r0:r0 + span, lo:hi]
    acc = None
    for b in range(8):
        offsets = [o for o in range(first, hist + 1) if o % 8 == b]
        if not offsets:
            continue
        shifted = xe if b == 0 else pltpu.roll(xe, span - b, 0)
        for o in offsets:
            k = o - first
            term = w_ref[k:k + 1, lo:hi] * shifted[o - b:o - b + rows, :]
            acc = term if acc is None else acc + term
    return acc


def _mix_in_kernel(x_ref, sh_ref, gs_ref, w_ref, ca_ref, na_ref, cb_ref, cbb_ref,
                   lg_ref, lb_ref, y_ref, cv_ext, u_ext, ba_buf, conv_b_out, *, tm, da, db):
    nc = MIX_COLS
    @pl.when(pl.program_id(1) == 0)
    def _():
        cv_ext[...] = jnp.zeros(cv_ext.shape, _F32)
        u_ext[0:HIST_B, :] = jnp.zeros((HIST_B, db), _F32)
        ba_buf[...] = jnp.zeros(ba_buf.shape, _F32)
        conv_b_out[...] = jnp.zeros(conv_b_out.shape, _F32)

    x = x_ref[0]
    h = x * _rms(x) * gs_ref[0] + sh_ref[0]
    hb = h.astype(_BF16)

    def finish_head(hd):
        lo, hi = hd * HEAD_DIM, (hd + 1) * HEAD_DIM
        for r0 in range(0, tm, CONV_ROWS):
            conv = _dwconv_chunk(cv_ext, ca_ref, r0, CONV_ROWS, lo, hi, HIST_A, KERNEL_A)
            seg = ba_buf[r0:r0 + CONV_ROWS, lo:hi] * conv
            y_ref[0, r0:r0 + CONV_ROWS, lo:hi] = (seg * _rms(seg) * na_ref[:, lo:hi]).astype(_BF16)

    def finish_rows(r0, rows):
        acc = conv_b_out[r0:r0 + rows, :]
        mu = jnp.mean(acc, axis=-1, keepdims=True)
        cen = acc - mu
        var = jnp.mean(cen * cen, axis=-1, keepdims=True)
        yn = cen * lax.rsqrt(var + EPS) * lg_ref[...] + lb_ref[...]
        y_ref[0, r0:r0 + rows, da:da + db] = (yn * jax.nn.sigmoid(yn)).astype(_BF16)

    n_b = db // nc
    heads_per = (da // HEAD_DIM) // n_b
    ln_rows = tm // n_b
    for j in range(n_b):
        lo = j * nc
        val = _dot(hb, w_ref[:, 3 * da + lo:3 * da + lo + nc])
        gate = _dot(hb, w_ref[:, 3 * da + db + lo:3 * da + db + lo + nc])
        for hd in range(j * heads_per, (j + 1) * heads_per):
            finish_head(hd)
        finish_rows(j * ln_rows, ln_rows)
        u_ext[HIST_B:HIST_B + tm, lo:lo + nc] = val * jax.nn.sigmoid(gate)
    cv_ext[0:HIST_A, :] = cv_ext[tm:tm + HIST_A, :]

    conv_chunks = [(lo, r0) for lo in range(0, db, HEAD_DIM) for r0 in range(0, tm, CONV_ROWS)]
    n_a = da // nc
    n_mm = 3 * n_a
    done = 0
    for q in range(n_mm):
        j = q // 2 if q < 2 * n_a else q - 2 * n_a
        lo = j * nc
        if q < 2 * n_a and q % 2 == 0:
            c_part = _dot(hb, w_ref[:, da + lo:da + lo + nc])
        elif q < 2 * n_a:
            v_part = _dot(hb, w_ref[:, 2 * da + lo:2 * da + lo + nc])
            cv_ext[HIST_A:HIST_A + tm, lo:lo + nc] = c_part * v_part
        else:
            ba_buf[:, lo:lo + nc] = _dot(hb, w_ref[:, lo:lo + nc])
        upto = (q + 1) * len(conv_chunks) // n_mm
        for lo_c, r0 in conv_chunks[done:upto]:
            conv = _dwconv_chunk(u_ext, cb_ref, r0, CONV_ROWS, lo_c, lo_c + HEAD_DIM, HIST_B, KERNEL_B)
            conv_b_out[r0:r0 + CONV_ROWS, lo_c:lo_c + HEAD_DIM] = conv + cbb_ref[:, lo_c:lo_c + HEAD_DIM]
        done = upto
    u_ext[0:HIST_B, :] = u_ext[tm:tm + HIST_B, :]


def _mix_in(x, shift, gain_scale, w_in_bf, conv_a, norm_a, conv_b, conv_b_bias, ln_g, ln_b):
    bsz, t, d = x.shape
    da = conv_a.shape[1]
    db = conv_b.shape[1]
    tm = TM_MIX
    full = lambda shape: pl.BlockSpec(shape, lambda b, s: (0,) * len(shape))
    per_b = pl.BlockSpec((1, 1, d), lambda b, s: (b, 0, 0))
    n_t = t // tm
    return pl.pallas_call(
        functools.partial(_mix_in_kernel, tm=tm, da=da, db=db),
        grid=(bsz, n_t + 1),
        in_specs=[
            pl.BlockSpec((1, tm, d), lambda b, s: (b, jnp.minimum(s, n_t - 1), 0)),
            per_b, per_b,
            pl.BlockSpec(w_in_bf.shape, lambda b, s: (0, 0), pipeline_mode=pl.Buffered(1)),
            full((KERNEL_A, da)), full((1, da)),
            full((KERNEL_B, db)), full((1, db)), full((1, db)), full((1, db)),
        ],
        out_specs=pl.BlockSpec((1, tm, da + db), lambda b, s: (b, jnp.maximum(s - 1, 0), 0)),
        out_shape=jax.ShapeDtypeStruct((bsz, t, da + db), _BF16),
        scratch_shapes=[pltpu.VMEM((HIST_A + tm, da), _F32), pltpu.VMEM((HIST_B + tm, db), _F32),
                        pltpu.VMEM((tm, da), _F32), pltpu.VMEM((tm, db), _F32)],
        compiler_params=pltpu.CompilerParams(
            dimension_semantics=("arbitrary", "arbitrary"), vmem_limit_bytes=VMEM_LIMIT),
        name="mix_in",
    )(x, shift, gain_scale, w_in_bf, conv_a, norm_a, conv_b, conv_b_bias, ln_g, ln_b)


def _route_tile(hb, wr, br, n_groups, per_group):
    logits = _dot(hb, wr) + br
    lane = lax.broadcasted_iota(jnp.int32, logits.shape, 1).astype(_F32)
    neg = jnp.float32(-jnp.inf)
    big = jnp.float32(ROUTE_LANES)
    is_coarse = lane < n_groups
    coarse = jnp.where(is_coarse, logits, neg)
    m = jnp.max(coarse, axis=-1, keepdims=True)
    g_sel = jnp.min(jnp.where(coarse == m, lane, big), axis=-1, keepdims=True)
    p_g = 1.0 / jnp.sum(jnp.where(is_coarse, jnp.exp(logits - m), 0.0), axis=-1, keepdims=True)
    lo = n_groups + g_sel * per_group
    fine = jnp.where((lane >= lo) & (lane < lo + per_group), logits, neg)
    v1 = jnp.max(fine, axis=-1, keepdims=True)
    l1 = jnp.min(jnp.where(fine == v1, lane, big), axis=-1, keepdims=True)
    fine2 = jnp.where(lane == l1, neg, fine)
    v2 = jnp.max(fine2, axis=-1, keepdims=True)
    l2 = jnp.min(jnp.where(fine2 == v2, lane, big), axis=-1, keepdims=True)
    e2 = jnp.exp(v2 - v1)
    w1 = p_g / (1.0 + e2)
    w2 = p_g * e2 / (1.0 + e2)
    route = jnp.where(lane == 0, l1 - n_groups,
                      jnp.where(lane == 1, l2 - n_groups,
                                jnp.where(lane == 2, w1, jnp.where(lane == 3, w2, 0.0))))
    return route


def _mix_out_kernel(y_ref, x_ref, gp_ref, sh2_ref, gs_ref, wout_ref, wr_ref, br_ref,
                    x1_ref, h2_ref, route_ref, route_t_ref, cnt_ref, o_a, o_b, hb_buf, *, tm,
                    n_groups, per_group):
    part = tm // OUT_PARTS
    nc = MIX_COLS
    n_chunks = wout_ref.shape[1] // nc
    rows = part // n_chunks
    bufs = (o_a, o_b)

    def finish_rows(o_part, base, q):
        r0 = q * rows
        o = o_part[r0:r0 + rows, :]
        x1 = x_ref[0, base + r0:base + r0 + rows, :] + o * _rms(o) * gp_ref[0]
        x1_ref[0, base + r0:base + r0 + rows, :] = x1
        h2 = x1 * _rms(x1) * gs_ref[0] + sh2_ref[0]
        h2_ref[0, base + r0:base + r0 + rows, :] = _pack_bf16_pairs(h2)
        hb_buf[base + r0:base + r0 + rows, :] = h2.astype(_BF16)

    for k in range(OUT_PARTS + 1):
        y_part = y_ref[0, k * part:(k + 1) * part, :] if k < OUT_PARTS else None
        for q in range(n_chunks):
            if k < OUT_PARTS:
                bufs[k % 2][:, q * nc:(q + 1) * nc] = _dot(y_part, wout_ref[:, q * nc:(q + 1) * nc])
            if k > 0:
                finish_rows(bufs[(k - 1) % 2], (k - 1) * part, q)
    route = _route_tile(hb_buf[...], wr_ref[...], br_ref[...], n_groups, per_group)
    route_ref[0] = route
    route_t = jnp.transpose(route)[0:ROUTE_ROWS, :]
    route_t_ref[...] = route_t

    @pl.when(jnp.logical_and(pl.program_id(0) == 0, pl.program_id(1) == 0))
    def _():
        cnt_ref[...] = jnp.zeros(cnt_ref.shape, _F32)

    sub = lax.broadcasted_iota(jnp.int32, (EXPERT_ROWS, tm), 0).astype(_F32)
    chosen = jnp.where(sub == route_t[0:1, :], 1.0, 0.0) + jnp.where(sub == route_t[1:2, :], 1.0, 0.0)
    cnt_ref[...] = cnt_ref[...] + jnp.sum(chosen, axis=1, keepdims=True)


def _mix_out(y, x, gate_post, shift2, gain_scale, w_out_bf, w_route, b_route, n_groups, per_group):
    bsz, t, d = x.shape
    tm = TM_MIX
    full = lambda shape: pl.BlockSpec(shape, lambda b, s: (0,) * len(shape))
    per_b = pl.BlockSpec((1, 1, d), lambda b, s: (b, 0, 0))
    tile = pl.BlockSpec((1, tm, d), lambda b, s: (b, s, 0))
    return pl.pallas_call(
        functools.partial(_mix_out_kernel, tm=tm, n_groups=n_groups, per_group=per_group),
        grid=(bsz, t // tm),
        in_specs=[
            tile, tile, per_b, per_b, per_b,
            pl.BlockSpec(w_out_bf.shape, lambda b, s: (0, 0), pipeline_mode=pl.Buffered(1)),
            full((d, ROUTE_LANES)), full((1, ROUTE_LANES)),
        ],
        out_specs=[tile, pl.BlockSpec((1, tm, d // 2), lambda b, s: (b, s, 0)),
                   pl.BlockSpec((1, tm, ROUTE_LANES), lambda b, s: (b, s, 0)),
                   pl.BlockSpec((ROUTE_ROWS, tm), lambda b, s: (0, b * (t // tm) + s)),
                   full((EXPERT_ROWS, ROUTE_LANES))],
        out_shape=[
            jax.ShapeDtypeStruct((bsz, t, d), _F32),
            jax.ShapeDtypeStruct((bsz, t, d // 2), jnp.uint32),
            jax.ShapeDtypeStruct((bsz, t, ROUTE_LANES), _F32),
            jax.ShapeDtypeStruct((ROUTE_ROWS, bsz * t), _F32),
            jax.ShapeDtypeStruct((EXPERT_ROWS, ROUTE_LANES), _F32),
        ],
        scratch_shapes=[pltpu.VMEM((tm // OUT_PARTS, d), _F32), pltpu.VMEM((tm // OUT_PARTS, d), _F32),
                        pltpu.VMEM((tm, d), _BF16)],
        compiler_params=pltpu.CompilerParams(
            dimension_semantics=("arbitrary", "arbitrary"), vmem_limit_bytes=VMEM_LIMIT),
        name="mix_out",
    )(y, x, gate_post, shift2, gain_scale, w_out_bf, w_route, b_route)


def _rank_kernel(rt_ref, cnt_ref, u_ref, dest_ref, carry, pbase, *, tq, tm):
    sub = lax.broadcasted_iota(jnp.int32, (EXPERT_ROWS, tq), 0).astype(_F32)
    oh0 = sub == rt_ref[0:1, :]
    oh1 = sub == rt_ref[1:2, :]
    both = jnp.where(oh0, 1.0, 0.0) + jnp.where(oh1, 1.0, 0.0)

    @pl.when(pl.program_id(0) == 0)
    def _():
        tiles = jnp.floor((cnt_ref[...] + (tm - 1)) * (1.0 / tm))
        hi = jnp.floor(tiles * (1.0 / 16.0))
        lo = tiles - 16.0 * hi
        r = lax.broadcasted_iota(jnp.int32, (EXPERT_ROWS, EXPERT_ROWS), 0)
        c = lax.broadcasted_iota(jnp.int32, (EXPERT_ROWS, EXPERT_ROWS), 1)
        before = jnp.where(c < r, 1.0, 0.0).astype(_BF16)
        start_tiles = 16.0 * _dot(before, hi.astype(_BF16)) + _dot(before, lo.astype(_BF16))
        pbase[...] = start_tiles * tm
        carry[...] = jnp.zeros(carry.shape, _F32)

    earlier = _dot(both.astype(_BF16), u_ref[...])
    slot = earlier + carry[:, 0:1] + pbase[:, 0:1]
    d0 = jnp.sum(jnp.where(oh0, slot, 0.0), axis=0, keepdims=True)
    d1 = jnp.sum(jnp.where(oh1, slot, 0.0), axis=0, keepdims=True)
    row = lax.broadcasted_iota(jnp.int32, (ROUTE_ROWS, tq), 0)
    dest_ref[...] = jnp.where(row == 0, d0, jnp.where(row == 1, d1, 0.0)).astype(jnp.int32)
    carry[...] = carry[...] + jnp.sum(both, axis=1, keepdims=True)


def _rank(route_t, counts, tm):
    n_tok = route_t.shape[1]
    tq = TQ_RANK
    assert tm & (tm - 1) == 0, "segment padding must be a power of two for exact f32 arithmetic"
    idx = jnp.arange(tq, dtype=jnp.int32)
    earlier_mask = (idx[:, None] < idx[None, :]).astype(_BF16)
    return pl.pallas_call(
        functools.partial(_rank_kernel, tq=tq, tm=tm),
        grid=(n_tok // tq,),
        in_specs=[
            pl.BlockSpec((ROUTE_ROWS, tq), lambda j: (0, j)),
            pl.BlockSpec((EXPERT_ROWS, ROUTE_LANES), lambda j: (0, 0)),
            pl.BlockSpec((tq, tq), lambda j: (0, 0)),
        ],
        out_specs=pl.BlockSpec((ROUTE_ROWS, tq), lambda j: (0, j)),
        out_shape=jax.ShapeDtypeStruct((ROUTE_ROWS, n_tok), jnp.int32),
        scratch_shapes=[pltpu.VMEM((EXPERT_ROWS, ROUTE_LANES), _F32),
                        pltpu.VMEM((EXPERT_ROWS, ROUTE_LANES), _F32)],
        compiler_params=pltpu.CompilerParams(
            dimension_semantics=("arbitrary",), vmem_limit_bytes=VMEM_LIMIT),
        name="rank",
    )(route_t, counts, earlier_mask)


def _row_copy(src, dst, sem):
    return pltpu.make_async_copy(src, dst, sem)


def _scatter_kernel(ps_ref, pn_ref, used_ref, d0_ref, d1_ref, h_ref, xs_hbm, zrow, sem, psem, *, tk,
                    tm):
    i = pl.program_id(0)

    @pl.when(i == 0)
    def _():
        zrow[...] = jnp.zeros(zrow.shape, zrow.dtype)

    @pl.when(i == pl.num_programs(0) - 1)
    def _():
        n_rows = xs_hbm.shape[0]

        def fill(t, carry):
            dst = xs_hbm.at[pl.ds(pl.multiple_of(t * tm, tm), tm)]
            _row_copy(zrow, dst, psem).start()
            _row_copy(zrow, dst, psem).wait()
            return carry

        lax.fori_loop(used_ref[0], n_rows // tm, fill, 0)

    for r in range(tk):
        src = h_ref.at[pl.ds(r, 1)]
        _row_copy(src, xs_hbm.at[pl.ds(d0_ref[0, 0, r], 1)], sem).start(priority=0)
        _row_copy(src, xs_hbm.at[pl.ds(d1_ref[0, 0, r], 1)], sem).start(priority=1)

    pad_start = ps_ref[i]
    n_pad = pn_ref[i]
    zsrc = zrow.at[pl.ds(0, 1)]

    def start_pad(r, carry):
        _row_copy(zsrc, xs_hbm.at[pl.ds(pad_start + r, 1)], psem).start()
        return carry

    def wait_pad(r, carry):
        _row_copy(zsrc, xs_hbm.at[pl.ds(0, 1)], psem).wait()
        return carry

    lax.fori_loop(0, n_pad, start_pad, 0)
    lax.fori_loop(0, n_pad, wait_pad, 0)
    for _ in range(TOP_K):
        _row_copy(h_ref, xs_hbm.at[pl.ds(0, tk)], sem).wait()


def _scatter(h2_flat, d0_tiles, d1_tiles, pad_start, pad_len, used_tiles, n_rows):
    n_tok, d = h2_flat.shape
    tk = TM_SCT
    tm = TM_EXP
    smem_tile = pl.BlockSpec((1, 1, tk), lambda i, ps, pn, used: (i, 0, 0), memory_space=pltpu.SMEM)
    return pl.pallas_call(
        functools.partial(_scatter_kernel, tk=tk, tm=tm),
        grid_spec=pltpu.PrefetchScalarGridSpec(
            num_scalar_prefetch=3,
            grid=(n_tok // tk,),
            in_specs=[smem_tile, smem_tile, pl.BlockSpec((tk, d), lambda i, ps, pn, used: (i, 0))],
            out_specs=pl.BlockSpec(memory_space=pl.ANY),
            scratch_shapes=[pltpu.VMEM((tm, d), h2_flat.dtype), pltpu.SemaphoreType.DMA(()),
                            pltpu.SemaphoreType.DMA(())],
        ),
        out_shape=jax.ShapeDtypeStruct((n_rows, d), h2_flat.dtype),
        compiler_params=pltpu.CompilerParams(
            dimension_semantics=("arbitrary",), vmem_limit_bytes=VMEM_LIMIT),
        name="scatter",
    )(pad_start, pad_len, used_tiles, d0_tiles, d1_tiles, h2_flat)


def _expert_kernel(tv_ref, par_ref, glo_ref, ghi_ref, sege_ref, ntot_ref, last_ref, half_ref,
                   xs_ref, wg_hbm, wu_hbm, wd_hbm, ys_ref,
                   wgb, wub, wdb, sg, su, sd, sem, *, layer, nch):
    del last_ref
    i = pl.program_id(0)
    rg = wgb.shape[1] // nch
    rd = wdb.shape[1] // nch

    def chunk_copies(g):
        e = sege_ref[g // nch]
        c = g % nch
        s = g % 2
        rows_g = pl.ds(pl.multiple_of(c * rg, rg), rg)
        rows_d = pl.ds(pl.multiple_of(c * rd, rd), rd)
        return (_row_copy(wg_hbm.at[layer, e, rows_g], sg.at[s], sem.at[s]),
                _row_copy(wu_hbm.at[layer, e, rows_g], su.at[s], sem.at[s]),
                _row_copy(wd_hbm.at[layer, e, rows_d], sd.at[s], sem.at[s]))

    def start_chunk(g):
        for cp in chunk_copies(g):
            cp.start()

    def process_chunk(g, carry):
        for cp in chunk_copies(g):
            cp.wait()
        c = g % nch
        s = g % 2
        p = (g // nch) % 2
        wgb[p, pl.ds(pl.multiple_of(c * rg, rg), rg), :] = sg[s].astype(_BF16)
        wub[p, pl.ds(pl.multiple_of(c * rg, rg), rg), :] = su[s].astype(_BF16)
        wdb[p, pl.ds(pl.multiple_of(c * rd, rd), rd), :] = sd[s].astype(_BF16)

        @pl.when(g + 2 < ntot_ref[0])
        def _():
            start_chunk(g + 2)
        return carry

    @pl.when(i == 0)
    def _():
        start_chunk(0)
        start_chunk(1)
        lax.fori_loop(0, nch, process_chunk, 0)

    valid = tv_ref[i] == 1

    def tile_mlp(rows):
        p = par_ref[i]
        lo, hi = _unpack_bf16_pairs(xs_ref[0:rows, :])
        xb = jnp.concatenate([lo.astype(_BF16), hi.astype(_BF16)], axis=1)
        g = _dot(xb, wgb[p])
        u = _dot(xb, wub[p])
        hid = (g * jax.nn.sigmoid(g) * u).astype(_BF16)
        ys_ref[0:rows, :] = _pack_bf16_pairs(_dot(hid, wdb[p]))

    @pl.when(valid)
    def _():
        lax.fori_loop(glo_ref[i], ghi_ref[i], process_chunk, 0)
        tm = ys_ref.shape[0]
        half_only = half_ref[i] == 1

        @pl.when(jnp.logical_not(half_only))
        def _():
            tile_mlp(tm)

        @pl.when(half_only)
        def _():
            tile_mlp(tm // 2)
            ys_ref[tm // 2:tm, :] = jnp.zeros((tm - tm // 2, ys_ref.shape[1]), ys_ref.dtype)

    @pl.when(jnp.logical_not(valid))
    def _():
        ys_ref[...] = jnp.zeros(ys_ref.shape, ys_ref.dtype)


def _experts(xs, tables, w_gate, w_up, w_down, layer):
    n_rows, dp = xs.shape
    d = w_gate.shape[2]
    tm = TM_EXP
    f = w_gate.shape[3]
    nch = W_CHUNKS
    any_spec = pl.BlockSpec(memory_space=pl.ANY)
    return pl.pallas_call(
        functools.partial(_expert_kernel, layer=layer, nch=nch),
        grid_spec=pltpu.PrefetchScalarGridSpec(
            num_scalar_prefetch=8,
            grid=(n_rows // tm,),
            in_specs=[
                pl.BlockSpec((tm, dp), lambda i, *s: (jnp.minimum(i, s[6][0]), 0)),
                any_spec, any_spec, any_spec,
            ],
            out_specs=pl.BlockSpec((tm, dp), lambda i, *s: (i, 0)),
            scratch_shapes=[
                pltpu.VMEM((2, d, f), _BF16), pltpu.VMEM((2, d, f), _BF16), pltpu.VMEM((2, f, d), _BF16),
                pltpu.VMEM((2, d // nch, f), _F32), pltpu.VMEM((2, d // nch, f), _F32),
                pltpu.VMEM((2, f // nch, d), _F32), pltpu.SemaphoreType.DMA((2,)),
            ],
        ),
        out_shape=jax.ShapeDtypeStruct((n_rows, dp), jnp.uint32),
        compiler_params=pltpu.CompilerParams(
            dimension_semantics=("arbitrary",), vmem_limit_bytes=VMEM_LIMIT),
        name="experts",
    )(*tables, xs, w_gate, w_up, w_down)


def _start_row_gather(src_hbm, idx_ref, dst, sem, n_rows):
    for r in range(n_rows):
        _row_copy(src_hbm.at[pl.ds(idx_ref[0, 0, r], 1)], dst.at[pl.ds(r, 1)], sem).start(
            priority=r % 2)


def _wait_row_gather(src_hbm, dst, sem, n_rows):
    _row_copy(src_hbm.at[pl.ds(0, n_rows)], dst, sem).wait()


def _combine_kernel(d_cur, d_nxt, ys_hbm, x1_ref, route_ref, gp_ref, o_ref, rbuf, sem, *,
                    tk):
    i = pl.program_id(0)
    n = pl.num_programs(0)
    slot = i % 2

    @pl.when(i == 0)
    def _():
        _start_row_gather(ys_hbm, d_cur, rbuf.at[0], sem.at[0], TOP_K * tk)

    _wait_row_gather(ys_hbm, rbuf.at[slot], sem.at[slot], TOP_K * tk)

    @pl.when(i + 1 < n)
    def _():
        _start_row_gather(ys_hbm, d_nxt, rbuf.at[1 - slot], sem.at[1 - slot], TOP_K * tk)

    route = route_ref[0]
    lo0, hi0 = _unpack_bf16_pairs(rbuf[slot, 0:tk, :])
    lo1, hi1 = _unpack_bf16_pairs(rbuf[slot, tk:2 * tk, :])
    w0 = route[:, 2:3]
    w1 = route[:, 3:4]
    y = jnp.concatenate([w0 * lo0 + w1 * lo1, w0 * hi0 + w1 * hi1], axis=1)
    o_ref[0] = x1_ref[0] + y * _rms(y) * gp_ref[0]


def _combine(ys, dest_tiles, x1, route, gate_post):
    bsz, t, d = x1.shape
    tk = TM_CMB
    per_seq = t // tk
    n_tiles = bsz * per_seq
    smem_tile = lambda imap: pl.BlockSpec((1, 1, TOP_K * tk), imap, memory_space=pltpu.SMEM)
    return pl.pallas_call(
        functools.partial(_combine_kernel, tk=tk),
        grid=(n_tiles,),
        in_specs=[
            smem_tile(lambda i: (i, 0, 0)),
            smem_tile(lambda i: (jnp.minimum(i + 1, n_tiles - 1), 0, 0)),
            pl.BlockSpec(memory_space=pl.ANY),
            pl.BlockSpec((1, tk, d), lambda i: (i // per_seq, i % per_seq, 0)),
            pl.BlockSpec((1, tk, ROUTE_LANES), lambda i: (i // per_seq, i % per_seq, 0)),
            pl.BlockSpec((1, 1, d), lambda i: (i // per_seq, 0, 0)),
        ],
        out_specs=pl.BlockSpec((1, tk, d), lambda i: (i // per_seq, i % per_seq, 0)),
        out_shape=jax.ShapeDtypeStruct((bsz, t, d), _F32),
        scratch_shapes=[pltpu.VMEM((2, TOP_K * tk, d // 2), jnp.uint32),
                        pltpu.SemaphoreType.DMA((2,))],
        compiler_params=pltpu.CompilerParams(
            dimension_semantics=("arbitrary",), vmem_limit_bytes=VMEM_LIMIT),
        name="combine",
    )(dest_tiles, dest_tiles, ys, x1, route, gate_post)


def _segment_tables(counts, tm, n_tiles, n_steps, nch):
    n_experts = counts.shape[0]
    i32 = jnp.int32
    nt = (counts + tm - 1) // tm
    tend = jnp.cumsum(nt)
    tstart = tend - nt
    tile_idx = jnp.arange(n_tiles, dtype=i32)
    valid = tile_idx < tend[-1]
    nonempty = nt > 0
    seg_of_e = jnp.cumsum(nonempty.astype(i32)) - 1
    n_seg = seg_of_e[-1] + 1
    ids = jnp.arange(n_experts, dtype=i32)
    seg_expert = jnp.sum(
        jnp.where(nonempty[None, :] & (seg_of_e[None, :] == ids[:, None]), ids[None, :], 0), axis=1)
    owns = (tstart[None, :] <= tile_idx[:, None]) & (tile_idx[:, None] < tend[None, :])
    pick = lambda v: jnp.sum(jnp.where(owns, v[None, :], 0), axis=1)
    k = pick(seg_of_e)
    j = tile_idx - pick(tstart)
    ntt = jnp.maximum(pick(nt), 1)
    has_next = valid & (k + 1 < n_seg)
    glo = jnp.where(has_next, (k + 1) * nch + j * nch // ntt, 0)
    ghi = jnp.where(has_next, (k + 1) * nch + (j + 1) * nch // ntt, 0)
    real_rows = jnp.clip(pick(counts) - j * tm, 0, tm)
    half_only = valid & (real_rows <= tm // 2)
    expert_tables = (valid.astype(i32), (k % 2).astype(i32), glo.astype(i32), ghi.astype(i32),
                     seg_expert.astype(i32), (n_seg * nch).reshape(1).astype(i32),
                     (tend[-1:] - 1).astype(i32), half_only.astype(i32))
    pad_start = jnp.zeros((n_steps,), i32).at[:n_experts].set(tstart * tm + counts)
    pad_len = jnp.zeros((n_steps,), i32).at[:n_experts].set(nt * tm - counts)
    return expert_tables, pad_start, pad_len, tend[-1:].astype(i32)


def kernel(x, c, w_ada, b_ada, norm_mix_pre, norm_mix_post, w_in, conv_a, norm_a_out, conv_b, conv_b_bias, ln_b_gain, ln_b_bias, w_out, norm_moe_pre, norm_moe_post, w_router_group, b_router_group, w_router_expert, b_router_expert, w_expert_gate, w_expert_up, w_expert_down):
    bsz, t, d = x.shape
    depth = w_ada.shape[0]
    n_groups = w_router_group.shape[2]
    per_group = w_router_expert.shape[3]
    n_experts = n_groups * per_group
    n_tok = bsz * t
    n_assign = n_tok * TOP_K
    n_tiles = (n_assign + n_experts * (TM_EXP - 1) + TM_EXP - 1) // TM_EXP
    row = lambda v: v.reshape(1, -1)

    for l in range(depth):
        mod = _ada(c, w_ada[l], b_ada[l])
        shift1, scale1, gate1, shift2, scale2, gate2 = [
            m.reshape(bsz, 1, d) for m in jnp.split(mod, N_MOD, axis=-1)]

        y = _mix_in(x, shift1, norm_mix_pre[l] * (1.0 + scale1), w_in[l].astype(_BF16), conv_a[l],
                    row(norm_a_out[l]), conv_b[l], row(conv_b_bias[l]), row(ln_b_gain[l]),
                    row(ln_b_bias[l]))

        n_route = n_groups + n_experts
        w_fine = jnp.transpose(w_router_expert[l], (1, 0, 2)).reshape(d, n_experts)
        w_route = jnp.zeros((d, ROUTE_LANES), _F32).at[:, :n_route].set(
            jnp.concatenate([w_router_group[l], w_fine], axis=1)).astype(_BF16)
        b_route = jnp.zeros((1, ROUTE_LANES), _F32).at[0, :n_route].set(
            jnp.concatenate([b_router_group[l], b_router_expert[l].reshape(-1)]))
        x1, h2, route, route_t, counts_f = _mix_out(
            y, x, gate1 * norm_mix_post[l], shift2, norm_moe_pre[l] * (1.0 + scale2),
            w_out[l].astype(_BF16), w_route, b_route, n_groups, per_group)

        dest = _rank(route_t, counts_f, TM_EXP)
        counts = counts_f[:n_experts, 0].astype(jnp.int32)
        n_stiles = n_tok // TM_SCT
        assert n_stiles >= n_experts
        expert_tables, pad_start, pad_len, used_tiles = _segment_tables(
            counts, TM_EXP, n_tiles, n_stiles, W_CHUNKS)
        xs = _scatter(h2.reshape(n_tok, d // 2), dest[0].reshape(n_stiles, 1, TM_SCT),
                      dest[1].reshape(n_stiles, 1, TM_SCT), pad_start, pad_len, used_tiles,
                      n_tiles * TM_EXP)
        ys = _experts(xs, expert_tables, w_expert_gate, w_expert_up, w_expert_down, l)

        n_ctiles = n_tok // TM_CMB
        dest_tiles = jnp.concatenate([dest[0].reshape(n_ctiles, 1, TM_CMB),
                                      dest[1].reshape(n_ctiles, 1, TM_CMB)], axis=2)
        x = _combine(ys, dest_tiles, x1, route, gate2 * norm_moe_post[l])
    return x
```

```python
import functools

import jax
import jax.numpy as jnp
from jax import lax
from jax.experimental import pallas as pl
from jax.experimental.pallas import tpu as pltpu

EPS = 1e-6
HEAD_DIM = 128
KERNEL_A = 3
KERNEL_B = 31
N_MOD = 6
TOP_K = 2
ROUTE_LANES = 128
ROUTE_ROWS = 8
EXPERT_ROWS = 128
TQ_RANK = 1024
HIST_A = 8
HIST_B = 32
CONV_ROWS = 128
MIX_COLS = 256
OUT_PARTS = 4
W_CHUNKS = 8

TM_MIX = 512
TM_EXP = 256
TM_CMB = 256
TM_SCT = 1024
ADA_COLS = 1024
VMEM_LIMIT = 56 * 1024 * 1024

_BF16 = jnp.bfloat16
_F32 = jnp.float32


def _dot(a, b):
    return jnp.dot(a, b, preferred_element_type=_F32)


def _rms(v):
    return lax.rsqrt(jnp.mean(v * v, axis=-1, keepdims=True) + EPS)


def _pack_bf16_pairs(v):
    n = v.shape[1] // 2
    bits = pltpu.bitcast(v.astype(_BF16).astype(_F32), jnp.uint32)
    return (bits[:, :n] >> 16) | bits[:, n:]


def _unpack_bf16_pairs(p):
    lo = pltpu.bitcast(p << 16, _F32)
    hi = pltpu.bitcast(p & jnp.uint32(0xFFFF0000), _F32)
    return lo, hi


def _ada_kernel(c_ref, w_ref, b_ref, o_ref):
    c = c_ref[...]
    act = (c * jax.nn.sigmoid(c)).astype(_BF16)
    o_ref[...] = _dot(act, w_ref[...].astype(_BF16)) + b_ref[...]


def _ada(c, w_ada, b_ada):
    bsz, d = c.shape
    n_out = w_ada.shape[1]
    c_pad = jnp.zeros((8, d), _F32).at[:bsz].set(c)
    out = pl.pallas_call(
        _ada_kernel,
        grid=(n_out // ADA_COLS,),
        in_specs=[
            pl.BlockSpec((8, d), lambda j: (0, 0)),
            pl.BlockSpec((d, ADA_COLS), lambda j: (0, j)),
            pl.BlockSpec((1, ADA_COLS), lambda j: (0, j)),
        ],
        out_specs=pl.BlockSpec((8, ADA_COLS), lambda j: (0, j)),
        out_shape=jax.ShapeDtypeStruct((8, n_out), _F32),
        compiler_params=pltpu.CompilerParams(
            dimension_semantics=("arbitrary",), vmem_limit_bytes=VMEM_LIMIT),
        name="ada",
    )(c_pad, w_ada, b_ada.reshape(1, n_out))
    return out[:bsz]


def _dwconv_chunk(ext_ref, w_ref, r0, rows, lo, hi, hist, ksize):
    span = rows + hist
    first = hist - (ksize - 1)
    xe = ext_ref[r0:r0 + span, lo:hi]
    acc = None
    for b in range(8):
        offsets = [o for o in range(first, hist + 1) if o % 8 == b]
        if not offsets:
            continue
        shifted = xe if b == 0 else pltpu.roll(xe, span - b, 0)
        for o in offsets:
            k = o - first
            term = w_ref[k:k + 1, lo:hi] * shifted[o - b:o - b + rows, :]
            acc = term if acc is None else acc + term
    return acc


def _mix_in_kernel(x_ref, sh_ref, gs_ref, w_ref, ca_ref, na_ref, cb_ref, cbb_ref,
                   lg_ref, lb_ref, y_ref, cv_ext, u_ext, ba_buf, conv_b_out, *, tm, da, db):
    nc = MIX_COLS
    @pl.when(pl.program_id(1) == 0)
    def _():
        cv_ext[...] = jnp.zeros(cv_ext.shape, _F32)
        u_ext[0:HIST_B, :] = jnp.zeros((HIST_B, db), _F32)
        ba_buf[...] = jnp.zeros(ba_buf.shape, _F32)
        conv_b_out[...] = jnp.zeros(conv_b_out.shape, _F32)

    x = x_ref[0]
    h = x * _rms(x) * gs_ref[0] + sh_ref[0]
    hb = h.astype(_BF16)

    def finish_head(hd):
        lo, hi = hd * HEAD_DIM, (hd + 1) * HEAD_DIM
        for r0 in range(0, tm, CONV_ROWS):
            conv = _dwconv_chunk(cv_ext, ca_ref, r0, CONV_ROWS, lo, hi, HIST_A, KERNEL_A)
            seg = ba_buf[r0:r0 + CONV_ROWS, lo:hi] * conv
            y_ref[0, r0:r0 + CONV_ROWS, lo:hi] = (seg * _rms(seg) * na_ref[:, lo:hi]).astype(_BF16)

    def finish_rows(r0, rows):
        acc = conv_b_out[r0:r0 + rows, :]
        mu = jnp.mean(acc, axis=-1, keepdims=True)
        cen = acc - mu
        var = jnp.mean(cen * cen, axis=-1, keepdims=True)
        yn = cen * lax.rsqrt(var + EPS) * lg_ref[...] + lb_ref[...]
        y_ref[0, r0:r0 + rows, da:da + db] = (yn * jax.nn.sigmoid(yn)).astype(_BF16)

    n_b = db // nc
    heads_per = (da // HEAD_DIM) // n_b
    ln_rows = tm // n_b
    for j in range(n_b):
        lo = j * nc
        val = _dot(hb, w_ref[:, 3 * da + lo:3 * da + lo + nc])
        gate = _dot(hb, w_ref[:, 3 * da + db + lo:3 * da + db + lo + nc])
        for hd in range(j * heads_per, (j + 1) * heads_per):
            finish_head(hd)
        finish_rows(j * ln_rows, ln_rows)
        u_ext[HIST_B:HIST_B + tm, lo:lo + nc] = val * jax.nn.sigmoid(gate)
    cv_ext[0:HIST_A, :] = cv_ext[tm:tm + HIST_A, :]

    conv_chunks = [(lo, r0) for lo in range(0, db, HEAD_DIM) for r0 in range(0, tm, CONV_ROWS)]
    n_a = da // nc
    n_mm = 3 * n_a
    done = 0
    for q in range(n_mm):
        j = q // 2 if q < 2 * n_a else q - 2 * n_a
        lo = j * nc
        if q < 2 * n_a and q % 2 == 0:
            c_part = _dot(hb, w_ref[:, da + lo:da + lo + nc])
        elif q < 2 * n_a:
            v_part = _dot(hb, w_ref[:, 2 * da + lo:2 * da + lo + nc])
            cv_ext[HIST_A:HIST_A + tm, lo:lo + nc] = c_part * v_part
        else:
            ba_buf[:, lo:lo + nc] = _dot(hb, w_ref[:, lo:lo + nc])
        upto = (q + 1) * len(conv_chunks) // n_mm
        for lo_c, r0 in conv_chunks[done:upto]:
            conv = _dwconv_chunk(u_ext, cb_ref, r0, CONV_ROWS, lo_c, lo_c + HEAD_DIM, HIST_B, KERNEL_B)
            conv_b_out[r0:r0 + CONV_ROWS, lo_c:lo_c + HEAD_DIM] = conv + cbb_ref[:, lo_c:lo_c + HEAD_DIM]
        done = upto
    u_ext[0:HIST_B, :] = u_ext[tm:tm + HIST_B, :]


def _mix_in(x, shift, gain_scale, w_in_bf, conv_a, norm_a, conv_b, conv_b_bias, ln_g, ln_b):
    bsz, t, d = x.shape
    da = conv_a.shape[1]
    db = conv_b.shape[1]
    tm = TM_MIX
    full = lambda shape: pl.BlockSpec(shape, lambda b, s: (0,) * len(shape))
    per_b = pl.BlockSpec((1, 1, d), lambda b, s: (b, 0, 0))
    n_t = t // tm
    return pl.pallas_call(
        functools.partial(_mix_in_kernel, tm=tm, da=da, db=db),
        grid=(bsz, n_t + 1),
        in_specs=[
            pl.BlockSpec((1, tm, d), lambda b, s: (b, jnp.minimum(s, n_t - 1), 0)),
            per_b, per_b,
            pl.BlockSpec(w_in_bf.shape, lambda b, s: (0, 0), pipeline_mode=pl.Buffered(1)),
            full((KERNEL_A, da)), full((1, da)),
            full((KERNEL_B, db)), full((1, db)), full((1, db)), full((1, db)),
        ],
        out_specs=pl.BlockSpec((1, tm, da + db), lambda b, s: (b, jnp.maximum(s - 1, 0), 0)),
        out_shape=jax.ShapeDtypeStruct((bsz, t, da + db), _BF16),
        scratch_shapes=[pltpu.VMEM((HIST_A + tm, da), _F32), pltpu.VMEM((HIST_B + tm, db), _F32),
                        pltpu.VMEM((tm, da), _F32), pltpu.VMEM((tm, db), _F32)],
        compiler_params=pltpu.CompilerParams(
            dimension_semantics=("arbitrary", "arbitrary"), vmem_limit_bytes=VMEM_LIMIT),
        name="mix_in",
    )(x, shift, gain_scale, w_in_bf, conv_a, norm_a, conv_b, conv_b_bias, ln_g, ln_b)


def _route_tile(hb, wr, br, n_groups, per_group):
    logits = _dot(hb, wr) + br
    lane = lax.broadcasted_iota(jnp.int32, logits.shape, 1).astype(_F32)
    neg = jnp.float32(-jnp.inf)
    big = jnp.float32(ROUTE_LANES)
    is_coarse = lane < n_groups
    coarse = jnp.where(is_coarse, logits, neg)
    m = jnp.max(coarse, axis=-1, keepdims=True)
    g_sel = jnp.min(jnp.where(coarse == m, lane, big), axis=-1, keepdims=True)
    p_g = 1.0 / jnp.sum(jnp.where(is_coarse, jnp.exp(logits - m), 0.0), axis=-1, keepdims=True)
    lo = n_groups + g_sel * per_group
    fine = jnp.where((lane >= lo) & (lane < lo + per_group), logits, neg)
    v1 = jnp.max(fine, axis=-1, keepdims=True)
    l1 = jnp.min(jnp.where(fine == v1, lane, big), axis=-1, keepdims=True)
    fine2 = jnp.where(lane == l1, neg, fine)
    v2 = jnp.max(fine2, axis=-1, keepdims=True)
    l2 = jnp.min(jnp.where(fine2 == v2, lane, big), axis=-1, keepdims=True)
    e2 = jnp.exp(v2 - v1)
    w1 = p_g / (1.0 + e2)
    w2 = p_g * e2 / (1.0 + e2)
    route = jnp.where(lane == 0, l1 - n_groups,
                      jnp.where(lane == 1, l2 - n_groups,
                                jnp.where(lane == 2, w1, jnp.where(lane == 3, w2, 0.0))))
    return route


def _mix_out_kernel(y_ref, x_ref, gp_ref, sh2_ref, gs_ref, wout_ref, wr_ref, br_ref,
                    x1_ref, h2_ref, route_ref, route_t_ref, cnt_ref, o_a, o_b, hb_buf, *, tm,
                    n_groups, per_group):
    part = tm // OUT_PARTS
    nc = MIX_COLS
    n_chunks = wout_ref.shape[1] // nc
    rows = part // n_chunks
    bufs = (o_a, o_b)

    def finish_rows(o_part, base, q):
        r0 = q * rows
        o = o_part[r0:r0 + rows, :]
        x1 = x_ref[0, base + r0:base + r0 + rows, :] + o * _rms(o) * gp_ref[0]
        x1_ref[0, base + r0:base + r0 + rows, :] = x1
        h2 = x1 * _rms(x1) * gs_ref[0] + sh2_ref[0]
        h2_ref[0, base + r0:base + r0 + rows, :] = _pack_bf16_pairs(h2)
        hb_buf[base + r0:base + r0 + rows, :] = h2.astype(_BF16)

    for k in range(OUT_PARTS + 1):
        y_part = y_ref[0, k * part:(k + 1) * part, :] if k < OUT_PARTS else None
        for q in range(n_chunks):
            if k < OUT_PARTS:
                bufs[k % 2][:, q * nc:(q + 1) * nc] = _dot(y_part, wout_ref[:, q * nc:(q + 1) * nc])
            if k > 0:
                finish_rows(bufs[(k - 1) % 2], (k - 1) * part, q)
    route = _route_tile(hb_buf[...], wr_ref[...], br_ref[...], n_groups, per_group)
    route_ref[0] = route
    route_t = jnp.transpose(route)[0:ROUTE_ROWS, :]
    route_t_ref[...] = route_t

    @pl.when(jnp.logical_and(pl.program_id(0) == 0, pl.program_id(1) == 0))
    def _():
        cnt_ref[...] = jnp.zeros(cnt_ref.shape, _F32)

    sub = lax.broadcasted_iota(jnp.int32, (EXPERT_ROWS, tm), 0).astype(_F32)
    chosen = jnp.where(sub == route_t[0:1, :], 1.0, 0.0) + jnp.where(sub == route_t[1:2, :], 1.0, 0.0)
    cnt_ref[...] = cnt_ref[...] + jnp.sum(chosen, axis=1, keepdims=True)


def _mix_out(y, x, gate_post, shift2, gain_scale, w_out_bf, w_route, b_route, n_groups, per_group):
    bsz, t, d = x.shape
    tm = TM_MIX
    full = lambda shape: pl.BlockSpec(shape, lambda b, s: (0,) * len(shape))
    per_b = pl.BlockSpec((1, 1, d), lambda b, s: (b, 0, 0))
    tile = pl.BlockSpec((1, tm, d), lambda b, s: (b, s, 0))
    return pl.pallas_call(
        functools.partial(_mix_out_kernel, tm=tm, n_groups=n_groups, per_group=per_group),
        grid=(bsz, t // tm),
        in_specs=[
            tile, tile, per_b, per_b, per_b,
            pl.BlockSpec(w_out_bf.shape, lambda b, s: (0, 0), pipeline_mode=pl.Buffered(1)),
            full((d, ROUTE_LANES)), full((1, ROUTE_LANES)),
        ],
        out_specs=[tile, pl.BlockSpec((1, tm, d // 2), lambda b, s: (b, s, 0)),
                   pl.BlockSpec((1, tm, ROUTE_LANES), lambda b, s: (b, s, 0)),
                   pl.BlockSpec((ROUTE_ROWS, tm), lambda b, s: (0, b * (t // tm) + s)),
                   full((EXPERT_ROWS, ROUTE_LANES))],
        out_shape=[
            jax.ShapeDtypeStruct((bsz, t, d), _F32),
            jax.ShapeDtypeStruct((bsz, t, d // 2), jnp.uint32),
            jax.ShapeDtypeStruct((bsz, t, ROUTE_LANES), _F32),
            jax.ShapeDtypeStruct((ROUTE_ROWS, bsz * t), _F32),
            jax.ShapeDtypeStruct((EXPERT_ROWS, ROUTE_LANES), _F32),
        ],
        scratch_shapes=[pltpu.VMEM((tm // OUT_PARTS, d), _F32), pltpu.VMEM((tm // OUT_PARTS, d), _F32),
                        pltpu.VMEM((tm, d), _BF16)],
        compiler_params=pltpu.CompilerParams(
            dimension_semantics=("arbitrary", "arbitrary"), vmem_limit_bytes=VMEM_LIMIT),
        name="mix_out",
    )(y, x, gate_post, shift2, gain_scale, w_out_bf, w_route, b_route)


def _rank_kernel(rt_ref, cnt_ref, u_ref, dest_ref, carry, pbase, *, tq, tm):
    sub = lax.broadcasted_iota(jnp.int32, (EXPERT_ROWS, tq), 0).astype(_F32)
    oh0 = sub == rt_ref[0:1, :]
    oh1 = sub == rt_ref[1:2, :]
    both = jnp.where(oh0, 1.0, 0.0) + jnp.where(oh1, 1.0, 0.0)

    @pl.when(pl.program_id(0) == 0)
    def _():
        tiles = jnp.floor((cnt_ref[...] + (tm - 1)) * (1.0 / tm))
        hi = jnp.floor(tiles * (1.0 / 16.0))
        lo = tiles - 16.0 * hi
        r = lax.broadcasted_iota(jnp.int32, (EXPERT_ROWS, EXPERT_ROWS), 0)
        c = lax.broadcasted_iota(jnp.int32, (EXPERT_ROWS, EXPERT_ROWS), 1)
        before = jnp.where(c < r, 1.0, 0.0).astype(_BF16)
        start_tiles = 16.0 * _dot(before, hi.astype(_BF16)) + _dot(before, lo.astype(_BF16))
        pbase[...] = start_tiles * tm
        carry[...] = jnp.zeros(carry.shape, _F32)

    earlier = _dot(both.astype(_BF16), u_ref[...])
    slot = earlier + carry[:, 0:1] + pbase[:, 0:1]
    d0 = jnp.sum(jnp.where(oh0, slot, 0.0), axis=0, keepdims=True)
    d1 = jnp.sum(jnp.where(oh1, slot, 0.0), axis=0, keepdims=True)
    row = lax.broadcasted_iota(jnp.int32, (ROUTE_ROWS, tq), 0)
    dest_ref[...] = jnp.where(row == 0, d0, jnp.where(row == 1, d1, 0.0)).astype(jnp.int32)
    carry[...] = carry[...] + jnp.sum(both, axis=1, keepdims=True)


def _rank(route_t, counts, tm):
    n_tok = route_t.shape[1]
    tq = TQ_RANK
    assert tm & (tm - 1) == 0, "segment padding must be a power of two for exact f32 arithmetic"
    idx = jnp.arange(tq, dtype=jnp.int32)
    earlier_mask = (idx[:, None] < idx[None, :]).astype(_BF16)
    return pl.pallas_call(
        functools.partial(_rank_kernel, tq=tq, tm=tm),
        grid=(n_tok // tq,),
        in_specs=[
            pl.BlockSpec((ROUTE_ROWS, tq), lambda j: (0, j)),
            pl.BlockSpec((EXPERT_ROWS, ROUTE_LANES), lambda j: (0, 0)),
            pl.BlockSpec((tq, tq), lambda j: (0, 0)),
        ],
        out_specs=pl.BlockSpec((ROUTE_ROWS, tq), lambda j: (0, j)),
        out_shape=jax.ShapeDtypeStruct((ROUTE_ROWS, n_tok), jnp.int32),
        scratch_shapes=[pltpu.VMEM((EXPERT_ROWS, ROUTE_LANES), _F32),
                        pltpu.VMEM((EXPERT_ROWS, ROUTE_LANES), _F32)],
        compiler_params=pltpu.CompilerParams(
            dimension_semantics=("arbitrary",), vmem_limit_bytes=VMEM_LIMIT),
        name="rank",
    )(route_t, counts, earlier_mask)


def _row_copy(src, dst, sem):
    return pltpu.make_async_copy(src, dst, sem)


def _scatter_kernel(ps_ref, pn_ref, used_ref, d0_ref, d1_ref, h_ref, xs_hbm, zrow, sem, psem, *, tk,
                    tm, experts_per_step):
    i = pl.program_id(0)

    @pl.when(i == 0)
    def _():
        zrow[...] = jnp.zeros(zrow.shape, zrow.dtype)

    @pl.when(i == pl.num_programs(0) - 1)
    def _():
        n_rows = xs_hbm.shape[0]

        def fill(t, carry):
            dst = xs_hbm.at[pl.ds(pl.multiple_of(t * tm, tm), tm)]
            _row_copy(zrow, dst, psem).start()
            _row_copy(zrow, dst, psem).wait()
            return carry

        lax.fori_loop(used_ref[0], n_rows // tm, fill, 0)

    for r in range(tk):
        src = h_ref.at[pl.ds(r, 1)]
        _row_copy(src, xs_hbm.at[pl.ds(d0_ref[0, 0, r], 1)], sem).start(priority=0)
        _row_copy(src, xs_hbm.at[pl.ds(d1_ref[0, 0, r], 1)], sem).start(priority=1)

    zsrc = zrow.at[pl.ds(0, 1)]
    for q in range(experts_per_step):
        pad_start = ps_ref[i * experts_per_step + q]
        n_pad = pn_ref[i * experts_per_step + q]

        def start_pad(r, carry, pad_start=pad_start):
            _row_copy(zsrc, xs_hbm.at[pl.ds(pad_start + r, 1)], psem).start()
            return carry

        def wait_pad(r, carry):
            _row_copy(zsrc, xs_hbm.at[pl.ds(0, 1)], psem).wait()
            return carry

        lax.fori_loop(0, n_pad, start_pad, 0)
        lax.fori_loop(0, n_pad, wait_pad, 0)
    for _ in range(TOP_K):
        _row_copy(h_ref, xs_hbm.at[pl.ds(0, tk)], sem).wait()


def _scatter(h2_flat, d0_tiles, d1_tiles, pad_start, pad_len, used_tiles, n_rows):
    n_tok, d = h2_flat.shape
    tk = TM_SCT
    tm = TM_EXP
    smem_tile = pl.BlockSpec((1, 1, tk), lambda i, ps, pn, used: (i, 0, 0), memory_space=pltpu.SMEM)
    return pl.pallas_call(
        functools.partial(_scatter_kernel, tk=tk, tm=tm,
                          experts_per_step=pad_start.shape[0] // (n_tok // tk)),
        grid_spec=pltpu.PrefetchScalarGridSpec(
            num_scalar_prefetch=3,
            grid=(n_tok // tk,),
            in_specs=[smem_tile, smem_tile, pl.BlockSpec((tk, d), lambda i, ps, pn, used: (i, 0))],
            out_specs=pl.BlockSpec(memory_space=pl.ANY),
            scratch_shapes=[pltpu.VMEM((tm, d), h2_flat.dtype), pltpu.SemaphoreType.DMA(()),
                            pltpu.SemaphoreType.DMA(())],
        ),
        out_shape=jax.ShapeDtypeStruct((n_rows, d), h2_flat.dtype),
        compiler_params=pltpu.CompilerParams(
            dimension_semantics=("arbitrary",), vmem_limit_bytes=VMEM_LIMIT),
        name="scatter",
    )(pad_start, pad_len, used_tiles, d0_tiles, d1_tiles, h2_flat)


def _expert_kernel(tv_ref, par_ref, glo_ref, ghi_ref, sege_ref, ntot_ref, last_ref, half_ref,
                   xs_ref, wg_hbm, wu_hbm, wd_hbm, ys_ref,
                   wgb, wub, wdb, sg, su, sd, sem, *, layer, nch):
    del last_ref
    i = pl.program_id(0)
    rg = wgb.shape[1] // nch
    rd = wdb.shape[1] // nch

    def chunk_copies(g):
        e = sege_ref[g // nch]
        c = g % nch
        s = g % 2
        rows_g = pl.ds(pl.multiple_of(c * rg, rg), rg)
        rows_d = pl.ds(pl.multiple_of(c * rd, rd), rd)
        return (_row_copy(wg_hbm.at[layer, e, rows_g], sg.at[s], sem.at[s]),
                _row_copy(wu_hbm.at[layer, e, rows_g], su.at[s], sem.at[s]),
                _row_copy(wd_hbm.at[layer, e, rows_d], sd.at[s], sem.at[s]))

    def start_chunk(g):
        for cp in chunk_copies(g):
            cp.start()

    def process_chunk(g, carry):
        for cp in chunk_copies(g):
            cp.wait()
        c = g % nch
        s = g % 2
        p = (g // nch) % 2
        wgb[p, pl.ds(pl.multiple_of(c * rg, rg), rg), :] = sg[s].astype(_BF16)
        wub[p, pl.ds(pl.multiple_of(c * rg, rg), rg), :] = su[s].astype(_BF16)
        wdb[p, pl.ds(pl.multiple_of(c * rd, rd), rd), :] = sd[s].astype(_BF16)

        @pl.when(g + 2 < ntot_ref[0])
        def _():
            start_chunk(g + 2)
        return carry

    @pl.when(i == 0)
    def _():
        start_chunk(0)
        start_chunk(1)
        lax.fori_loop(0, nch, process_chunk, 0)

    valid = tv_ref[i] == 1

    def tile_mlp(rows):
        p = par_ref[i]
        lo, hi = _unpack_bf16_pairs(xs_ref[0:rows, :])
        xb = jnp.concatenate([lo.astype(_BF16), hi.astype(_BF16)], axis=1)
        g = _dot(xb, wgb[p])
        u = _dot(xb, wub[p])
        hid = (g * jax.nn.sigmoid(g) * u).astype(_BF16)
        ys_ref[0:rows, :] = _pack_bf16_pairs(_dot(hid, wdb[p]))

    @pl.when(valid)
    def _():
        lax.fori_loop(glo_ref[i], ghi_ref[i], process_chunk, 0)
        tm = ys_ref.shape[0]
        half_only = half_ref[i] == 1

        @pl.when(jnp.logical_not(half_only))
        def _():
            tile_mlp(tm)

        @pl.when(half_only)
        def _():
            tile_mlp(tm // 2)
            ys_ref[tm // 2:tm, :] = jnp.zeros((tm - tm // 2, ys_ref.shape[1]), ys_ref.dtype)

    @pl.when(jnp.logical_not(valid))
    def _():
        ys_ref[...] = jnp.zeros(ys_ref.shape, ys_ref.dtype)


def _experts(xs, tables, w_gate, w_up, w_down, layer):
    n_rows, dp = xs.shape
    d = w_gate.shape[2]
    tm = TM_EXP
    f = w_gate.shape[3]
    nch = W_CHUNKS
    any_spec = pl.BlockSpec(memory_space=pl.ANY)
    return pl.pallas_call(
        functools.partial(_expert_kernel, layer=layer, nch=nch),
        grid_spec=pltpu.PrefetchScalarGridSpec(
            num_scalar_prefetch=8,
            grid=(n_rows // tm,),
            in_specs=[
                pl.BlockSpec((tm, dp), lambda i, *s: (jnp.minimum(i, s[6][0]), 0)),
                any_spec, any_spec, any_spec,
            ],
            out_specs=pl.BlockSpec((tm, dp), lambda i, *s: (i, 0)),
            scratch_shapes=[
                pltpu.VMEM((2, d, f), _BF16), pltpu.VMEM((2, d, f), _BF16), pltpu.VMEM((2, f, d), _BF16),
                pltpu.VMEM((2, d // nch, f), _F32), pltpu.VMEM((2, d // nch, f), _F32),
                pltpu.VMEM((2, f // nch, d), _F32), pltpu.SemaphoreType.DMA((2,)),
            ],
        ),
        out_shape=jax.ShapeDtypeStruct((n_rows, dp), jnp.uint32),
        compiler_params=pltpu.CompilerParams(
            dimension_semantics=("arbitrary",), vmem_limit_bytes=VMEM_LIMIT),
        name="experts",
    )(*tables, xs, w_gate, w_up, w_down)


def _start_row_gather(src_hbm, idx_ref, dst, sem, n_rows):
    for r in range(n_rows):
        _row_copy(src_hbm.at[pl.ds(idx_ref[0, 0, r], 1)], dst.at[pl.ds(r, 1)], sem).start(
            priority=r % 2)


def _wait_row_gather(src_hbm, dst, sem, n_rows):
    _row_copy(src_hbm.at[pl.ds(0, n_rows)], dst, sem).wait()


def _combine_kernel(d_cur, d_nxt, ys_hbm, x1_ref, route_ref, gp_ref, o_ref, rbuf, sem, *,
                    tk):
    i = pl.program_id(0)
    n = pl.num_programs(0)
    slot = i % 2

    @pl.when(i == 0)
    def _():
        _start_row_gather(ys_hbm, d_cur, rbuf.at[0], sem.at[0], TOP_K * tk)

    _wait_row_gather(ys_hbm, rbuf.at[slot], sem.at[slot], TOP_K * tk)

    @pl.when(i + 1 < n)
    def _():
        _start_row_gather(ys_hbm, d_nxt, rbuf.at[1 - slot], sem.at[1 - slot], TOP_K * tk)

    route = route_ref[0]
    lo0, hi0 = _unpack_bf16_pairs(rbuf[slot, 0:tk, :])
    lo1, hi1 = _unpack_bf16_pairs(rbuf[slot, tk:2 * tk, :])
    w0 = route[:, 2:3]
    w1 = route[:, 3:4]
    y = jnp.concatenate([w0 * lo0 + w1 * lo1, w0 * hi0 + w1 * hi1], axis=1)
    o_ref[0] = x1_ref[0] + y * _rms(y) * gp_ref[0]


def _combine(ys, dest_tiles, x1, route, gate_post):
    bsz, t, d = x1.shape
    tk = TM_CMB
    per_seq = t // tk
    n_tiles = bsz * per_seq
    smem_tile = lambda imap: pl.BlockSpec((1, 1, TOP_K * tk), imap, memory_space=pltpu.SMEM)
    return pl.pallas_call(
        functools.partial(_combine_kernel, tk=tk),
        grid=(n_tiles,),
        in_specs=[
            smem_tile(lambda i: (i, 0, 0)),
            smem_tile(lambda i: (jnp.minimum(i + 1, n_tiles - 1), 0, 0)),
            pl.BlockSpec(memory_space=pl.ANY),
            pl.BlockSpec((1, tk, d), lambda i: (i // per_seq, i % per_seq, 0)),
            pl.BlockSpec((1, tk, ROUTE_LANES), lambda i: (i // per_seq, i % per_seq, 0)),
            pl.BlockSpec((1, 1, d), lambda i: (i // per_seq, 0, 0)),
        ],
        out_specs=pl.BlockSpec((1, tk, d), lambda i: (i // per_seq, i % per_seq, 0)),
        out_shape=jax.ShapeDtypeStruct((bsz, t, d), _F32),
        scratch_shapes=[pltpu.VMEM((2, TOP_K * tk, d // 2), jnp.uint32),
                        pltpu.SemaphoreType.DMA((2,))],
        compiler_params=pltpu.CompilerParams(
            dimension_semantics=("arbitrary",), vmem_limit_bytes=VMEM_LIMIT),
        name="combine",
    )(dest_tiles, dest_tiles, ys, x1, route, gate_post)


def _segment_tables(counts, tm, n_tiles, n_steps, nch):
    n_experts = counts.shape[0]
    i32 = jnp.int32
    nt = (counts + tm - 1) // tm
    tend = jnp.cumsum(nt)
    tstart = tend - nt
    tile_idx = jnp.arange(n_tiles, dtype=i32)
    valid = tile_idx < tend[-1]
    nonempty = nt > 0
    seg_of_e = jnp.cumsum(nonempty.astype(i32)) - 1
    n_seg = seg_of_e[-1] + 1
    ids = jnp.arange(n_experts, dtype=i32)
    seg_expert = jnp.sum(
        jnp.where(nonempty[None, :] & (seg_of_e[None, :] == ids[:, None]), ids[None, :], 0), axis=1)
    owns = (tstart[None, :] <= tile_idx[:, None]) & (tile_idx[:, None] < tend[None, :])
    pick = lambda v: jnp.sum(jnp.where(owns, v[None, :], 0), axis=1)
    k = pick(seg_of_e)
    j = tile_idx - pick(tstart)
    ntt = jnp.maximum(pick(nt), 1)
    has_next = valid & (k + 1 < n_seg)
    glo = jnp.where(has_next, (k + 1) * nch + j * nch // ntt, 0)
    ghi = jnp.where(has_next, (k + 1) * nch + (j + 1) * nch // ntt, 0)
    real_rows = jnp.clip(pick(counts) - j * tm, 0, tm)
    half_only = valid & (real_rows <= tm // 2)
    expert_tables = (valid.astype(i32), (k % 2).astype(i32), glo.astype(i32), ghi.astype(i32),
                     seg_expert.astype(i32), (n_seg * nch).reshape(1).astype(i32),
                     (tend[-1:] - 1).astype(i32), half_only.astype(i32))
    pad_start = jnp.zeros((n_steps,), i32).at[:n_experts].set(tstart * tm + counts)
    pad_len = jnp.zeros((n_steps,), i32).at[:n_experts].set(nt * tm - counts)
    return expert_tables, pad_start, pad_len, tend[-1:].astype(i32)


def kernel(x, c, w_ada, b_ada, norm_mix_pre, norm_mix_post, w_in, conv_a, norm_a_out, conv_b, conv_b_bias, ln_b_gain, ln_b_bias, w_out, norm_moe_pre, norm_moe_post, w_router_group, b_router_group, w_router_expert, b_router_expert, w_expert_gate, w_expert_up, w_expert_down):
    bsz, t, d = x.shape
    depth = w_ada.shape[0]
    n_groups = w_router_group.shape[2]
    per_group = w_router_expert.shape[3]
    n_experts = n_groups * per_group
    n_tok = bsz * t
    n_assign = n_tok * TOP_K
    n_tiles = (n_assign + n_experts * (TM_EXP - 1) + TM_EXP - 1) // TM_EXP
    row = lambda v: v.reshape(1, -1)

    for l in range(depth):
        mod = _ada(c, w_ada[l], b_ada[l])
        shift1, scale1, gate1, shift2, scale2, gate2 = [
            m.reshape(bsz, 1, d) for m in jnp.split(mod, N_MOD, axis=-1)]

        y = _mix_in(x, shift1, norm_mix_pre[l] * (1.0 + scale1), w_in[l].astype(_BF16), conv_a[l],
                    row(norm_a_out[l]), conv_b[l], row(conv_b_bias[l]), row(ln_b_gain[l]),
                    row(ln_b_bias[l]))

        n_route = n_groups + n_experts
        w_fine = jnp.transpose(w_router_expert[l], (1, 0, 2)).reshape(d, n_experts)
        w_route = jnp.zeros((d, ROUTE_LANES), _F32).at[:, :n_route].set(
            jnp.concatenate([w_router_group[l], w_fine], axis=1)).astype(_BF16)
        b_route = jnp.zeros((1, ROUTE_LANES), _F32).at[0, :n_route].set(
            jnp.concatenate([b_router_group[l], b_router_expert[l].reshape(-1)]))
        x1, h2, route, route_t, counts_f = _mix_out(
            y, x, gate1 * norm_mix_post[l], shift2, norm_moe_pre[l] * (1.0 + scale2),
            w_out[l].astype(_BF16), w_route, b_route, n_groups, per_group)

        dest = _rank(route_t, counts_f, TM_EXP)
        counts = counts_f[:n_experts, 0].astype(jnp.int32)
        n_stiles = n_tok // TM_SCT
        experts_per_step = -(-n_experts // n_stiles)
        expert_tables, pad_start, pad_len, used_tiles = _segment_tables(
            counts, TM_EXP, n_tiles, n_stiles * experts_per_step, W_CHUNKS)
        xs = _scatter(h2.reshape(n_tok, d // 2), dest[0].reshape(n_stiles, 1, TM_SCT),
                      dest[1].reshape(n_stiles, 1, TM_SCT), pad_start, pad_len, used_tiles,
                      n_tiles * TM_EXP)
        ys = _experts(xs, expert_tables, w_expert_gate, w_expert_up, w_expert_down, l)

        n_ctiles = n_tok // TM_CMB
        dest_tiles = jnp.concatenate([dest[0].reshape(n_ctiles, 1, TM_CMB),
                                      dest[1].reshape(n_ctiles, 1, TM_CMB)], axis=2)
        x = _combine(ys, dest_tiles, x1, route, gate2 * norm_moe_post[l])
    return x
```

```python
import functools

import jax
import jax.numpy as jnp
from jax import lax
from jax.experimental import pallas as pl
from jax.experimental.pallas import tpu as pltpu

EPS = 1e-6
HEAD_DIM = 128
KERNEL_A = 3
KERNEL_B = 31
N_MOD = 6
TOP_K = 2
ROUTE_LANES = 128
ROUTE_ROWS = 8
EXPERT_ROWS = 128
TQ_RANK = 1024
HIST_A = 8
HIST_B = 32
CONV_ROWS = 128
MIX_COLS = 256
OUT_PARTS = 4
W_CHUNKS = 8

TM_MIX = 512
TM_EXP = 256
TM_CMB = 256
TM_SCT = 512
ADA_COLS = 1024
VMEM_LIMIT = 56 * 1024 * 1024

_BF16 = jnp.bfloat16
_F32 = jnp.float32


def _dot(a, b):
    return jnp.dot(a, b, preferred_element_type=_F32)


def _rms(v):
    return lax.rsqrt(jnp.mean(v * v, axis=-1, keepdims=True) + EPS)


def _pack_bf16_pairs(v):
    n = v.shape[1] // 2
    bits = pltpu.bitcast(v.astype(_BF16).astype(_F32), jnp.uint32)
    return (bits[:, :n] >> 16) | bits[:, n:]


def _unpack_bf16_pairs(p):
    lo = pltpu.bitcast(p << 16, _F32)
    hi = pltpu.bitcast(p & jnp.uint32(0xFFFF0000), _F32)
    return lo, hi


def _ada_kernel(c_ref, w_ref, b_ref, o_ref):
    c = c_ref[...]
    act = (c * jax.nn.sigmoid(c)).astype(_BF16)
    o_ref[...] = _dot(act, w_ref[...].astype(_BF16)) + b_ref[...]


def _ada(c, w_ada, b_ada):
    bsz, d = c.shape
    n_out = w_ada.shape[1]
    c_pad = jnp.zeros((8, d), _F32).at[:bsz].set(c)
    out = pl.pallas_call(
        _ada_kernel,
        grid=(n_out // ADA_COLS,),
        in_specs=[
            pl.BlockSpec((8, d), lambda j: (0, 0)),
            pl.BlockSpec((d, ADA_COLS), lambda j: (0, j)),
            pl.BlockSpec((1, ADA_COLS), lambda j: (0, j)),
        ],
        out_specs=pl.BlockSpec((8, ADA_COLS), lambda j: (0, j)),
        out_shape=jax.ShapeDtypeStruct((8, n_out), _F32),
        compiler_params=pltpu.CompilerParams(
            dimension_semantics=("arbitrary",), vmem_limit_bytes=VMEM_LIMIT),
        name="ada",
    )(c_pad, w_ada, b_ada.reshape(1, n_out))
    return out[:bsz]


def _dwconv_chunk(ext_ref, w_ref, r0, rows, lo, hi, hist, ksize):
    span = rows + hist
    first = hist - (ksize - 1)
    xe = ext_ref[r0:r0 + span, lo:hi]
    acc = None
    for b in range(8):
        offsets = [o for o in range(first, hist + 1) if o % 8 == b]
        if not offsets:
            continue
        shifted = xe if b == 0 else pltpu.roll(xe, span - b, 0)
        for o in offsets:
            k = o - first
            term = w_ref[k:k + 1, lo:hi] * shifted[o - b:o - b + rows, :]
            acc = term if acc is None else acc + term
    return acc


def _mix_in_kernel(x_ref, sh_ref, gs_ref, w_ref, ca_ref, na_ref, cb_ref, cbb_ref,
                   lg_ref, lb_ref, y_ref, cv_ext, u_ext, ba_buf, conv_b_out, *, tm, da, db):
    nc = MIX_COLS
    @pl.when(pl.program_id(1) == 0)
    def _():
        cv_ext[...] = jnp.zeros(cv_ext.shape, _F32)
        u_ext[0:HIST_B, :] = jnp.zeros((HIST_B, db), _F32)
        ba_buf[...] = jnp.zeros(ba_buf.shape, _F32)
        conv_b_out[...] = jnp.zeros(conv_b_out.shape, _F32)

    def finish_head(hd):
        lo, hi = hd * HEAD_DIM, (hd + 1) * HEAD_DIM
        for r0 in range(0, tm, CONV_ROWS):
            conv = _dwconv_chunk(cv_ext, ca_ref, r0, CONV_ROWS, lo, hi, HIST_A, KERNEL_A)
            seg = ba_buf[r0:r0 + CONV_ROWS, lo:hi] * conv
            y_ref[0, r0:r0 + CONV_ROWS, lo:hi] = (seg * _rms(seg) * na_ref[:, lo:hi]).astype(_BF16)

    def finish_rows(r0, rows):
        acc = conv_b_out[r0:r0 + rows, :]
        mu = jnp.mean(acc, axis=-1, keepdims=True)
        cen = acc - mu
        var = jnp.mean(cen * cen, axis=-1, keepdims=True)
        yn = cen * lax.rsqrt(var + EPS) * lg_ref[...] + lb_ref[...]
        y_ref[0, r0:r0 + rows, da:da + db] = (yn * jax.nn.sigmoid(yn)).astype(_BF16)

    last = pl.num_programs(1) - 1

    @pl.when(pl.program_id(1) == last)
    def _():
        for hd in range(da // HEAD_DIM):
            finish_head(hd)
        finish_rows(0, tm)

    @pl.when(pl.program_id(1) < last)
    def _():
        x = x_ref[0]
        h = x * _rms(x) * gs_ref[0] + sh_ref[0]
        hb = h.astype(_BF16)

        n_b = db // nc
        heads_per = (da // HEAD_DIM) // n_b
        ln_rows = tm // n_b
        for j in range(n_b):
            lo = j * nc
            val = _dot(hb, w_ref[:, 3 * da + lo:3 * da + lo + nc])
            gate = _dot(hb, w_ref[:, 3 * da + db + lo:3 * da + db + lo + nc])
            for hd in range(j * heads_per, (j + 1) * heads_per):
                finish_head(hd)
            finish_rows(j * ln_rows, ln_rows)
            u_ext[HIST_B:HIST_B + tm, lo:lo + nc] = val * jax.nn.sigmoid(gate)
        cv_ext[0:HIST_A, :] = cv_ext[tm:tm + HIST_A, :]

        conv_chunks = [(lo, r0) for lo in range(0, db, HEAD_DIM) for r0 in range(0, tm, CONV_ROWS)]
        n_a = da // nc
        n_mm = 3 * n_a
        done = 0
        for q in range(n_mm):
            j = q // 2 if q < 2 * n_a else q - 2 * n_a
            lo = j * nc
            if q < 2 * n_a and q % 2 == 0:
                c_part = _dot(hb, w_ref[:, da + lo:da + lo + nc])
            elif q < 2 * n_a:
                v_part = _dot(hb, w_ref[:, 2 * da + lo:2 * da + lo + nc])
                cv_ext[HIST_A:HIST_A + tm, lo:lo + nc] = c_part * v_part
            else:
                ba_buf[:, lo:lo + nc] = _dot(hb, w_ref[:, lo:lo + nc])
            upto = (q + 1) * len(conv_chunks) // n_mm
            for lo_c, r0 in conv_chunks[done:upto]:
                conv = _dwconv_chunk(u_ext, cb_ref, r0, CONV_ROWS, lo_c, lo_c + HEAD_DIM, HIST_B, KERNEL_B)
                conv_b_out[r0:r0 + CONV_ROWS, lo_c:lo_c + HEAD_DIM] = conv + cbb_ref[:, lo_c:lo_c + HEAD_DIM]
            done = upto
        u_ext[0:HIST_B, :] = u_ext[tm:tm + HIST_B, :]


def _mix_in(x, shift, gain_scale, w_in_bf, conv_a, norm_a, conv_b, conv_b_bias, ln_g, ln_b):
    bsz, t, d = x.shape
    da = conv_a.shape[1]
    db = conv_b.shape[1]
    tm = TM_MIX
    full = lambda shape: pl.BlockSpec(shape, lambda b, s: (0,) * len(shape))
    per_b = pl.BlockSpec((1, 1, d), lambda b, s: (b, 0, 0))
    n_t = t // tm
    return pl.pallas_call(
        functools.partial(_mix_in_kernel, tm=tm, da=da, db=db),
        grid=(bsz, n_t + 1),
        in_specs=[
            pl.BlockSpec((1, tm, d), lambda b, s: (b, jnp.minimum(s, n_t - 1), 0)),
            per_b, per_b,
            pl.BlockSpec(w_in_bf.shape, lambda b, s: (0, 0), pipeline_mode=pl.Buffered(1)),
            full((KERNEL_A, da)), full((1, da)),
            full((KERNEL_B, db)), full((1, db)), full((1, db)), full((1, db)),
        ],
        out_specs=pl.BlockSpec((1, tm, da + db), lambda b, s: (b, jnp.maximum(s - 1, 0), 0)),
        out_shape=jax.ShapeDtypeStruct((bsz, t, da + db), _BF16),
        scratch_shapes=[pltpu.VMEM((HIST_A + tm, da), _F32), pltpu.VMEM((HIST_B + tm, db), _F32),
                        pltpu.VMEM((tm, da), _F32), pltpu.VMEM((tm, db), _F32)],
        compiler_params=pltpu.CompilerParams(
            dimension_semantics=("arbitrary", "arbitrary"), vmem_limit_bytes=VMEM_LIMIT),
        name="mix_in",
    )(x, shift, gain_scale, w_in_bf, conv_a, norm_a, conv_b, conv_b_bias, ln_g, ln_b)


def _route_tile(hb, wr, br, n_groups, per_group):
    logits = _dot(hb, wr) + br
    lane = lax.broadcasted_iota(jnp.int32, logits.shape, 1).astype(_F32)
    neg = jnp.float32(-jnp.inf)
    big = jnp.float32(ROUTE_LANES)
    is_coarse = lane < n_groups
    coarse = jnp.where(is_coarse, logits, neg)
    m = jnp.max(coarse, axis=-1, keepdims=True)
    g_sel = jnp.min(jnp.where(coarse == m, lane, big), axis=-1, keepdims=True)
    p_g = 1.0 / jnp.sum(jnp.where(is_coarse, jnp.exp(logits - m), 0.0), axis=-1, keepdims=True)
    lo = n_groups + g_sel * per_group
    fine = jnp.where((lane >= lo) & (lane < lo + per_group), logits, neg)
    v1 = jnp.max(fine, axis=-1, keepdims=True)
    l1 = jnp.min(jnp.where(fine == v1, lane, big), axis=-1, keepdims=True)
    fine2 = jnp.where(lane == l1, neg, fine)
    v2 = jnp.max(fine2, axis=-1, keepdims=True)
    l2 = jnp.min(jnp.where(fine2 == v2, lane, big), axis=-1, keepdims=True)
    e2 = jnp.exp(v2 - v1)
    w1 = p_g / (1.0 + e2)
    w2 = p_g * e2 / (1.0 + e2)
    route = jnp.where(lane == 0, l1 - n_groups,
                      jnp.where(lane == 1, l2 - n_groups,
                                jnp.where(lane == 2, w1, jnp.where(lane == 3, w2, 0.0))))
    return route


def _mix_out_kernel(y_ref, x_ref, gp_ref, sh2_ref, gs_ref, wout_ref, wr_ref, br_ref,
                    x1_ref, h2_ref, route_ref, route_t_ref, cnt_ref, o_a, o_b, hb_buf, *, tm,
                    n_groups, per_group):
    part = tm // OUT_PARTS
    nc = MIX_COLS
    n_chunks = wout_ref.shape[1] // nc
    rows = part // n_chunks
    bufs = (o_a, o_b)

    def finish_rows(o_part, base, q):
        r0 = q * rows
        o = o_part[r0:r0 + rows, :]
        x1 = x_ref[0, base + r0:base + r0 + rows, :] + o * _rms(o) * gp_ref[0]
        x1_ref[0, base + r0:base + r0 + rows, :] = x1
        h2 = x1 * _rms(x1) * gs_ref[0] + sh2_ref[0]
        h2_ref[0, base + r0:base + r0 + rows, :] = _pack_bf16_pairs(h2)
        hb_buf[base + r0:base + r0 + rows, :] = h2.astype(_BF16)

    for k in range(OUT_PARTS + 1):
        y_part = y_ref[0, k * part:(k + 1) * part, :] if k < OUT_PARTS else None
        for q in range(n_chunks):
            if k < OUT_PARTS:
                bufs[k % 2][:, q * nc:(q + 1) * nc] = _dot(y_part, wout_ref[:, q * nc:(q + 1) * nc])
            if k > 0:
                finish_rows(bufs[(k - 1) % 2], (k - 1) * part, q)
    route = _route_tile(hb_buf[...], wr_ref[...], br_ref[...], n_groups, per_group)
    route_ref[0] = route
    route_t = jnp.transpose(route)[0:ROUTE_ROWS, :]
    route_t_ref[...] = route_t

    @pl.when(jnp.logical_and(pl.program_id(0) == 0, pl.program_id(1) == 0))
    def _():
        cnt_ref[...] = jnp.zeros(cnt_ref.shape, _F32)

    sub = lax.broadcasted_iota(jnp.int32, (EXPERT_ROWS, tm), 0).astype(_F32)
    chosen = jnp.where(sub == route_t[0:1, :], 1.0, 0.0) + jnp.where(sub == route_t[1:2, :], 1.0, 0.0)
    cnt_ref[...] = cnt_ref[...] + jnp.sum(chosen, axis=1, keepdims=True)


def _mix_out(y, x, gate_post, shift2, gain_scale, w_out_bf, w_route, b_route, n_groups, per_group):
    bsz, t, d = x.shape
    tm = TM_MIX
    full = lambda shape: pl.BlockSpec(shape, lambda b, s: (0,) * len(shape))
    per_b = pl.BlockSpec((1, 1, d), lambda b, s: (b, 0, 0))
    tile = pl.BlockSpec((1, tm, d), lambda b, s: (b, s, 0))
    return pl.pallas_call(
        functools.partial(_mix_out_kernel, tm=tm, n_groups=n_groups, per_group=per_group),
        grid=(bsz, t // tm),
        in_specs=[
            tile, tile, per_b, per_b, per_b,
            pl.BlockSpec(w_out_bf.shape, lambda b, s: (0, 0), pipeline_mode=pl.Buffered(1)),
            full((d, ROUTE_LANES)), full((1, ROUTE_LANES)),
        ],
        out_specs=[tile, pl.BlockSpec((1, tm, d // 2), lambda b, s: (b, s, 0)),
                   pl.BlockSpec((1, tm, ROUTE_LANES), lambda b, s: (b, s, 0)),
                   pl.BlockSpec((ROUTE_ROWS, tm), lambda b, s: (0, b * (t // tm) + s)),
                   full((EXPERT_ROWS, ROUTE_LANES))],
        out_shape=[
            jax.ShapeDtypeStruct((bsz, t, d), _F32),
            jax.ShapeDtypeStruct((bsz, t, d // 2), jnp.uint32),
            jax.ShapeDtypeStruct((bsz, t, ROUTE_LANES), _F32),
            jax.ShapeDtypeStruct((ROUTE_ROWS, bsz * t), _F32),
            jax.ShapeDtypeStruct((EXPERT_ROWS, ROUTE_LANES), _F32),
        ],
        scratch_shapes=[pltpu.VMEM((tm // OUT_PARTS, d), _F32), pltpu.VMEM((tm // OUT_PARTS, d), _F32),
                        pltpu.VMEM((tm, d), _BF16)],
        compiler_params=pltpu.CompilerParams(
            dimension_semantics=("arbitrary", "arbitrary"), vmem_limit_bytes=VMEM_LIMIT),
        name="mix_out",
    )(y, x, gate_post, shift2, gain_scale, w_out_bf, w_route, b_route)


def _rank_kernel(rt_ref, cnt_ref, u_ref, dest_ref, carry, pbase, *, tq, tm):
    sub = lax.broadcasted_iota(jnp.int32, (EXPERT_ROWS, tq), 0).astype(_F32)
    oh0 = sub == rt_ref[0:1, :]
    oh1 = sub == rt_ref[1:2, :]
    both = jnp.where(oh0, 1.0, 0.0) + jnp.where(oh1, 1.0, 0.0)

    @pl.when(pl.program_id(0) == 0)
    def _():
        tiles = jnp.floor((cnt_ref[...] + (tm - 1)) * (1.0 / tm))
        hi = jnp.floor(tiles * (1.0 / 16.0))
        lo = tiles - 16.0 * hi
        r = lax.broadcasted_iota(jnp.int32, (EXPERT_ROWS, EXPERT_ROWS), 0)
        c = lax.broadcasted_iota(jnp.int32, (EXPERT_ROWS, EXPERT_ROWS), 1)
        before = jnp.where(c < r, 1.0, 0.0).astype(_BF16)
        start_tiles = 16.0 * _dot(before, hi.astype(_BF16)) + _dot(before, lo.astype(_BF16))
        pbase[...] = start_tiles * tm
        carry[...] = jnp.zeros(carry.shape, _F32)

    earlier = _dot(both.astype(_BF16), u_ref[...])
    slot = earlier + carry[:, 0:1] + pbase[:, 0:1]
    d0 = jnp.sum(jnp.where(oh0, slot, 0.0), axis=0, keepdims=True)
    d1 = jnp.sum(jnp.where(oh1, slot, 0.0), axis=0, keepdims=True)
    row = lax.broadcasted_iota(jnp.int32, (ROUTE_ROWS, tq), 0)
    dest_ref[...] = jnp.where(row == 0, d0, jnp.where(row == 1, d1, 0.0)).astype(jnp.int32)
    carry[...] = carry[...] + jnp.sum(both, axis=1, keepdims=True)


def _rank(route_t, counts, tm):
    n_tok = route_t.shape[1]
    tq = TQ_RANK
    assert tm & (tm - 1) == 0, "segment padding must be a power of two for exact f32 arithmetic"
    idx = jnp.arange(tq, dtype=jnp.int32)
    earlier_mask = (idx[:, None] < idx[None, :]).astype(_BF16)
    return pl.pallas_call(
        functools.partial(_rank_kernel, tq=tq, tm=tm),
        grid=(n_tok // tq,),
        in_specs=[
            pl.BlockSpec((ROUTE_ROWS, tq), lambda j: (0, j)),
            pl.BlockSpec((EXPERT_ROWS, ROUTE_LANES), lambda j: (0, 0)),
            pl.BlockSpec((tq, tq), lambda j: (0, 0)),
        ],
        out_specs=pl.BlockSpec((ROUTE_ROWS, tq), lambda j: (0, j)),
        out_shape=jax.ShapeDtypeStruct((ROUTE_ROWS, n_tok), jnp.int32),
        scratch_shapes=[pltpu.VMEM((EXPERT_ROWS, ROUTE_LANES), _F32),
                        pltpu.VMEM((EXPERT_ROWS, ROUTE_LANES), _F32)],
        compiler_params=pltpu.CompilerParams(
            dimension_semantics=("arbitrary",), vmem_limit_bytes=VMEM_LIMIT),
        name="rank",
    )(route_t, counts, earlier_mask)


def _row_copy(src, dst, sem):
    return pltpu.make_async_copy(src, dst, sem)


def _scatter_kernel(ps_ref, pn_ref, used_ref, d0_ref, d1_ref, h_ref, xs_hbm, zrow, sem, psem, *, tk,
                    tm):
    i = pl.program_id(0)

    @pl.when(i == 0)
    def _():
        zrow[...] = jnp.zeros(zrow.shape, zrow.dtype)

    @pl.when(i == pl.num_programs(0) - 1)
    def _():
        n_rows = xs_hbm.shape[0]

        def fill(t, carry):
            dst = xs_hbm.at[pl.ds(pl.multiple_of(t * tm, tm), tm)]
            _row_copy(zrow, dst, psem).start()
            _row_copy(zrow, dst, psem).wait()
            return carry

        lax.fori_loop(used_ref[0], n_rows // tm, fill, 0)

    for r in range(tk):
        src = h_ref.at[pl.ds(r, 1)]
        _row_copy(src, xs_hbm.at[pl.ds(d0_ref[0, 0, r], 1)], sem).start(priority=0)
        _row_copy(src, xs_hbm.at[pl.ds(d1_ref[0, 0, r], 1)], sem).start(priority=1)

    pad_start = ps_ref[i]
    n_pad = pn_ref[i]
    zsrc = zrow.at[pl.ds(0, 1)]

    def start_pad(r, carry):
        _row_copy(zsrc, xs_hbm.at[pl.ds(pad_start + r, 1)], psem).start()
        return carry

    def wait_pad(r, carry):
        _row_copy(zsrc, xs_hbm.at[pl.ds(0, 1)], psem).wait()
        return carry

    lax.fori_loop(0, n_pad, start_pad, 0)
    lax.fori_loop(0, n_pad, wait_pad, 0)
    for _ in range(TOP_K):
        _row_copy(h_ref, xs_hbm.at[pl.ds(0, tk)], sem).wait()


def _scatter(h2_flat, d0_tiles, d1_tiles, pad_start, pad_len, used_tiles, n_rows):
    n_tok, d = h2_flat.shape
    tk = TM_SCT
    tm = TM_EXP
    smem_tile = pl.BlockSpec((1, 1, tk), lambda i, ps, pn, used: (i, 0, 0), memory_space=pltpu.SMEM)
    return pl.pallas_call(
        functools.partial(_scatter_kernel, tk=tk, tm=tm),
        grid_spec=pltpu.PrefetchScalarGridSpec(
            num_scalar_prefetch=3,
            grid=(n_tok // tk,),
            in_specs=[smem_tile, smem_tile, pl.BlockSpec((tk, d), lambda i, ps, pn, used: (i, 0))],
            out_specs=pl.BlockSpec(memory_space=pl.ANY),
            scratch_shapes=[pltpu.VMEM((tm, d), h2_flat.dtype), pltpu.SemaphoreType.DMA(()),
                            pltpu.SemaphoreType.DMA(())],
        ),
        out_shape=jax.ShapeDtypeStruct((n_rows, d), h2_flat.dtype),
        compiler_params=pltpu.CompilerParams(
            dimension_semantics=("arbitrary",), vmem_limit_bytes=VMEM_LIMIT),
        name="scatter",
    )(pad_start, pad_len, used_tiles, d0_tiles, d1_tiles, h2_flat)


def _expert_kernel(tv_ref, par_ref, glo_ref, ghi_ref, sege_ref, ntot_ref, last_ref, half_ref,
                   xs_ref, wg_hbm, wu_hbm, wd_hbm, ys_ref,
                   wgb, wub, wdb, sg, su, sd, sem, *, layer, nch):
    del last_ref
    i = pl.program_id(0)
    rg = wgb.shape[1] // nch
    rd = wdb.shape[1] // nch

    def chunk_copies(g):
        e = sege_ref[g // nch]
        c = g % nch
        s = g % 2
        rows_g = pl.ds(pl.multiple_of(c * rg, rg), rg)
        rows_d = pl.ds(pl.multiple_of(c * rd, rd), rd)
        return (_row_copy(wg_hbm.at[layer, e, rows_g], sg.at[s], sem.at[s]),
                _row_copy(wu_hbm.at[layer, e, rows_g], su.at[s], sem.at[s]),
                _row_copy(wd_hbm.at[layer, e, rows_d], sd.at[s], sem.at[s]))

    def start_chunk(g):
        for cp in chunk_copies(g):
            cp.start()

    def process_chunk(g, carry):
        for cp in chunk_copies(g):
            cp.wait()
        c = g % nch
        s = g % 2
        p = (g // nch) % 2
        wgb[p, pl.ds(pl.multiple_of(c * rg, rg), rg), :] = sg[s].astype(_BF16)
        wub[p, pl.ds(pl.multiple_of(c * rg, rg), rg), :] = su[s].astype(_BF16)
        wdb[p, pl.ds(pl.multiple_of(c * rd, rd), rd), :] = sd[s].astype(_BF16)

        @pl.when(g + 2 < ntot_ref[0])
        def _():
            start_chunk(g + 2)
        return carry

    @pl.when(i == 0)
    def _():
        start_chunk(0)
        start_chunk(1)
        lax.fori_loop(0, nch, process_chunk, 0)

    valid = tv_ref[i] == 1

    def tile_mlp(rows):
        p = par_ref[i]
        lo, hi = _unpack_bf16_pairs(xs_ref[0:rows, :])
        xb = jnp.concatenate([lo.astype(_BF16), hi.astype(_BF16)], axis=1)
        g = _dot(xb, wgb[p])
        u = _dot(xb, wub[p])
        hid = (g * jax.nn.sigmoid(g) * u).astype(_BF16)
        ys_ref[0:rows, :] = _pack_bf16_pairs(_dot(hid, wdb[p]))

    @pl.when(valid)
    def _():
        lax.fori_loop(glo_ref[i], ghi_ref[i], process_chunk, 0)
        tm = ys_ref.shape[0]
        half_only = half_ref[i] == 1

        @pl.when(jnp.logical_not(half_only))
        def _():
            tile_mlp(tm)

        @pl.when(half_only)
        def _():
            tile_mlp(tm // 2)
            ys_ref[tm // 2:tm, :] = jnp.zeros((tm - tm // 2, ys_ref.shape[1]), ys_ref.dtype)

    @pl.when(jnp.logical_not(valid))
    def _():
        ys_ref[...] = jnp.zeros(ys_ref.shape, ys_ref.dtype)


def _experts(xs, tables, w_gate, w_up, w_down, layer):
    n_rows, dp = xs.shape
    d = w_gate.shape[2]
    tm = TM_EXP
    f = w_gate.shape[3]
    nch = W_CHUNKS
    any_spec = pl.BlockSpec(memory_space=pl.ANY)
    return pl.pallas_call(
        functools.partial(_expert_kernel, layer=layer, nch=nch),
        grid_spec=pltpu.PrefetchScalarGridSpec(
            num_scalar_prefetch=8,
            grid=(n_rows // tm,),
            in_specs=[
                pl.BlockSpec((tm, dp), lambda i, *s: (jnp.minimum(i, s[6][0]), 0)),
                any_spec, any_spec, any_spec,
            ],
            out_specs=pl.BlockSpec((tm, dp), lambda i, *s: (i, 0)),
            scratch_shapes=[
                pltpu.VMEM((2, d, f), _BF16), pltpu.VMEM((2, d, f), _BF16), pltpu.VMEM((2, f, d), _BF16),
                pltpu.VMEM((2, d // nch, f), _F32), pltpu.VMEM((2, d // nch, f), _F32),
                pltpu.VMEM((2, f // nch, d), _F32), pltpu.SemaphoreType.DMA((2,)),
            ],
        ),
        out_shape=jax.ShapeDtypeStruct((n_rows, dp), jnp.uint32),
        compiler_params=pltpu.CompilerParams(
            dimension_semantics=("arbitrary",), vmem_limit_bytes=VMEM_LIMIT),
        name="experts",
    )(*tables, xs, w_gate, w_up, w_down)


def _start_row_gather(src_hbm, idx_ref, dst, sem, n_rows):
    for r in range(n_rows):
        _row_copy(src_hbm.at[pl.ds(idx_ref[0, 0, r], 1)], dst.at[pl.ds(r, 1)], sem).start(
            priority=r % 2)


def _wait_row_gather(src_hbm, dst, sem, n_rows):
    _row_copy(src_hbm.at[pl.ds(0, n_rows)], dst, sem).wait()


def _combine_kernel(d_cur, d_nxt, ys_hbm, x1_ref, route_ref, gp_ref, o_ref, rbuf, sem, *,
                    tk):
    i = pl.program_id(0)
    n = pl.num_programs(0)
    slot = i % 2

    @pl.when(i == 0)
    def _():
        _start_row_gather(ys_hbm, d_cur, rbuf.at[0], sem.at[0], TOP_K * tk)

    _wait_row_gather(ys_hbm, rbuf.at[slot], sem.at[slot], TOP_K * tk)

    @pl.when(i + 1 < n)
    def _():
        _start_row_gather(ys_hbm, d_nxt, rbuf.at[1 - slot], sem.at[1 - slot], TOP_K * tk)

    route = route_ref[0]
    lo0, hi0 = _unpack_bf16_pairs(rbuf[slot, 0:tk, :])
    lo1, hi1 = _unpack_bf16_pairs(rbuf[slot, tk:2 * tk, :])
    w0 = route[:, 2:3]
    w1 = route[:, 3:4]
    y = jnp.concatenate([w0 * lo0 + w1 * lo1, w0 * hi0 + w1 * hi1], axis=1)
    o_ref[0] = x1_ref[0] + y * _rms(y) * gp_ref[0]


def _combine(ys, dest_tiles, x1, route, gate_post):
    bsz, t, d = x1.shape
    tk = TM_CMB
    per_seq = t // tk
    n_tiles = bsz * per_seq
    smem_tile = lambda imap: pl.BlockSpec((1, 1, TOP_K * tk), imap, memory_space=pltpu.SMEM)
    return pl.pallas_call(
        functools.partial(_combine_kernel, tk=tk),
        grid=(n_tiles,),
        in_specs=[
            smem_tile(lambda i: (i, 0, 0)),
            smem_tile(lambda i: (jnp.minimum(i + 1, n_tiles - 1), 0, 0)),
            pl.BlockSpec(memory_space=pl.ANY),
            pl.BlockSpec((1, tk, d), lambda i: (i // per_seq, i % per_seq, 0)),
            pl.BlockSpec((1, tk, ROUTE_LANES), lambda i: (i // per_seq, i % per_seq, 0)),
            pl.BlockSpec((1, 1, d), lambda i: (i // per_seq, 0, 0)),
        ],
        out_specs=pl.BlockSpec((1, tk, d), lambda i: (i // per_seq, i % per_seq, 0)),
        out_shape=jax.ShapeDtypeStruct((bsz, t, d), _F32),
        scratch_shapes=[pltpu.VMEM((2, TOP_K * tk, d // 2), jnp.uint32),
                        pltpu.SemaphoreType.DMA((2,))],
        compiler_params=pltpu.CompilerParams(
            dimension_semantics=("arbitrary",), vmem_limit_bytes=VMEM_LIMIT),
        name="combine",
    )(dest_tiles, dest_tiles, ys, x1, route, gate_post)


def _segment_tables(counts, tm, n_tiles, n_steps, nch):
    n_experts = counts.shape[0]
    i32 = jnp.int32
    nt = (counts + tm - 1) // tm
    tend = jnp.cumsum(nt)
    tstart = tend - nt
    tile_idx = jnp.arange(n_tiles, dtype=i32)
    valid = tile_idx < tend[-1]
    nonempty = nt > 0
    seg_of_e = jnp.cumsum(nonempty.astype(i32)) - 1
    n_seg = seg_of_e[-1] + 1
    ids = jnp.arange(n_experts, dtype=i32)
    seg_expert = jnp.sum(
        jnp.where(nonempty[None, :] & (seg_of_e[None, :] == ids[:, None]), ids[None, :], 0), axis=1)
    owns = (tstart[None, :] <= tile_idx[:, None]) & (tile_idx[:, None] < tend[None, :])
    pick = lambda v: jnp.sum(jnp.where(owns, v[None, :], 0), axis=1)
    k = pick(seg_of_e)
    j = tile_idx - pick(tstart)
    ntt = jnp.maximum(pick(nt), 1)
    has_next = valid & (k + 1 < n_seg)
    glo = jnp.where(has_next, (k + 1) * nch + j * nch // ntt, 0)
    ghi = jnp.where(has_next, (k + 1) * nch + (j + 1) * nch // ntt, 0)
    real_rows = jnp.clip(pick(counts) - j * tm, 0, tm)
    half_only = valid & (real_rows <= tm // 2)
    expert_tables = (valid.astype(i32), (k % 2).astype(i32), glo.astype(i32), ghi.astype(i32),
                     seg_expert.astype(i32), (n_seg * nch).reshape(1).astype(i32),
                     (tend[-1:] - 1).astype(i32), half_only.astype(i32))
    pad_start = jnp.zeros((n_steps,), i32).at[:n_experts].set(tstart * tm + counts)
    pad_len = jnp.zeros((n_steps,), i32).at[:n_experts].set(nt * tm - counts)
    return expert_tables, pad_start, pad_len, tend[-1:].astype(i32)


def kernel(x, c, w_ada, b_ada, norm_mix_pre, norm_mix_post, w_in, conv_a, norm_a_out, conv_b, conv_b_bias, ln_b_gain, ln_b_bias, w_out, norm_moe_pre, norm_moe_post, w_router_group, b_router_group, w_router_expert, b_router_expert, w_expert_gate, w_expert_up, w_expert_down):
    bsz, t, d = x.shape
    depth = w_ada.shape[0]
    n_groups = w_router_group.shape[2]
    per_group = w_router_expert.shape[3]
    n_experts = n_groups * per_group
    n_tok = bsz * t
    n_assign = n_tok * TOP_K
    n_tiles = (n_assign + n_experts * (TM_EXP - 1) + TM_EXP - 1) // TM_EXP
    row = lambda v: v.reshape(1, -1)

    for l in range(depth):
        mod = _ada(c, w_ada[l], b_ada[l])
        shift1, scale1, gate1, shift2, scale2, gate2 = [
            m.reshape(bsz, 1, d) for m in jnp.split(mod, N_MOD, axis=-1)]

        y = _mix_in(x, shift1, norm_mix_pre[l] * (1.0 + scale1), w_in[l].astype(_BF16), conv_a[l],
                    row(norm_a_out[l]), conv_b[l], row(conv_b_bias[l]), row(ln_b_gain[l]),
                    row(ln_b_bias[l]))

        n_route = n_groups + n_experts
        w_fine = jnp.transpose(w_router_expert[l], (1, 0, 2)).reshape(d, n_experts)
        w_route = jnp.zeros((d, ROUTE_LANES), _F32).at[:, :n_route].set(
            jnp.concatenate([w_router_group[l], w_fine], axis=1)).astype(_BF16)
        b_route = jnp.zeros((1, ROUTE_LANES), _F32).at[0, :n_route].set(
            jnp.concatenate([b_router_group[l], b_router_expert[l].reshape(-1)]))
        x1, h2, route, route_t, counts_f = _mix_out(
            y, x, gate1 * norm_mix_post[l], shift2, norm_moe_pre[l] * (1.0 + scale2),
            w_out[l].astype(_BF16), w_route, b_route, n_groups, per_group)

        dest = _rank(route_t, counts_f, TM_EXP)
        counts = counts_f[:n_experts, 0].astype(jnp.int32)
        n_stiles = n_tok // TM_SCT
        assert n_stiles >= n_experts
        expert_tables, pad_start, pad_len, used_tiles = _segment_tables(
            counts, TM_EXP, n_tiles, n_stiles, W_CHUNKS)
        xs = _scatter(h2.reshape(n_tok, d // 2), dest[0].reshape(n_stiles, 1, TM_SCT),
                      dest[1].reshape(n_stiles, 1, TM_SCT), pad_start, pad_len, used_tiles,
                      n_tiles * TM_EXP)
        ys = _experts(xs, expert_tables, w_expert_gate, w_expert_up, w_expert_down, l)

        n_ctiles = n_tok // TM_CMB
        dest_tiles = jnp.concatenate([dest[0].reshape(n_ctiles, 1, TM_CMB),
                                      dest[1].reshape(n_ctiles, 1, TM_CMB)], axis=2)
        x = _combine(ys, dest_tiles, x1, route, gate2 * norm_moe_post[l])
    return x
```

```python
import functools

import jax
import jax.numpy as jnp
from jax import lax
from jax.experimental import pallas as pl
from jax.experimental.pallas import tpu as pltpu

EPS = 1e-6
HEAD_DIM = 128
KERNEL_A = 3
KERNEL_B = 31
N_MOD = 6
TOP_K = 2
ROUTE_LANES = 128
ROUTE_ROWS = 8
EXPERT_ROWS = 128
TQ_RANK = 1024
HIST_A = 8
HIST_B = 32
CONV_ROWS = 128
MIX_COLS = 256
OUT_PARTS = 2
W_CHUNKS = 8

TM_MIX = 512
TM_EXP = 256
TM_CMB = 256
TM_SCT = 512
ADA_COLS = 1024
VMEM_LIMIT = 56 * 1024 * 1024

_BF16 = jnp.bfloat16
_F32 = jnp.float32


def _dot(a, b):
    return jnp.dot(a, b, preferred_element_type=_F32)


def _rms(v):
    return lax.rsqrt(jnp.mean(v * v, axis=-1, keepdims=True) + EPS)


def _pack_bf16_pairs(v):
    n = v.shape[1] // 2
    bits = pltpu.bitcast(v.astype(_BF16).astype(_F32), jnp.uint32)
    return (bits[:, :n] >> 16) | bits[:, n:]


def _unpack_bf16_pairs(p):
    lo = pltpu.bitcast(p << 16, _F32)
    hi = pltpu.bitcast(p & jnp.uint32(0xFFFF0000), _F32)
    return lo, hi


def _ada_kernel(c_ref, w_ref, b_ref, o_ref):
    c = c_ref[...]
    act = (c * jax.nn.sigmoid(c)).astype(_BF16)
    o_ref[...] = _dot(act, w_ref[...].astype(_BF16)) + b_ref[...]


def _ada(c, w_ada, b_ada):
    bsz, d = c.shape
    n_out = w_ada.shape[1]
    c_pad = jnp.zeros((8, d), _F32).at[:bsz].set(c)
    out = pl.pallas_call(
        _ada_kernel,
        grid=(n_out // ADA_COLS,),
        in_specs=[
            pl.BlockSpec((8, d), lambda j: (0, 0)),
            pl.BlockSpec((d, ADA_COLS), lambda j: (0, j)),
            pl.BlockSpec((1, ADA_COLS), lambda j: (0, j)),
        ],
        out_specs=pl.BlockSpec((8, ADA_COLS), lambda j: (0, j)),
        out_shape=jax.ShapeDtypeStruct((8, n_out), _F32),
        compiler_params=pltpu.CompilerParams(
            dimension_semantics=("arbitrary",), vmem_limit_bytes=VMEM_LIMIT),
        name="ada",
    )(c_pad, w_ada, b_ada.reshape(1, n_out))
    return out[:bsz]


def _dwconv_chunk(ext_ref, w_ref, r0, rows, lo, hi, hist, ksize):
    span = rows + hist
    first = hist - (ksize - 1)
    xe = ext_ref[r0:r0 + span, lo:hi]
    acc = None
    for b in range(8):
        offsets = [o for o in range(first, hist + 1) if o % 8 == b]
        if not offsets:
            continue
        shifted = xe if b == 0 else pltpu.roll(xe, span - b, 0)
        for o in offsets:
            k = o - first
            term = w_ref[k:k + 1, lo:hi] * shifted[o - b:o - b + rows, :]
            acc = term if acc is None else acc + term
    return acc


def _mix_in_kernel(x_ref, sh_ref, gs_ref, w_ref, ca_ref, na_ref, cb_ref, cbb_ref,
                   lg_ref, lb_ref, y_ref, cv_ext, u_ext, ba_buf, conv_b_out, *, tm, da, db):
    nc = MIX_COLS
    @pl.when(pl.program_id(1) == 0)
    def _():
        cv_ext[...] = jnp.zeros(cv_ext.shape, _F32)
        u_ext[0:HIST_B, :] = jnp.zeros((HIST_B, db), _F32)
        ba_buf[...] = jnp.zeros(ba_buf.shape, _F32)
        conv_b_out[...] = jnp.zeros(conv_b_out.shape, _F32)

    def finish_head(hd):
        lo, hi = hd * HEAD_DIM, (hd + 1) * HEAD_DIM
        for r0 in range(0, tm, CONV_ROWS):
            conv = _dwconv_chunk(cv_ext, ca_ref, r0, CONV_ROWS, lo, hi, HIST_A, KERNEL_A)
            seg = ba_buf[r0:r0 + CONV_ROWS, lo:hi] * conv
            y_ref[0, r0:r0 + CONV_ROWS, lo:hi] = (seg * _rms(seg) * na_ref[:, lo:hi]).astype(_BF16)

    def finish_rows(r0, rows):
        acc = conv_b_out[r0:r0 + rows, :]
        mu = jnp.mean(acc, axis=-1, keepdims=True)
        cen = acc - mu
        var = jnp.mean(cen * cen, axis=-1, keepdims=True)
        yn = cen * lax.rsqrt(var + EPS) * lg_ref[...] + lb_ref[...]
        y_ref[0, r0:r0 + rows, da:da + db] = (yn * jax.nn.sigmoid(yn)).astype(_BF16)

    last = pl.num_programs(1) - 1

    @pl.when(pl.program_id(1) == last)
    def _():
        for hd in range(da // HEAD_DIM):
            finish_head(hd)
        finish_rows(0, tm)

    @pl.when(pl.program_id(1) < last)
    def _():
        x = x_ref[0]
        h = x * _rms(x) * gs_ref[0] + sh_ref[0]
        hb = h.astype(_BF16)

        n_b = db // nc
        heads_per = (da // HEAD_DIM) // n_b
        ln_rows = tm // n_b
        for j in range(n_b):
            lo = j * nc
            val = _dot(hb, w_ref[:, 3 * da + lo:3 * da + lo + nc])
            gate = _dot(hb, w_ref[:, 3 * da + db + lo:3 * da + db + lo + nc])
            for hd in range(j * heads_per, (j + 1) * heads_per):
                finish_head(hd)
            finish_rows(j * ln_rows, ln_rows)
            u_ext[HIST_B:HIST_B + tm, lo:lo + nc] = val * jax.nn.sigmoid(gate)
        cv_ext[0:HIST_A, :] = cv_ext[tm:tm + HIST_A, :]

        conv_chunks = [(lo, r0) for lo in range(0, db, HEAD_DIM) for r0 in range(0, tm, CONV_ROWS)]
        n_a = da // nc
        n_mm = 3 * n_a
        done = 0
        for q in range(n_mm):
            j = q // 2 if q < 2 * n_a else q - 2 * n_a
            lo = j * nc
            if q < 2 * n_a and q % 2 == 0:
                c_part = _dot(hb, w_ref[:, da + lo:da + lo + nc])
            elif q < 2 * n_a:
                v_part = _dot(hb, w_ref[:, 2 * da + lo:2 * da + lo + nc])
                cv_ext[HIST_A:HIST_A + tm, lo:lo + nc] = c_part * v_part
            else:
                ba_buf[:, lo:lo + nc] = _dot(hb, w_ref[:, lo:lo + nc])
            upto = (q + 1) * len(conv_chunks) // n_mm
            for lo_c, r0 in conv_chunks[done:upto]:
                conv = _dwconv_chunk(u_ext, cb_ref, r0, CONV_ROWS, lo_c, lo_c + HEAD_DIM, HIST_B, KERNEL_B)
                conv_b_out[r0:r0 + CONV_ROWS, lo_c:lo_c + HEAD_DIM] = conv + cbb_ref[:, lo_c:lo_c + HEAD_DIM]
            done = upto
        u_ext[0:HIST_B, :] = u_ext[tm:tm + HIST_B, :]


def _mix_in(x, shift, gain_scale, w_in_bf, conv_a, norm_a, conv_b, conv_b_bias, ln_g, ln_b):
    bsz, t, d = x.shape
    da = conv_a.shape[1]
    db = conv_b.shape[1]
    tm = TM_MIX
    full = lambda shape: pl.BlockSpec(shape, lambda b, s: (0,) * len(shape))
    per_b = pl.BlockSpec((1, 1, d), lambda b, s: (b, 0, 0))
    n_t = t // tm
    return pl.pallas_call(
        functools.partial(_mix_in_kernel, tm=tm, da=da, db=db),
        grid=(bsz, n_t + 1),
        in_specs=[
            pl.BlockSpec((1, tm, d), lambda b, s: (b, jnp.minimum(s, n_t - 1), 0)),
            per_b, per_b,
            pl.BlockSpec(w_in_bf.shape, lambda b, s: (0, 0), pipeline_mode=pl.Buffered(1)),
            full((KERNEL_A, da)), full((1, da)),
            full((KERNEL_B, db)), full((1, db)), full((1, db)), full((1, db)),
        ],
        out_specs=pl.BlockSpec((1, tm, da + db), lambda b, s: (b, jnp.maximum(s - 1, 0), 0)),
        out_shape=jax.ShapeDtypeStruct((bsz, t, da + db), _BF16),
        scratch_shapes=[pltpu.VMEM((HIST_A + tm, da), _F32), pltpu.VMEM((HIST_B + tm, db), _F32),
                        pltpu.VMEM((tm, da), _F32), pltpu.VMEM((tm, db), _F32)],
        compiler_params=pltpu.CompilerParams(
            dimension_semantics=("arbitrary", "arbitrary"), vmem_limit_bytes=VMEM_LIMIT),
        name="mix_in",
    )(x, shift, gain_scale, w_in_bf, conv_a, norm_a, conv_b, conv_b_bias, ln_g, ln_b)


def _route_tile(hb, wr, br, n_groups, per_group):
    logits = _dot(hb, wr) + br
    lane = lax.broadcasted_iota(jnp.int32, logits.shape, 1).astype(_F32)
    neg = jnp.float32(-jnp.inf)
    big = jnp.float32(ROUTE_LANES)
    is_coarse = lane < n_groups
    coarse = jnp.where(is_coarse, logits, neg)
    m = jnp.max(coarse, axis=-1, keepdims=True)
    g_sel = jnp.min(jnp.where(coarse == m, lane, big), axis=-1, keepdims=True)
    p_g = 1.0 / jnp.sum(jnp.where(is_coarse, jnp.exp(logits - m), 0.0), axis=-1, keepdims=True)
    lo = n_groups + g_sel * per_group
    fine = jnp.where((lane >= lo) & (lane < lo + per_group), logits, neg)
    v1 = jnp.max(fine, axis=-1, keepdims=True)
    l1 = jnp.min(jnp.where(fine == v1, lane, big), axis=-1, keepdims=True)
    fine2 = jnp.where(lane == l1, neg, fine)
    v2 = jnp.max(fine2, axis=-1, keepdims=True)
    l2 = jnp.min(jnp.where(fine2 == v2, lane, big), axis=-1, keepdims=True)
    e2 = jnp.exp(v2 - v1)
    w1 = p_g / (1.0 + e2)
    w2 = p_g * e2 / (1.0 + e2)
    route = jnp.where(lane == 0, l1 - n_groups,
                      jnp.where(lane == 1, l2 - n_groups,
                                jnp.where(lane == 2, w1, jnp.where(lane == 3, w2, 0.0))))
    return route


def _mix_out_kernel(y_ref, x_ref, gp_ref, sh2_ref, gs_ref, wout_ref, wr_ref, br_ref,
                    x1_ref, h2_ref, route_ref, route_t_ref, cnt_ref, o_a, o_b, hb_buf, *, tm,
                    n_groups, per_group):
    part = tm // OUT_PARTS
    nc = MIX_COLS
    n_chunks = wout_ref.shape[1] // nc
    rows = part // n_chunks
    bufs = (o_a, o_b)

    def finish_rows(o_part, base, q):
        r0 = q * rows
        o = o_part[r0:r0 + rows, :]
        x1 = x_ref[0, base + r0:base + r0 + rows, :] + o * _rms(o) * gp_ref[0]
        x1_ref[0, base + r0:base + r0 + rows, :] = x1
        h2 = x1 * _rms(x1) * gs_ref[0] + sh2_ref[0]
        h2_ref[0, base + r0:base + r0 + rows, :] = _pack_bf16_pairs(h2)
        hb_buf[base + r0:base + r0 + rows, :] = h2.astype(_BF16)

    for k in range(OUT_PARTS + 1):
        y_part = y_ref[0, k * part:(k + 1) * part, :] if k < OUT_PARTS else None
        for q in range(n_chunks):
            if k < OUT_PARTS:
                bufs[k % 2][:, q * nc:(q + 1) * nc] = _dot(y_part, wout_ref[:, q * nc:(q + 1) * nc])
            if k > 0:
                finish_rows(bufs[(k - 1) % 2], (k - 1) * part, q)
    route = _route_tile(hb_buf[...], wr_ref[...], br_ref[...], n_groups, per_group)
    route_ref[0] = route
    route_t = jnp.transpose(route)[0:ROUTE_ROWS, :]
    route_t_ref[...] = route_t

    @pl.when(jnp.logical_and(pl.program_id(0) == 0, pl.program_id(1) == 0))
    def _():
        cnt_ref[...] = jnp.zeros(cnt_ref.shape, _F32)

    sub = lax.broadcasted_iota(jnp.int32, (EXPERT_ROWS, tm), 0).astype(_F32)
    chosen = jnp.where(sub == route_t[0:1, :], 1.0, 0.0) + jnp.where(sub == route_t[1:2, :], 1.0, 0.0)
    cnt_ref[...] = cnt_ref[...] + jnp.sum(chosen, axis=1, keepdims=True)


def _mix_out(y, x, gate_post, shift2, gain_scale, w_out_bf, w_route, b_route, n_groups, per_group):
    bsz, t, d = x.shape
    tm = TM_MIX
    full = lambda shape: pl.BlockSpec(shape, lambda b, s: (0,) * len(shape))
    per_b = pl.BlockSpec((1, 1, d), lambda b, s: (b, 0, 0))
    tile = pl.BlockSpec((1, tm, d), lambda b, s: (b, s, 0))
    return pl.pallas_call(
        functools.partial(_mix_out_kernel, tm=tm, n_groups=n_groups, per_group=per_group),
        grid=(bsz, t // tm),
        in_specs=[
            tile, tile, per_b, per_b, per_b,
            pl.BlockSpec(w_out_bf.shape, lambda b, s: (0, 0), pipeline_mode=pl.Buffered(1)),
            full((d, ROUTE_LANES)), full((1, ROUTE_LANES)),
        ],
        out_specs=[tile, pl.BlockSpec((1, tm, d // 2), lambda b, s: (b, s, 0)),
                   pl.BlockSpec((1, tm, ROUTE_LANES), lambda b, s: (b, s, 0)),
                   pl.BlockSpec((ROUTE_ROWS, tm), lambda b, s: (0, b * (t // tm) + s)),
                   full((EXPERT_ROWS, ROUTE_LANES))],
        out_shape=[
            jax.ShapeDtypeStruct((bsz, t, d), _F32),
            jax.ShapeDtypeStruct((bsz, t, d // 2), jnp.uint32),
            jax.ShapeDtypeStruct((bsz, t, ROUTE_LANES), _F32),
            jax.ShapeDtypeStruct((ROUTE_ROWS, bsz * t), _F32),
            jax.ShapeDtypeStruct((EXPERT_ROWS, ROUTE_LANES), _F32),
        ],
        scratch_shapes=[pltpu.VMEM((tm // OUT_PARTS, d), _F32), pltpu.VMEM((tm // OUT_PARTS, d), _F32),
                        pltpu.VMEM((tm, d), _BF16)],
        compiler_params=pltpu.CompilerParams(
            dimension_semantics=("arbitrary", "arbitrary"), vmem_limit_bytes=VMEM_LIMIT),
        name="mix_out",
    )(y, x, gate_post, shift2, gain_scale, w_out_bf, w_route, b_route)


def _rank_kernel(rt_ref, cnt_ref, u_ref, dest_ref, carry, pbase, *, tq, tm):
    sub = lax.broadcasted_iota(jnp.int32, (EXPERT_ROWS, tq), 0).astype(_F32)
    oh0 = sub == rt_ref[0:1, :]
    oh1 = sub == rt_ref[1:2, :]
    both = jnp.where(oh0, 1.0, 0.0) + jnp.where(oh1, 1.0, 0.0)

    @pl.when(pl.program_id(0) == 0)
    def _():
        tiles = jnp.floor((cnt_ref[...] + (tm - 1)) * (1.0 / tm))
        hi = jnp.floor(tiles * (1.0 / 16.0))
        lo = tiles - 16.0 * hi
        r = lax.broadcasted_iota(jnp.int32, (EXPERT_ROWS, EXPERT_ROWS), 0)
        c = lax.broadcasted_iota(jnp.int32, (EXPERT_ROWS, EXPERT_ROWS), 1)
        before = jnp.where(c < r, 1.0, 0.0).astype(_BF16)
        start_tiles = 16.0 * _dot(before, hi.astype(_BF16)) + _dot(before, lo.astype(_BF16))
        pbase[...] = start_tiles * tm
        carry[...] = jnp.zeros(carry.shape, _F32)

    earlier = _dot(both.astype(_BF16), u_ref[...])
    slot = earlier + carry[:, 0:1] + pbase[:, 0:1]
    d0 = jnp.sum(jnp.where(oh0, slot, 0.0), axis=0, keepdims=True)
    d1 = jnp.sum(jnp.where(oh1, slot, 0.0), axis=0, keepdims=True)
    row = lax.broadcasted_iota(jnp.int32, (ROUTE_ROWS, tq), 0)
    dest_ref[...] = jnp.where(row == 0, d0, jnp.where(row == 1, d1, 0.0)).astype(jnp.int32)
    carry[...] = carry[...] + jnp.sum(both, axis=1, keepdims=True)


def _rank(route_t, counts, tm):
    n_tok = route_t.shape[1]
    tq = TQ_RANK
    assert tm & (tm - 1) == 0, "segment padding must be a power of two for exact f32 arithmetic"
    idx = jnp.arange(tq, dtype=jnp.int32)
    earlier_mask = (idx[:, None] < idx[None, :]).astype(_BF16)
    return pl.pallas_call(
        functools.partial(_rank_kernel, tq=tq, tm=tm),
        grid=(n_tok // tq,),
        in_specs=[
            pl.BlockSpec((ROUTE_ROWS, tq), lambda j: (0, j)),
            pl.BlockSpec((EXPERT_ROWS, ROUTE_LANES), lambda j: (0, 0)),
            pl.BlockSpec((tq, tq), lambda j: (0, 0)),
        ],
        out_specs=pl.BlockSpec((ROUTE_ROWS, tq), lambda j: (0, j)),
        out_shape=jax.ShapeDtypeStruct((ROUTE_ROWS, n_tok), jnp.int32),
        scratch_shapes=[pltpu.VMEM((EXPERT_ROWS, ROUTE_LANES), _F32),
                        pltpu.VMEM((EXPERT_ROWS, ROUTE_LANES), _F32)],
        compiler_params=pltpu.CompilerParams(
            dimension_semantics=("arbitrary",), vmem_limit_bytes=VMEM_LIMIT),
        name="rank",
    )(route_t, counts, earlier_mask)


def _row_copy(src, dst, sem):
    return pltpu.make_async_copy(src, dst, sem)


def _scatter_kernel(ps_ref, pn_ref, used_ref, d0_ref, d1_ref, h_ref, xs_hbm, zrow, sem, psem, *, tk,
                    tm):
    i = pl.program_id(0)

    @pl.when(i == 0)
    def _():
        zrow[...] = jnp.zeros(zrow.shape, zrow.dtype)

    @pl.when(i == pl.num_programs(0) - 1)
    def _():
        n_rows = xs_hbm.shape[0]

        def fill(t, carry):
            dst = xs_hbm.at[pl.ds(pl.multiple_of(t * tm, tm), tm)]
            _row_copy(zrow, dst, psem).start()
            _row_copy(zrow, dst, psem).wait()
            return carry

        lax.fori_loop(used_ref[0], n_rows // tm, fill, 0)

    for r in range(tk):
        src = h_ref.at[pl.ds(r, 1)]
        _row_copy(src, xs_hbm.at[pl.ds(d0_ref[0, 0, r], 1)], sem).start(priority=0)
        _row_copy(src, xs_hbm.at[pl.ds(d1_ref[0, 0, r], 1)], sem).start(priority=1)

    pad_start = ps_ref[i]
    n_pad = pn_ref[i]
    zsrc = zrow.at[pl.ds(0, 1)]

    def start_pad(r, carry):
        _row_copy(zsrc, xs_hbm.at[pl.ds(pad_start + r, 1)], psem).start()
        return carry

    def wait_pad(r, carry):
        _row_copy(zsrc, xs_hbm.at[pl.ds(0, 1)], psem).wait()
        return carry

    lax.fori_loop(0, n_pad, start_pad, 0)
    lax.fori_loop(0, n_pad, wait_pad, 0)
    for _ in range(TOP_K):
        _row_copy(h_ref, xs_hbm.at[pl.ds(0, tk)], sem).wait()


def _scatter(h2_flat, d0_tiles, d1_tiles, pad_start, pad_len, used_tiles, n_rows):
    n_tok, d = h2_flat.shape
    tk = TM_SCT
    tm = TM_EXP
    smem_tile = pl.BlockSpec((1, 1, tk), lambda i, ps, pn, used: (i, 0, 0), memory_space=pltpu.SMEM)
    return pl.pallas_call(
        functools.partial(_scatter_kernel, tk=tk, tm=tm),
        grid_spec=pltpu.PrefetchScalarGridSpec(
            num_scalar_prefetch=3,
            grid=(n_tok // tk,),
            in_specs=[smem_tile, smem_tile, pl.BlockSpec((tk, d), lambda i, ps, pn, used: (i, 0))],
            out_specs=pl.BlockSpec(memory_space=pl.ANY),
            scratch_shapes=[pltpu.VMEM((tm, d), h2_flat.dtype), pltpu.SemaphoreType.DMA(()),
                            pltpu.SemaphoreType.DMA(())],
        ),
        out_shape=jax.ShapeDtypeStruct((n_rows, d), h2_flat.dtype),
        compiler_params=pltpu.CompilerParams(
            dimension_semantics=("arbitrary",), vmem_limit_bytes=VMEM_LIMIT),
        name="scatter",
    )(pad_start, pad_len, used_tiles, d0_tiles, d1_tiles, h2_flat)


def _expert_kernel(tv_ref, par_ref, glo_ref, ghi_ref, sege_ref, ntot_ref, last_ref, half_ref,
                   xs_ref, wg_hbm, wu_hbm, wd_hbm, ys_ref,
                   wgb, wub, wdb, sg, su, sd, sem, *, layer, nch):
    del last_ref
    i = pl.program_id(0)
    rg = wgb.shape[1] // nch
    rd = wdb.shape[1] // nch

    def chunk_copies(g):
        e = sege_ref[g // nch]
        c = g % nch
        s = g % 2
        rows_g = pl.ds(pl.multiple_of(c * rg, rg), rg)
        rows_d = pl.ds(pl.multiple_of(c * rd, rd), rd)
        return (_row_copy(wg_hbm.at[layer, e, rows_g], sg.at[s], sem.at[s]),
                _row_copy(wu_hbm.at[layer, e, rows_g], su.at[s], sem.at[s]),
                _row_copy(wd_hbm.at[layer, e, rows_d], sd.at[s], sem.at[s]))

    def start_chunk(g):
        for cp in chunk_copies(g):
            cp.start()

    def process_chunk(g, carry):
        for cp in chunk_copies(g):
            cp.wait()
        c = g % nch
        s = g % 2
        p = (g // nch) % 2
        wgb[p, pl.ds(pl.multiple_of(c * rg, rg), rg), :] = sg[s].astype(_BF16)
        wub[p, pl.ds(pl.multiple_of(c * rg, rg), rg), :] = su[s].astype(_BF16)
        wdb[p, pl.ds(pl.multiple_of(c * rd, rd), rd), :] = sd[s].astype(_BF16)

        @pl.when(g + 2 < ntot_ref[0])
        def _():
            start_chunk(g + 2)
        return carry

    @pl.when(i == 0)
    def _():
        start_chunk(0)
        start_chunk(1)
        lax.fori_loop(0, nch, process_chunk, 0)

    valid = tv_ref[i] == 1

    def tile_mlp(rows):
        p = par_ref[i]
        lo, hi = _unpack_bf16_pairs(xs_ref[0:rows, :])
        xb = jnp.concatenate([lo.astype(_BF16), hi.astype(_BF16)], axis=1)
        g = _dot(xb, wgb[p])
        u = _dot(xb, wub[p])
        hid = (g * jax.nn.sigmoid(g) * u).astype(_BF16)
        ys_ref[0:rows, :] = _pack_bf16_pairs(_dot(hid, wdb[p]))

    @pl.when(valid)
    def _():
        lax.fori_loop(glo_ref[i], ghi_ref[i], process_chunk, 0)
        tm = ys_ref.shape[0]
        half_only = half_ref[i] == 1

        @pl.when(jnp.logical_not(half_only))
        def _():
            tile_mlp(tm)

        @pl.when(half_only)
        def _():
            tile_mlp(tm // 2)
            ys_ref[tm // 2:tm, :] = jnp.zeros((tm - tm // 2, ys_ref.shape[1]), ys_ref.dtype)

    @pl.when(jnp.logical_not(valid))
    def _():
        ys_ref[...] = jnp.zeros(ys_ref.shape, ys_ref.dtype)


def _experts(xs, tables, w_gate, w_up, w_down, layer):
    n_rows, dp = xs.shape
    d = w_gate.shape[2]
    tm = TM_EXP
    f = w_gate.shape[3]
    nch = W_CHUNKS
    any_spec = pl.BlockSpec(memory_space=pl.ANY)
    return pl.pallas_call(
        functools.partial(_expert_kernel, layer=layer, nch=nch),
        grid_spec=pltpu.PrefetchScalarGridSpec(
            num_scalar_prefetch=8,
            grid=(n_rows // tm,),
            in_specs=[
                pl.BlockSpec((tm, dp), lambda i, *s: (jnp.minimum(i, s[6][0]), 0)),
                any_spec, any_spec, any_spec,
            ],
            out_specs=pl.BlockSpec((tm, dp), lambda i, *s: (i, 0)),
            scratch_shapes=[
                pltpu.VMEM((2, d, f), _BF16), pltpu.VMEM((2, d, f), _BF16), pltpu.VMEM((2, f, d), _BF16),
                pltpu.VMEM((2, d // nch, f), _F32), pltpu.VMEM((2, d // nch, f), _F32),
                pltpu.VMEM((2, f // nch, d), _F32), pltpu.SemaphoreType.DMA((2,)),
            ],
        ),
        out_shape=jax.ShapeDtypeStruct((n_rows, dp), jnp.uint32),
        compiler_params=pltpu.CompilerParams(
            dimension_semantics=("arbitrary",), vmem_limit_bytes=VMEM_LIMIT),
        name="experts",
    )(*tables, xs, w_gate, w_up, w_down)


def _start_row_gather(src_hbm, idx_ref, dst, sem, n_rows):
    for r in range(n_rows):
        _row_copy(src_hbm.at[pl.ds(idx_ref[0, 0, r], 1)], dst.at[pl.ds(r, 1)], sem).start(
            priority=r % 2)


def _wait_row_gather(src_hbm, dst, sem, n_rows):
    _row_copy(src_hbm.at[pl.ds(0, n_rows)], dst, sem).wait()


def _combine_kernel(d_cur, d_nxt, ys_hbm, x1_ref, route_ref, gp_ref, o_ref, rbuf, sem, *,
                    tk):
    i = pl.program_id(0)
    n = pl.num_programs(0)
    slot = i % 2

    @pl.when(i == 0)
    def _():
        _start_row_gather(ys_hbm, d_cur, rbuf.at[0], sem.at[0], TOP_K * tk)

    _wait_row_gather(ys_hbm, rbuf.at[slot], sem.at[slot], TOP_K * tk)

    @pl.when(i + 1 < n)
    def _():
        _start_row_gather(ys_hbm, d_nxt, rbuf.at[1 - slot], sem.at[1 - slot], TOP_K * tk)

    route = route_ref[0]
    lo0, hi0 = _unpack_bf16_pairs(rbuf[slot, 0:tk, :])
    lo1, hi1 = _unpack_bf16_pairs(rbuf[slot, tk:2 * tk, :])
    w0 = route[:, 2:3]
    w1 = route[:, 3:4]
    y = jnp.concatenate([w0 * lo0 + w1 * lo1, w0 * hi0 + w1 * hi1], axis=1)
    o_ref[0] = x1_ref[0] + y * _rms(y) * gp_ref[0]


def _combine(ys, dest_tiles, x1, route, gate_post):
    bsz, t, d = x1.shape
    tk = TM_CMB
    per_seq = t // tk
    n_tiles = bsz * per_seq
    smem_tile = lambda imap: pl.BlockSpec((1, 1, TOP_K * tk), imap, memory_space=pltpu.SMEM)
    return pl.pallas_call(
        functools.partial(_combine_kernel, tk=tk),
        grid=(n_tiles,),
        in_specs=[
            smem_tile(lambda i: (i, 0, 0)),
            smem_tile(lambda i: (jnp.minimum(i + 1, n_tiles - 1), 0, 0)),
            pl.BlockSpec(memory_space=pl.ANY),
            pl.BlockSpec((1, tk, d), lambda i: (i // per_seq, i % per_seq, 0)),
            pl.BlockSpec((1, tk, ROUTE_LANES), lambda i: (i // per_seq, i % per_seq, 0)),
            pl.BlockSpec((1, 1, d), lambda i: (i // per_seq, 0, 0)),
        ],
        out_specs=pl.BlockSpec((1, tk, d), lambda i: (i // per_seq, i % per_seq, 0)),
        out_shape=jax.ShapeDtypeStruct((bsz, t, d), _F32),
        scratch_shapes=[pltpu.VMEM((2, TOP_K * tk, d // 2), jnp.uint32),
                        pltpu.SemaphoreType.DMA((2,))],
        compiler_params=pltpu.CompilerParams(
            dimension_semantics=("arbitrary",), vmem_limit_bytes=VMEM_LIMIT),
        name="combine",
    )(dest_tiles, dest_tiles, ys, x1, route, gate_post)


def _segment_tables(counts, tm, n_tiles, n_steps, nch):
    n_experts = counts.shape[0]
    i32 = jnp.int32
    nt = (counts + tm - 1) // tm
    tend = jnp.cumsum(nt)
    tstart = tend - nt
    tile_idx = jnp.arange(n_tiles, dtype=i32)
    valid = tile_idx < tend[-1]
    nonempty = nt > 0
    seg_of_e = jnp.cumsum(nonempty.astype(i32)) - 1
    n_seg = seg_of_e[-1] + 1
    ids = jnp.arange(n_experts, dtype=i32)
    seg_expert = jnp.sum(
        jnp.where(nonempty[None, :] & (seg_of_e[None, :] == ids[:, None]), ids[None, :], 0), axis=1)
    owns = (tstart[None, :] <= tile_idx[:, None]) & (tile_idx[:, None] < tend[None, :])
    pick = lambda v: jnp.sum(jnp.where(owns, v[None, :], 0), axis=1)
    k = pick(seg_of_e)
    j = tile_idx - pick(tstart)
    ntt = jnp.maximum(pick(nt), 1)
    has_next = valid & (k + 1 < n_seg)
    glo = jnp.where(has_next, (k + 1) * nch + j * nch // ntt, 0)
    ghi = jnp.where(has_next, (k + 1) * nch + (j + 1) * nch // ntt, 0)
    real_rows = jnp.clip(pick(counts) - j * tm, 0, tm)
    half_only = valid & (real_rows <= tm // 2)
    expert_tables = (valid.astype(i32), (k % 2).astype(i32), glo.astype(i32), ghi.astype(i32),
                     seg_expert.astype(i32), (n_seg * nch).reshape(1).astype(i32),
                     (tend[-1:] - 1).astype(i32), half_only.astype(i32))
    pad_start = jnp.zeros((n_steps,), i32).at[:n_experts].set(tstart * tm + counts)
    pad_len = jnp.zeros((n_steps,), i32).at[:n_experts].set(nt * tm - counts)
    return expert_tables, pad_start, pad_len, tend[-1:].astype(i32)


def kernel(x, c, w_ada, b_ada, norm_mix_pre, norm_mix_post, w_in, conv_a, norm_a_out, conv_b, conv_b_bias, ln_b_gain, ln_b_bias, w_out, norm_moe_pre, norm_moe_post, w_router_group, b_router_group, w_router_expert, b_router_expert, w_expert_gate, w_expert_up, w_expert_down):
    bsz, t, d = x.shape
    depth = w_ada.shape[0]
    n_groups = w_router_group.shape[2]
    per_group = w_router_expert.shape[3]
    n_experts = n_groups * per_group
    n_tok = bsz * t
    n_assign = n_tok * TOP_K
    n_tiles = (n_assign + n_experts * (TM_EXP - 1) + TM_EXP - 1) // TM_EXP
    row = lambda v: v.reshape(1, -1)

    for l in range(depth):
        mod = _ada(c, w_ada[l], b_ada[l])
        shift1, scale1, gate1, shift2, scale2, gate2 = [
            m.reshape(bsz, 1, d) for m in jnp.split(mod, N_MOD, axis=-1)]

        y = _mix_in(x, shift1, norm_mix_pre[l] * (1.0 + scale1), w_in[l].astype(_BF16), conv_a[l],
                    row(norm_a_out[l]), conv_b[l], row(conv_b_bias[l]), row(ln_b_gain[l]),
                    row(ln_b_bias[l]))

        n_route = n_groups + n_experts
        w_fine = jnp.transpose(w_router_expert[l], (1, 0, 2)).reshape(d, n_experts)
        w_route = jnp.zeros((d, ROUTE_LANES), _F32).at[:, :n_route].set(
            jnp.concatenate([w_router_group[l], w_fine], axis=1)).astype(_BF16)
        b_route = jnp.zeros((1, ROUTE_LANES), _F32).at[0, :n_route].set(
            jnp.concatenate([b_router_group[l], b_router_expert[l].reshape(-1)]))
        x1, h2, route, route_t, counts_f = _mix_out(
            y, x, gate1 * norm_mix_post[l], shift2, norm_moe_pre[l] * (1.0 + scale2),
            w_out[l].astype(_BF16), w_route, b_route, n_groups, per_group)

        dest = _rank(route_t, counts_f, TM_EXP)
        counts = counts_f[:n_experts, 0].astype(jnp.int32)
        n_stiles = n_tok // TM_SCT
        assert n_stiles >= n_experts
        expert_tables, pad_start, pad_len, used_tiles = _segment_tables(
            counts, TM_EXP, n_tiles, n_stiles, W_CHUNKS)
        xs = _scatter(h2.reshape(n_tok, d // 2), dest[0].reshape(n_stiles, 1, TM_SCT),
                      dest[1].reshape(n_stiles, 1, TM_SCT), pad_start, pad_len, used_tiles,
                      n_tiles * TM_EXP)
        ys = _experts(xs, expert_tables, w_expert_gate, w_expert_up, w_expert_down, l)

        n_ctiles = n_tok // TM_CMB
        dest_tiles = jnp.concatenate([dest[0].reshape(n_ctiles, 1, TM_CMB),
                                      dest[1].reshape(n_ctiles, 1, TM_CMB)], axis=2)
        x = _combine(ys, dest_tiles, x1, route, gate2 * norm_moe_post[l])
    return x
```
